```python
import jax, jax.numpy as jnp
from jax import lax
import numpy as np

D_MODEL = 4096
BATCH = 2
SEQ = 8192
DEPTH = 1
DEC_BATCH = 16
DEC_SEQ = 64
PAST_LEN = 4096

CHUNK = 64
D_MIX = D_MODEL
N_HEADS = 16
D_NOPE = 128
D_ROPE = 64
D_QK = D_NOPE + D_ROPE
D_V = 128
D_ATT = N_HEADS * D_V
Q_LORA = 1024
KV_LORA = 512
D_POOL = D_MIX - D_ATT
POOL_WINDOWS = (2, 4, 8, 16)
N_POOL_GROUPS = len(POOL_WINDOWS)
D_POOL_GROUP = D_POOL // N_POOL_GROUPS
POOL_BUF = max(POOL_WINDOWS) - 1
ROPE_THETA = 10000.0
EPS = 1e-6
Q_BLOCK = 128
SPLITS = (Q_LORA, KV_LORA, D_ROPE, D_ATT, D_POOL, D_POOL)
D_IN = sum(SPLITS)

kernel_name = 'hymba_mla_multiscale_pool_stream_step'


def rmsnorm(x, g):
    x32 = x.astype(jnp.float32)
    ms = jnp.mean(x32 * x32, axis=-1, keepdims=True)
    return (x32 * lax.rsqrt(ms + EPS) * g.astype(jnp.float32)).astype(x.dtype)


def rope(x, pos):
    half = D_ROPE // 2
    freqs = ROPE_THETA ** (-jnp.arange(half, dtype=jnp.float32) / half)
    ang = pos.astype(jnp.float32)[:, None] * freqs[None, :]
    ang = ang.reshape((ang.shape[0],) + (1,) * (x.ndim - 3) + (half,))
    cos = jnp.cos(ang).astype(x.dtype)
    sin = jnp.sin(ang).astype(x.dtype)
    x1, x2 = x[..., :half], x[..., half:]
    return jnp.concatenate([x1 * cos - x2 * sin, x2 * cos + x1 * sin], axis=-1)


def chunk_causal_attention(q_nope, q_rope, k_nope, k_rope, v, q_pos, k_pos):
    scale = D_QK ** -0.5
    k_chunk = k_pos // CHUNK

    def block(args):
        qn, qr, qp = args
        s = (jnp.einsum('bqhd,bkhd->bhqk', qn, k_nope, preferred_element_type=jnp.float32)
             + jnp.einsum('bqhd,bkd->bhqk', qr, k_rope, preferred_element_type=jnp.float32))
        mask = k_chunk[None, :] <= (qp // CHUNK)[:, None]
        s = jnp.where(mask[None, None], s * scale, -jnp.inf)
        p = jax.nn.softmax(s, axis=-1).astype(v.dtype)
        return jnp.einsum('bhqk,bkhd->bqhd', p, v)

    B, T = q_nope.shape[0], q_nope.shape[1]
    if T > Q_BLOCK and T % Q_BLOCK == 0:
        n = T // Q_BLOCK

        def split(a):
            return jnp.moveaxis(a.reshape((B, n, Q_BLOCK) + a.shape[2:]), 1, 0)

        out = lax.map(block, (split(q_nope), split(q_rope), q_pos.reshape(n, Q_BLOCK)))
        return jnp.moveaxis(out, 0, 1).reshape((B, T) + out.shape[3:])
    return block((q_nope, q_rope, q_pos))


def multiscale_pool(u, u_past, pos, w_pool, pool_scale):
    B, T = u.shape[0], u.shape[1]
    u_all = jnp.concatenate([u_past, u], axis=1)
    cs = jnp.cumsum(u_all.astype(jnp.float32), axis=1)
    cs = jnp.concatenate([jnp.zeros((B, 1, D_POOL), jnp.float32), cs], axis=1)
    end = cs[:, POOL_BUF + 1:]
    u32 = u.astype(jnp.float32)
    outs = []
    for g, w in enumerate(POOL_WINDOWS):
        sl = slice(g * D_POOL_GROUP, (g + 1) * D_POOL_GROUP)
        start = cs[:, POOL_BUF + 1 - w: POOL_BUF + 1 - w + T, sl]
        cnt = jnp.minimum(pos + 1, w).astype(jnp.float32)[None, :, None]
        outs.append((end[..., sl] - start) / cnt - u32[..., sl])
    p = jnp.stack(outs, axis=2).astype(u.dtype)
    h = jnp.einsum('btgc,gce->btge', p, w_pool).reshape(B, T, D_POOL)
    return h * pool_scale, u_all[:, -POOL_BUF:]


def mixer_layer(x, pos, ckv_past, krope_past, pool_past,
                g_norm, w_in, g_q_lat, w_uq, g_qn, g_qr, g_kv_lat, g_kr, w_ukv, g_kn,
                w_pool, pool_scale, w_out):
    B, T = x.shape[0], x.shape[1]
    h = rmsnorm(x, g_norm)
    proj = jnp.einsum('btd,de->bte', h, w_in)
    offs = np.cumsum((0,) + SPLITS)
    c_q, c_kv, k_r, z_att, u, z_pool = [proj[..., int(offs[i]):int(offs[i + 1])] for i in range(len(SPLITS))]

    q = jnp.einsum('btr,rhe->bthe', rmsnorm(c_q, g_q_lat), w_uq)
    q_nope = rmsnorm(q[..., :D_NOPE], g_qn)
    q_rope = rope(rmsnorm(q[..., D_NOPE:], g_qr), pos)
    c_kv = rmsnorm(c_kv, g_kv_lat)
    k_r = rope(rmsnorm(k_r, g_kr), pos)
    ckv_all = jnp.concatenate([ckv_past, c_kv], axis=1)
    kr_all = jnp.concatenate([krope_past, k_r], axis=1)
    kv = jnp.einsum('bsr,rhe->bshe', ckv_all, w_ukv)
    k_nope = rmsnorm(kv[..., :D_NOPE], g_kn)
    v = kv[..., D_NOPE:]
    S = ckv_all.shape[1]
    k_pos = jnp.arange(S, dtype=jnp.int32)
    att = chunk_causal_attention(q_nope, q_rope, k_nope, kr_all, v, pos, k_pos).reshape(B, T, D_ATT)
    att = att * jax.nn.silu(z_att)

    pool, pool_state = multiscale_pool(u, pool_past, pos, w_pool, pool_scale)
    pool = pool * jax.nn.silu(z_pool)

    mix = jnp.concatenate([att, pool], axis=-1)
    y = x + jnp.einsum('btm,md->btd', mix, w_out)
    return y, c_kv, k_r, pool_state


def setup_inputs(seed: int = 0) -> dict:
    key = jax.random.key(seed)
    ks = jax.random.split(key, 24)
    f32 = jnp.float32

    def nrm(k, shape, scale):
        return jax.random.normal(k, shape, f32) * scale

    def gain(k, shape):
        return 1.0 + 0.05 * jax.random.normal(k, shape, f32)

    return {
        'x_prompt': nrm(ks[0], (BATCH, SEQ, D_MODEL), 1.0),
        'x_sample': nrm(ks[1], (DEC_BATCH, DEC_SEQ, D_MODEL), 1.0),
        'cache_ckv': nrm(ks[2], (DEPTH, DEC_BATCH, PAST_LEN, KV_LORA), 1.0),
        'cache_krope': nrm(ks[3], (DEPTH, DEC_BATCH, PAST_LEN, D_ROPE), 1.0),
        'state_pool': nrm(ks[4], (DEPTH, DEC_BATCH, POOL_BUF, D_POOL), 1.0),
        'g_norm': gain(ks[5], (DEPTH, D_MODEL)),
        'w_in': nrm(ks[6], (DEPTH, D_MODEL, D_IN), D_MODEL ** -0.5),
        'g_q_lat': gain(ks[7], (DEPTH, Q_LORA)),
        'w_uq': nrm(ks[8], (DEPTH, Q_LORA, N_HEADS, D_QK), Q_LORA ** -0.5),
        'g_qn': gain(ks[9], (DEPTH, D_NOPE)),
        'g_qr': gain(ks[10], (DEPTH, D_ROPE)),
        'g_kv_lat': gain(ks[11], (DEPTH, KV_LORA)),
        'g_kr': gain(ks[12], (DEPTH, D_ROPE)),
        'w_ukv': nrm(ks[13], (DEPTH, KV_LORA, N_HEADS, D_NOPE + D_V), KV_LORA ** -0.5),
        'g_kn': gain(ks[14], (DEPTH, D_NOPE)),
        'w_pool': nrm(ks[15], (DEPTH, N_POOL_GROUPS, D_POOL_GROUP, D_POOL_GROUP), D_POOL_GROUP ** -0.5),
        'pool_scale': gain(ks[16], (DEPTH, D_POOL)),
        'w_out': nrm(ks[17], (DEPTH, D_MIX, D_MODEL), D_MIX ** -0.5),
    }


def reference(x_prompt, x_sample, cache_ckv, cache_krope, state_pool,
              g_norm, w_in, g_q_lat, w_uq, g_qn, g_qr, g_kv_lat, g_kr, w_ukv, g_kn,
              w_pool, pool_scale, w_out):
    B, T = x_prompt.shape[0], x_prompt.shape[1]
    Bd, Td = x_sample.shape[0], x_sample.shape[1]
    P = cache_ckv.shape[2]
    pos_p = jnp.arange(T, dtype=jnp.int32)
    pos_s = P + jnp.arange(Td, dtype=jnp.int32)
    dt = x_prompt.dtype
    yp, ys = x_prompt, x_sample
    ckv_p, kr_p, pool_p, ckv_s, kr_s, pool_s = [], [], [], [], [], []
    for l in range(DEPTH):
        params = (g_norm[l], w_in[l], g_q_lat[l], w_uq[l], g_qn[l], g_qr[l], g_kv_lat[l], g_kr[l],
                  w_ukv[l], g_kn[l], w_pool[l], pool_scale[l], w_out[l])
        yp, a, b, c = mixer_layer(yp, pos_p,
                                  jnp.zeros((B, 0, KV_LORA), dt), jnp.zeros((B, 0, D_ROPE), dt),
                                  jnp.zeros((B, POOL_BUF, D_POOL), dt), *params)
        ckv_p.append(a); kr_p.append(b); pool_p.append(c)
        ys, a, b, c = mixer_layer(ys, pos_s, cache_ckv[l], cache_krope[l], state_pool[l], *params)
        ckv_s.append(a); kr_s.append(b); pool_s.append(c)
    return (yp, ys, jnp.stack(ckv_p), jnp.stack(kr_p), jnp.stack(pool_p),
            jnp.stack(ckv_s), jnp.stack(kr_s), jnp.stack(pool_s))
```

```python
import functools

import numpy as np
import jax
import jax.numpy as jnp
from jax import lax
from jax.experimental import pallas as pl
from jax.experimental.pallas import tpu as pltpu

D_MODEL = 4096
CHUNK = 64
N_HEADS = 16
D_NOPE = 128
D_ROPE = 64
D_QK = D_NOPE + D_ROPE
D_V = 128
D_ATT = N_HEADS * D_V
Q_LORA = 1024
KV_LORA = 512
D_POOL = 2048
POOL_WINDOWS = (2, 4, 8, 16)
N_POOL_GROUPS = len(POOL_WINDOWS)
D_POOL_GROUP = D_POOL // N_POOL_GROUPS
POOL_BUF = max(POOL_WINDOWS) - 1
ROPE_THETA = 10000.0
EPS = 1e-6

LANES = 128
ROPE_PAD = LANES
QK_PAD = D_NOPE + ROPE_PAD
LAT_PAD = Q_LORA + KV_LORA + ROPE_PAD
CARRY = POOL_BUF + 1
VMEM_LIMIT = 56 * 1024 * 1024

F32 = jnp.float32
BF16 = jnp.bfloat16


def _params(semantics):
    return pltpu.CompilerParams(dimension_semantics=semantics, vmem_limit_bytes=VMEM_LIMIT)


def _rms(x, g, n=None):
    n = x.shape[-1] if n is None else n
    ms = jnp.sum(x * x, axis=-1, keepdims=True) * (1.0 / n)
    return x * lax.rsqrt(ms + EPS) * g


def _rope(x, cos, sin_hi, sin_lo):
    return x * cos + pltpu.roll(x, D_ROPE // 2, 1) * sin_hi + pltpu.roll(x, ROPE_PAD - D_ROPE // 2, 1) * sin_lo


def _silu(z):
    return z * (1.0 / (1.0 + jnp.exp(-z)))


def _latent_kernel(x_ref, gn_ref, wlat_ref, gql_ref, wuq_ref, gqn_ref, gqr_ref, gkv_ref, gkr_ref,
                   cos_ref, shi_ref, slo_ref, h_ref, q_ref, ckv_ref, kr_ref, *, scale):
    h = _rms(x_ref[...], gn_ref[...]).astype(BF16)
    h_ref[...] = h
    c = jnp.dot(h, wlat_ref[...], preferred_element_type=F32)
    qlat = _rms(c[:, :Q_LORA], gql_ref[...]).astype(BF16)
    ckv_ref[...] = _rms(c[:, Q_LORA:Q_LORA + KV_LORA], gkv_ref[...])
    cos, shi, slo = cos_ref[...], shi_ref[...], slo_ref[...]
    kr = _rope(_rms(c[:, Q_LORA + KV_LORA:], gkr_ref[...], D_ROPE), cos, shi, slo)
    kr_ref[...] = kr[:, :D_ROPE]
    for hd in range(N_HEADS):
        q = jnp.dot(qlat, wuq_ref[:, hd * QK_PAD:(hd + 1) * QK_PAD], preferred_element_type=F32)
        qn = _rms(q[:, :D_NOPE], gqn_ref[...]) * scale
        qr = _rope(_rms(q[:, D_NOPE:], gqr_ref[...], D_ROPE), cos, shi, slo) * scale
        q_ref[hd, :, :D_NOPE] = qn.astype(BF16)
        q_ref[hd, :, D_NOPE:] = qr.astype(BF16)


def _latent_call(x2, tabs, w, t_len, tm):
    rows = x2.shape[0]
    if tm > t_len:
        tabs = [jnp.tile(t, (tm // t_len, 1)) for t in tabs]
        t_len = tm
    nt = t_len // tm
    const = lambda i: (0, 0)
    resident = lambda shape: pl.BlockSpec(shape, const, pipeline_mode=pl.Buffered(1))
    tab_spec = pl.BlockSpec((tm, ROPE_PAD), lambda i: (i % nt, 0))
    return pl.pallas_call(
        functools.partial(_latent_kernel, scale=D_QK ** -0.5),
        grid=(rows // tm,),
        in_specs=[
            pl.BlockSpec((tm, D_MODEL), lambda i: (i, 0)),
            resident((1, D_MODEL)),
            resident((D_MODEL, LAT_PAD)),
            resident((1, Q_LORA)),
            resident((Q_LORA, N_HEADS * QK_PAD)),
            resident((1, D_NOPE)),
            resident((1, ROPE_PAD)),
            resident((1, KV_LORA)),
            resident((1, ROPE_PAD)),
            tab_spec, tab_spec, tab_spec,
        ],
        out_specs=[
            pl.BlockSpec((tm, D_MODEL), lambda i: (i, 0)),
            pl.BlockSpec((N_HEADS, tm, QK_PAD), lambda i: (0, i, 0)),
            pl.BlockSpec((tm, KV_LORA), lambda i: (i, 0)),
            pl.BlockSpec((tm, D_ROPE), lambda i: (i, 0)),
        ],
        out_shape=[
            jax.ShapeDtypeStruct((rows, D_MODEL), BF16),
            jax.ShapeDtypeStruct((N_HEADS, rows, QK_PAD), BF16),
            jax.ShapeDtypeStruct((rows, KV_LORA), F32),
            jax.ShapeDtypeStruct((rows, D_ROPE), F32),
        ],
        compiler_params=_params(("arbitrary",)),
        name="latent",
    )(x2, w["g_norm"], w["w_lat"], w["g_q_lat"], w["w_uq"], w["g_qn"], w["g_qr"], w["g_kv_lat"], w["g_kr"],
      *tabs)


def _gate_kernel(h_ref, w_ref, o_ref):
    o_ref[...] = _silu(jnp.dot(h_ref[...], w_ref[...], preferred_element_type=F32))


def _gate_call(h, w_z, tm, tn):
    rows = h.shape[0]
    return pl.pallas_call(
        _gate_kernel,
        grid=(D_ATT // tn, rows // tm),
        in_specs=[pl.BlockSpec((tm, D_MODEL), lambda j, i: (i, 0)),
                  pl.BlockSpec((D_MODEL, tn), lambda j, i: (0, j))],
        out_specs=pl.BlockSpec((tm, tn), lambda j, i: (i, j)),
        out_shape=jax.ShapeDtypeStruct((rows, D_ATT), F32),
        compiler_params=_params(("arbitrary", "arbitrary")),
        name="gate",
    )(h, w_z)


def _pool_kernel(h_ref, w_ref, past_ref, wp_ref, ps_ref, o_ref, st_ref, ext_ref, p_ref, *, nb, tt, pos0):
    g = pl.program_id(0)
    i = pl.program_id(2)
    uz = jnp.dot(h_ref[...], w_ref[...], preferred_element_type=F32)
    u = uz[:, :D_POOL_GROUP]

    @pl.when(i == 0)
    def _():
        for b in range(nb):
            ext_ref[b, 0:1, :] = jnp.zeros((1, D_POOL_GROUP), F32)
            ext_ref[b, 1:CARRY, :] = past_ref[b]

    @pl.when(i > 0)
    def _():
        for b in range(nb):
            ext_ref[b, 0:CARRY, :] = ext_ref[b, tt:tt + CARRY, :]

    for b in range(nb):
        ext_ref[b, CARRY:CARRY + tt, :] = u[b * tt:(b + 1) * tt]
    pos = pos0 + i * tt + lax.broadcasted_iota(jnp.int32, (tt, 1), 0)

    for gi, win in enumerate(POOL_WINDOWS):
        @pl.when(g == gi)
        def _(win=win):
            cnt = jnp.minimum(pos + 1, win).astype(F32)
            for b in range(nb):
                ub = u[b * tt:(b + 1) * tt]
                acc = ub
                for k in range(1, win):
                    acc = acc + ext_ref[b, CARRY - k:CARRY - k + tt, :]
                p_ref[b * tt:(b + 1) * tt, :] = (acc / cnt - ub).astype(BF16)

    ph = jnp.dot(p_ref[...], wp_ref[0], preferred_element_type=F32)
    o_ref[...] = (ph * ps_ref[...] * _silu(uz[:, D_POOL_GROUP:])).astype(BF16)
    for b in range(nb):
        st_ref[b] = ext_ref[b, tt + 1:tt + CARRY, :]


def _pool_call(h, pool_past, w, batch, t_len, pos0, tm):
    rows = h.shape[0]
    tt = min(tm, t_len)
    nb = tm // tt
    nt = t_len // tt
    return pl.pallas_call(
        functools.partial(_pool_kernel, nb=nb, tt=tt, pos0=pos0),
        grid=(N_POOL_GROUPS, batch // nb, nt),
        in_specs=[
            pl.BlockSpec((tm, D_MODEL), lambda g, b, i: (b * nt + i, 0)),
            pl.BlockSpec((D_MODEL, 2 * D_POOL_GROUP), lambda g, b, i: (0, g)),
            pl.BlockSpec((nb, POOL_BUF, D_POOL_GROUP), lambda g, b, i: (b, 0, g)),
            pl.BlockSpec((1, D_POOL_GROUP, D_POOL_GROUP), lambda g, b, i: (g, 0, 0)),
            pl.BlockSpec((1, D_POOL_GROUP), lambda g, b, i: (0, g)),
        ],
        out_specs=[
            pl.BlockSpec((tm, D_POOL_GROUP), lambda g, b, i: (b * nt + i, g)),
            pl.BlockSpec((nb, POOL_BUF, D_POOL_GROUP), lambda g, b, i: (b, 0, g)),
        ],
        out_shape=[
            jax.ShapeDtypeStruct((rows, D_POOL), BF16),
            jax.ShapeDtypeStruct((batch, POOL_BUF, D_POOL), F32),
        ],
        scratch_shapes=[pltpu.VMEM((nb, tt + CARRY, D_POOL_GROUP), F32),
                        pltpu.VMEM((tm, D_POOL_GROUP), BF16)],
        compiler_params=_params(("arbitrary", "arbitrary", "arbitrary")),
        name="pool",
    )(h, w["w_pool_in"], pool_past, w["w_pool"], w["pool_scale"])


def _kv_kernel(ckv_ref, kr_ref, w_ref, gkn_ref, k_ref, v_ref):
    ckv = ckv_ref[...].astype(BF16)
    kr = kr_ref[...].astype(BF16)
    tm = ckv.shape[0]
    pair = 2 * D_NOPE
    for j in range(N_HEADS // 2):
        kk = jnp.dot(ckv, w_ref[:, j * pair:(j + 1) * pair], preferred_element_type=F32)
        for t in range(2):
            hd = 2 * j + t
            k_ref[hd, :, :D_NOPE] = _rms(kk[:, t * D_NOPE:(t + 1) * D_NOPE], gkn_ref[...]).astype(BF16)
            k_ref[hd, :, D_NOPE:D_QK] = kr
            k_ref[hd, :, D_QK:] = jnp.zeros((tm, QK_PAD - D_QK), BF16)
    off = N_HEADS * D_NOPE
    for j in range(N_HEADS // 2):
        vv = jnp.dot(ckv, w_ref[:, off + j * pair:off + (j + 1) * pair], preferred_element_type=F32)
        vt = vv.T
        v_ref[0, 2 * j] = vt[:D_V].astype(BF16)
        v_ref[0, 2 * j + 1] = vt[D_V:].astype(BF16)


def _kv_call(ckv, kr, w, tm):
    rows = ckv.shape[0]
    return pl.pallas_call(
        _kv_kernel,
        grid=(rows // tm,),
        in_specs=[
            pl.BlockSpec((tm, KV_LORA), lambda i: (i, 0)),
            pl.BlockSpec((tm, D_ROPE), lambda i: (i, 0)),
            pl.BlockSpec((KV_LORA, N_HEADS * (D_NOPE + D_V)), lambda i: (0, 0)),
            pl.BlockSpec((1, D_NOPE), lambda i: (0, 0)),
        ],
        out_specs=[
            pl.BlockSpec((N_HEADS, tm, QK_PAD), lambda i: (0, i, 0)),
            pl.BlockSpec((1, N_HEADS, D_V, tm), lambda i: (i, 0, 0, 0)),
        ],
        out_shape=[
            jax.ShapeDtypeStruct((N_HEADS, rows, QK_PAD), BF16),
            jax.ShapeDtypeStruct((rows // tm, N_HEADS, D_V, tm), BF16),
        ],
        compiler_params=_params(("arbitrary",)),
        name="kv",
    )(ckv, kr, w["w_ukv"], w["g_kn"])


_NT = (((1,), (1,)), ((), ()))


def _attn_kernel(qi_ref, kj_ref, flag_ref, q_ref, k_ref, v_ref, *rest, tq, tk, pos0, has_new):
    if has_new:
        kn_ref, vn_ref, g_ref, o_ref, m_ref, l_ref, acc_ref, bias_ref = rest
    else:
        g_ref, o_ref, m_ref, l_ref, acc_ref, bias_ref = rest
    n = pl.program_id(1)
    qi, kj, flag = qi_ref[n], kj_ref[n], flag_ref[n]
    masked = (flag & 1) == 1
    last = (flag & 2) == 2

    @pl.when(kj == 0)
    def _():
        m_ref[...] = jnp.full(m_ref.shape, -jnp.inf, F32)
        l_ref[...] = jnp.zeros(l_ref.shape, F32)
        acc_ref[...] = jnp.zeros(acc_ref.shape, F32)

    def block(kref, vref, use_bias):
        def head(hd, carry):
            st = lax.dot_general(kref[hd], q_ref[hd], _NT, preferred_element_type=F32)
            if use_bias:
                st = st + bias_ref[...]
            m_prev = m_ref[hd]
            m_new = jnp.maximum(m_prev, jnp.max(st, axis=0, keepdims=True))
            alpha = jnp.exp(m_prev - m_new)
            p = jnp.exp(st - m_new)
            l_ref[hd] = alpha * l_ref[hd] + jnp.sum(p, axis=0, keepdims=True)
            acc_ref[hd] = alpha * acc_ref[hd] + jnp.dot(vref[0, hd], p.astype(BF16),
                                                         preferred_element_type=F32)
            m_ref[hd] = m_new
            return carry
        lax.fori_loop(0, N_HEADS, head, 0)

    @pl.when(masked)
    def _():
        shift = CHUNK.bit_length() - 1
        kc = (kj * tk + lax.broadcasted_iota(jnp.int32, (tk, 1), 0)) >> shift
        qc = (pos0 + qi * tq + lax.broadcasted_iota(jnp.int32, (1, tq), 1)) >> shift
        bias_ref[...] = jnp.where(kc <= qc, 0.0, -jnp.inf).astype(F32)
        block(k_ref, v_ref, True)

    @pl.when(jnp.logical_not(masked))
    def _():
        block(k_ref, v_ref, False)

    if has_new:
        @pl.when(last)
        def _():
            block(kn_ref, vn_ref, False)

    @pl.when(last)
    def _():
        for hd in range(N_HEADS):
            att = (acc_ref[hd] * (1.0 / l_ref[hd])).T
            o_ref[:, hd * D_V:(hd + 1) * D_V] = (att * g_ref[:, hd * D_V:(hd + 1) * D_V]).astype(BF16)


def _attn_schedule(t_len, n_keys, pos0, tq, tk):
    qi, kj, flag = [], [], []
    nk = n_keys // tk
    for i in range(t_len // tq):
        first_chunk = (pos0 + i * tq) // CHUNK
        last_visible = ((pos0 + (i + 1) * tq - 1) // CHUNK) * CHUNK + CHUNK - 1
        last_blk = min(nk - 1, last_visible // tk)
        for j in range(last_blk + 1):
            needs_mask = ((j + 1) * tk - 1) // CHUNK > first_chunk
            qi.append(i)
            kj.append(j)
            flag.append(int(needs_mask) + 2 * int(j == last_blk))
    return tuple(np.asarray(a, np.int32) for a in (qi, kj, flag))


def _attn_call(q, k, vt, k_new, vt_new, gate, batch, t_len, n_keys, pos0, tq, tk):
    rows = batch * t_len
    nq, nk = t_len // tq, n_keys // tk
    has_new = k_new is not None
    qi, kj, flag = _attn_schedule(t_len, n_keys, pos0, tq, tk)
    q_map = lambda b, n, qi, kj, fl: (0, b * nq + qi[n], 0)
    row_map = lambda b, n, qi, kj, fl: (b * nq + qi[n], 0)
    in_specs = [
        pl.BlockSpec((N_HEADS, tq, QK_PAD), q_map),
        pl.BlockSpec((N_HEADS, tk, QK_PAD), lambda b, n, qi, kj, fl: (0, b * nk + kj[n], 0)),
        pl.BlockSpec((1, N_HEADS, D_V, tk), lambda b, n, qi, kj, fl: (b * nk + kj[n], 0, 0, 0)),
    ]
    args = [q, k, vt]
    if has_new:
        assert pos0 % CHUNK == 0 and t_len <= CHUNK and tq == t_len
        in_specs += [
            pl.BlockSpec((N_HEADS, t_len, QK_PAD), lambda b, n, qi, kj, fl: (0, b, 0)),
            pl.BlockSpec((1, N_HEADS, D_V, t_len), lambda b, n, qi, kj, fl: (b, 0, 0, 0)),
        ]
        args += [k_new, vt_new]
    in_specs.append(pl.BlockSpec((tq, D_ATT), row_map))
    args.append(gate)
    grid_spec = pltpu.PrefetchScalarGridSpec(
        num_scalar_prefetch=3,
        grid=(batch, len(qi)),
        in_specs=in_specs,
        out_specs=pl.BlockSpec((tq, D_ATT), row_map),
        scratch_shapes=[
            pltpu.VMEM((N_HEADS, 1, tq), F32),
            pltpu.VMEM((N_HEADS, 1, tq), F32),
            pltpu.VMEM((N_HEADS, D_V, tq), F32),
            pltpu.VMEM((tk, tq), F32),
        ],
    )
    return pl.pallas_call(
        functools.partial(_attn_kernel, tq=tq, tk=tk, pos0=pos0, has_new=has_new),
        grid_spec=grid_spec,
        out_shape=jax.ShapeDtypeStruct((rows, D_ATT), BF16),
        compiler_params=_params(("arbitrary", "arbitrary")),
        name="attn",
    )(jnp.asarray(qi), jnp.asarray(kj), jnp.asarray(flag), *args)


def _out_kernel(a_ref, p_ref, w_ref, x_ref, y_ref):
    y_ref[...] = x_ref[...] + (jnp.dot(a_ref[...], w_ref[0], preferred_element_type=F32)
                               + jnp.dot(p_ref[...], w_ref[1], preferred_element_type=F32))


def _out_call(att, pool, w_out, x2, tm, tn):
    rows = x2.shape[0]
    return pl.pallas_call(
        _out_kernel,
        grid=(D_MODEL // tn, rows // tm),
        in_specs=[
            pl.BlockSpec((tm, D_ATT), lambda j, i: (i, 0)),
            pl.BlockSpec((tm, D_POOL), lambda j, i: (i, 0)),
            pl.BlockSpec((2, D_ATT, tn), lambda j, i: (0, 0, j)),
            pl.BlockSpec((tm, tn), lambda j, i: (i, j)),
        ],
        out_specs=pl.BlockSpec((tm, tn), lambda j, i: (i, j)),
        out_shape=jax.ShapeDtypeStruct((rows, D_MODEL), F32),
        compiler_params=_params(("arbitrary", "arbitrary")),
        name="out_proj",
    )(att, pool, w_out, x2)


def _rope_tables(pos0, t_len):
    half = D_ROPE // 2
    freqs = ROPE_THETA ** (-jnp.arange(half, dtype=F32) / half)
    ang = (pos0 + jnp.arange(t_len, dtype=jnp.int32)).astype(F32)[:, None] * freqs[None, :]
    cos, sin = jnp.cos(ang), jnp.sin(ang)
    zero = jnp.zeros_like(cos)
    pad = jnp.zeros((t_len, ROPE_PAD - D_ROPE), F32)
    return (jnp.concatenate([cos, cos, pad], axis=1),
            jnp.concatenate([zero, sin, pad], axis=1),
            jnp.concatenate([-sin, zero, pad], axis=1))


def _prep_weights(g_norm, w_in, g_q_lat, w_uq, g_qn, g_qr, g_kv_lat, g_kr, w_ukv, g_kn,
                  w_pool, pool_scale, w_out):
    o_z = Q_LORA + KV_LORA + D_ROPE
    o_u = o_z + D_ATT
    o_zp = o_u + D_POOL
    row = lambda v: v.reshape(1, -1).astype(F32)
    pad_row = lambda v: jnp.pad(v, (0, ROPE_PAD - D_ROPE)).reshape(1, -1).astype(F32)
    u_w = w_in[:, o_u:o_zp].reshape(D_MODEL, N_POOL_GROUPS, 1, D_POOL_GROUP)
    zp_w = w_in[:, o_zp:].reshape(D_MODEL, N_POOL_GROUPS, 1, D_POOL_GROUP)
    return {
        "g_norm": row(g_norm), "g_q_lat": row(g_q_lat), "g_qn": row(g_qn), "g_qr": pad_row(g_qr),
        "g_kv_lat": row(g_kv_lat), "g_kr": pad_row(g_kr), "g_kn": row(g_kn), "pool_scale": row(pool_scale),
        "w_lat": jnp.pad(w_in[:, :o_z], ((0, 0), (0, ROPE_PAD - D_ROPE))).astype(BF16),
        "w_z": w_in[:, o_z:o_u].astype(BF16),
        "w_pool_in": jnp.concatenate([u_w, zp_w], axis=2).reshape(D_MODEL, 2 * D_POOL).astype(BF16),
        "w_uq": jnp.pad(w_uq, ((0, 0), (0, 0), (0, QK_PAD - D_QK))).reshape(Q_LORA, N_HEADS * QK_PAD).astype(BF16),
        "w_ukv": jnp.concatenate([w_ukv[:, :, :D_NOPE].reshape(KV_LORA, -1),
                                  w_ukv[:, :, D_NOPE:].reshape(KV_LORA, -1)], axis=1).astype(BF16),
        "w_pool": w_pool.astype(BF16),
        "w_out": w_out.reshape(2, D_ATT, D_MODEL).astype(BF16),
    }


def _tile(n, pref):
    return pref if n % pref == 0 else n


def _mixer_layer(x, ckv_past, kr_past, pool_past, w):
    batch, t_len, _ = x.shape
    rows = batch * t_len
    pos0 = 0 if ckv_past is None else ckv_past.shape[1]
    x2 = x.reshape(rows, D_MODEL)

    h, q, ckv, kr = _latent_call(x2, _rope_tables(pos0, t_len), w, t_len, 256)
    gate = _gate_call(h, w["w_z"], 512, 1024)
    pool, pool_state = _pool_call(h, pool_past, w, batch, t_len, pos0, 512)
    tk = 512
    if ckv_past is None:
        k, vt = _kv_call(ckv, kr, w, tk)
        att = _attn_call(q, k, vt, None, None, gate, batch, t_len, t_len, pos0, _tile(t_len, 512), tk)
    else:
        k, vt = _kv_call(ckv_past.reshape(batch * pos0, KV_LORA), kr_past.reshape(batch * pos0, D_ROPE), w, tk)
        k_new, vt_new = _kv_call(ckv, kr, w, t_len)
        att = _attn_call(q, k, vt, k_new, vt_new, gate, batch, t_len, pos0, pos0, t_len, tk)
    y = _out_call(att, pool, w["w_out"], x2, 512, 1024)
    return (y.reshape(batch, t_len, D_MODEL), ckv.reshape(batch, t_len, KV_LORA),
            kr.reshape(batch, t_len, D_ROPE), pool_state)


def kernel(x_prompt, x_sample, cache_ckv, cache_krope, state_pool, g_norm, w_in, g_q_lat, w_uq, g_qn, g_qr,
           g_kv_lat, g_kr, w_ukv, g_kn, w_pool, pool_scale, w_out):
    depth = w_in.shape[0]
    batch = x_prompt.shape[0]
    yp, ys = x_prompt, x_sample
    outs = [[] for _ in range(6)]
    for l in range(depth):
        w = _prep_weights(g_norm[l], w_in[l], g_q_lat[l], w_uq[l], g_qn[l], g_qr[l], g_kv_lat[l], g_kr[l],
                          w_ukv[l], g_kn[l], w_pool[l], pool_scale[l], w_out[l])
        yp, a, b, c = _mixer_layer(yp, None, None, jnp.zeros((batch, POOL_BUF, D_POOL), F32), w)
        outs[0].append(a); outs[1].append(b); outs[2].append(c)
        ys, a, b, c = _mixer_layer(ys, cache_ckv[l], cache_krope[l], state_pool[l], w)
        outs[3].append(a); outs[4].append(b); outs[5].append(c)
    return (yp, ys) + tuple(jnp.stack(o) for o in outs)
```

```python
import functools

import numpy as np
import jax
import jax.numpy as jnp
from jax import lax
from jax.experimental import pallas as pl
from jax.experimental.pallas import tpu as pltpu

D_MODEL = 4096
CHUNK = 64
N_HEADS = 16
D_NOPE = 128
D_ROPE = 64
D_QK = D_NOPE + D_ROPE
D_V = 128
D_ATT = N_HEADS * D_V
Q_LORA = 1024
KV_LORA = 512
D_POOL = 2048
POOL_WINDOWS = (2, 4, 8, 16)
N_POOL_GROUPS = len(POOL_WINDOWS)
D_POOL_GROUP = D_POOL // N_POOL_GROUPS
POOL_BUF = max(POOL_WINDOWS) - 1
ROPE_THETA = 10000.0
EPS = 1e-6

LANES = 128
ROPE_PAD = LANES
QK_PAD = D_NOPE + ROPE_PAD
LAT_PAD = Q_LORA + KV_LORA + ROPE_PAD
CARRY = POOL_BUF + 1
VMEM_LIMIT = 56 * 1024 * 1024

F32 = jnp.float32
BF16 = jnp.bfloat16


def _params(semantics):
    return pltpu.CompilerParams(dimension_semantics=semantics, vmem_limit_bytes=VMEM_LIMIT)


def _rms(x, g, n=None):
    n = x.shape[-1] if n is None else n
    ms = jnp.sum(x * x, axis=-1, keepdims=True) * (1.0 / n)
    return x * lax.rsqrt(ms + EPS) * g


def _rope(x, cos, sin_hi, sin_lo):
    return x * cos + pltpu.roll(x, D_ROPE // 2, 1) * sin_hi + pltpu.roll(x, ROPE_PAD - D_ROPE // 2, 1) * sin_lo


def _silu(z):
    return z * (1.0 / (1.0 + jnp.exp(-z)))


def _latent_kernel(x_ref, gn_ref, wlat_ref, gql_ref, wuq_ref, gqn_ref, gqr_ref, gkv_ref, gkr_ref,
                   cos_ref, shi_ref, slo_ref, h_ref, q_ref, ckv_ref, kr_ref, *, scale):
    h = _rms(x_ref[...], gn_ref[...]).astype(BF16)
    h_ref[...] = h
    c = jnp.dot(h, wlat_ref[...], preferred_element_type=F32)
    qlat = _rms(c[:, :Q_LORA], gql_ref[...]).astype(BF16)
    ckv_ref[...] = _rms(c[:, Q_LORA:Q_LORA + KV_LORA], gkv_ref[...])
    cos, shi, slo = cos_ref[...], shi_ref[...], slo_ref[...]
    kr = _rope(_rms(c[:, Q_LORA + KV_LORA:], gkr_ref[...], D_ROPE), cos, shi, slo)
    kr_ref[...] = kr[:, :D_ROPE]
    for hd in range(N_HEADS):
        q = jnp.dot(qlat, wuq_ref[:, hd * QK_PAD:(hd + 1) * QK_PAD], preferred_element_type=F32)
        qn = _rms(q[:, :D_NOPE], gqn_ref[...]) * scale
        qr = _rope(_rms(q[:, D_NOPE:], gqr_ref[...], D_ROPE), cos, shi, slo) * scale
        q_ref[hd, :, :D_NOPE] = qn.astype(BF16)
        q_ref[hd, :, D_NOPE:] = qr.astype(BF16)


def _latent_call(x2, tabs, w, t_len, tm):
    rows = x2.shape[0]
    if tm > t_len:
        tabs = [jnp.tile(t, (tm // t_len, 1)) for t in tabs]
        t_len = tm
    nt = t_len // tm
    const = lambda i: (0, 0)
    resident = lambda shape: pl.BlockSpec(shape, const, pipeline_mode=pl.Buffered(1))
    tab_spec = pl.BlockSpec((tm, ROPE_PAD), lambda i: (i % nt, 0))
    return pl.pallas_call(
        functools.partial(_latent_kernel, scale=float(D_QK ** -0.5 * np.log2(np.e))),
        grid=(rows // tm,),
        in_specs=[
            pl.BlockSpec((tm, D_MODEL), lambda i: (i, 0)),
            resident((1, D_MODEL)),
            resident((D_MODEL, LAT_PAD)),
            resident((1, Q_LORA)),
            resident((Q_LORA, N_HEADS * QK_PAD)),
            resident((1, D_NOPE)),
            resident((1, ROPE_PAD)),
            resident((1, KV_LORA)),
            resident((1, ROPE_PAD)),
            tab_spec, tab_spec, tab_spec,
        ],
        out_specs=[
            pl.BlockSpec((tm, D_MODEL), lambda i: (i, 0)),
            pl.BlockSpec((N_HEADS, tm, QK_PAD), lambda i: (0, i, 0)),
            pl.BlockSpec((tm, KV_LORA), lambda i: (i, 0)),
            pl.BlockSpec((tm, D_ROPE), lambda i: (i, 0)),
        ],
        out_shape=[
            jax.ShapeDtypeStruct((rows, D_MODEL), BF16),
            jax.ShapeDtypeStruct((N_HEADS, rows, QK_PAD), BF16),
            jax.ShapeDtypeStruct((rows, KV_LORA), F32),
            jax.ShapeDtypeStruct((rows, D_ROPE), F32),
        ],
        compiler_params=_params(("arbitrary",)),
        name="latent",
    )(x2, w["g_norm"], w["w_lat"], w["g_q_lat"], w["w_uq"], w["g_qn"], w["g_qr"], w["g_kv_lat"], w["g_kr"],
      *tabs)


def _gate_kernel(h_ref, w_ref, o_ref):
    o_ref[...] = _silu(jnp.dot(h_ref[...], w_ref[...], preferred_element_type=F32))


def _gate_call(h, w_z, tm, tn):
    rows = h.shape[0]
    return pl.pallas_call(
        _gate_kernel,
        grid=(D_ATT // tn, rows // tm),
        in_specs=[pl.BlockSpec((tm, D_MODEL), lambda j, i: (i, 0)),
                  pl.BlockSpec((D_MODEL, tn), lambda j, i: (0, j))],
        out_specs=pl.BlockSpec((tm, tn), lambda j, i: (i, j)),
        out_shape=jax.ShapeDtypeStruct((rows, D_ATT), F32),
        compiler_params=_params(("arbitrary", "arbitrary")),
        name="gate",
    )(h, w_z)


def _pool_kernel(h_ref, w_ref, past_ref, wp_ref, ps_ref, o_ref, st_ref, ext_ref, p_ref, *, nb, tt, pos0):
    g = pl.program_id(0)
    i = pl.program_id(2)
    uz = jnp.dot(h_ref[...], w_ref[...], preferred_element_type=F32)
    u = uz[:, :D_POOL_GROUP]

    @pl.when(i == 0)
    def _():
        for b in range(nb):
            ext_ref[b, 0:1, :] = jnp.zeros((1, D_POOL_GROUP), F32)
            ext_ref[b, 1:CARRY, :] = past_ref[b]

    @pl.when(i > 0)
    def _():
        for b in range(nb):
            ext_ref[b, 0:CARRY, :] = ext_ref[b, tt:tt + CARRY, :]

    for b in range(nb):
        ext_ref[b, CARRY:CARRY + tt, :] = u[b * tt:(b + 1) * tt]
    pos = pos0 + i * tt + lax.broadcasted_iota(jnp.int32, (tt, 1), 0)

    for gi, win in enumerate(POOL_WINDOWS):
        @pl.when(g == gi)
        def _(win=win):
            cnt = jnp.minimum(pos + 1, win).astype(F32)
            for b in range(nb):
                ub = u[b * tt:(b + 1) * tt]
                acc = ub
                for k in range(1, win):
                    acc = acc + ext_ref[b, CARRY - k:CARRY - k + tt, :]
                p_ref[b * tt:(b + 1) * tt, :] = (acc / cnt - ub).astype(BF16)

    ph = jnp.dot(p_ref[...], wp_ref[0], preferred_element_type=F32)
    o_ref[...] = (ph * ps_ref[...] * _silu(uz[:, D_POOL_GROUP:])).astype(BF16)
    for b in range(nb):
        st_ref[b] = ext_ref[b, tt + 1:tt + CARRY, :]


def _pool_call(h, pool_past, w, batch, t_len, pos0, tm):
    rows = h.shape[0]
    tt = min(tm, t_len)
    nb = tm // tt
    nt = t_len // tt
    return pl.pallas_call(
        functools.partial(_pool_kernel, nb=nb, tt=tt, pos0=pos0),
        grid=(N_POOL_GROUPS, batch // nb, nt),
        in_specs=[
            pl.BlockSpec((tm, D_MODEL), lambda g, b, i: (b * nt + i, 0)),
            pl.BlockSpec((D_MODEL, 2 * D_POOL_GROUP), lambda g, b, i: (0, g)),
            pl.BlockSpec((nb, POOL_BUF, D_POOL_GROUP), lambda g, b, i: (b, 0, g)),
            pl.BlockSpec((1, D_POOL_GROUP, D_POOL_GROUP), lambda g, b, i: (g, 0, 0)),
            pl.BlockSpec((1, D_POOL_GROUP), lambda g, b, i: (0, g)),
        ],
        out_specs=[
            pl.BlockSpec((tm, D_POOL_GROUP), lambda g, b, i: (b * nt + i, g)),
            pl.BlockSpec((nb, POOL_BUF, D_POOL_GROUP), lambda g, b, i: (b, 0, g)),
        ],
        out_shape=[
            jax.ShapeDtypeStruct((rows, D_POOL), BF16),
            jax.ShapeDtypeStruct((batch, POOL_BUF, D_POOL), F32),
        ],
        scratch_shapes=[pltpu.VMEM((nb, tt + CARRY, D_POOL_GROUP), F32),
                        pltpu.VMEM((tm, D_POOL_GROUP), BF16)],
        compiler_params=_params(("arbitrary", "arbitrary", "arbitrary")),
        name="pool",
    )(h, w["w_pool_in"], pool_past, w["w_pool"], w["pool_scale"])


def _kv_kernel(ckv_ref, kr_ref, w_ref, gkn_ref, k_ref, v_ref):
    ckv = ckv_ref[...].astype(BF16)
    kr = kr_ref[...].astype(BF16)
    tm = ckv.shape[0]
    width = D_NOPE + D_V
    for hd in range(N_HEADS):
        kv = jnp.dot(ckv, w_ref[:, hd * width:(hd + 1) * width], preferred_element_type=F32)
        k_ref[hd, :, :D_NOPE] = _rms(kv[:, :D_NOPE], gkn_ref[...]).astype(BF16)
        k_ref[hd, :, D_NOPE:D_QK] = kr
        k_ref[hd, :, D_QK:] = jnp.zeros((tm, QK_PAD - D_QK), BF16)
        v_ref[0, hd] = kv[:, D_NOPE:].T.astype(BF16)


def _kv_call(ckv, kr, w, tm):
    rows = ckv.shape[0]
    return pl.pallas_call(
        _kv_kernel,
        grid=(rows // tm,),
        in_specs=[
            pl.BlockSpec((tm, KV_LORA), lambda i: (i, 0)),
            pl.BlockSpec((tm, D_ROPE), lambda i: (i, 0)),
            pl.BlockSpec((KV_LORA, N_HEADS * (D_NOPE + D_V)), lambda i: (0, 0)),
            pl.BlockSpec((1, D_NOPE), lambda i: (0, 0)),
        ],
        out_specs=[
            pl.BlockSpec((N_HEADS, tm, QK_PAD), lambda i: (0, i, 0)),
            pl.BlockSpec((1, N_HEADS, D_V, tm), lambda i: (i, 0, 0, 0)),
        ],
        out_shape=[
            jax.ShapeDtypeStruct((N_HEADS, rows, QK_PAD), BF16),
            jax.ShapeDtypeStruct((rows // tm, N_HEADS, D_V, tm), BF16),
        ],
        compiler_params=_params(("arbitrary",)),
        name="kv",
    )(ckv, kr, w["w_ukv"], w["g_kn"])


_NT = (((1,), (1,)), ((), ()))


def _attn_kernel(qi_ref, kj_ref, flag_ref, q_ref, k_ref, v_ref, *rest, tq, tk, pos0, has_new):
    if has_new:
        kn_ref, vn_ref, g_ref, o_ref, m_ref, l_ref, acc_ref, bias_ref, sa_ref, sb_ref = rest
    else:
        g_ref, o_ref, m_ref, l_ref, acc_ref, bias_ref, sa_ref, sb_ref = rest
    n = pl.program_id(1)
    qi, kj, flag = qi_ref[n], kj_ref[n], flag_ref[n]
    masked = (flag & 1) == 1
    last = (flag & 2) == 2

    @pl.when(kj == 0)
    def _():
        m_ref[...] = jnp.full(m_ref.shape, -jnp.inf, F32)
        l_ref[...] = jnp.zeros(l_ref.shape, F32)
        acc_ref[...] = jnp.zeros(acc_ref.shape, F32)

    def block(kref, vref, use_bias):
        nkeys = kref.shape[1]
        sa = sa_ref.at[0:nkeys]
        sb = sb_ref.at[0:nkeys]

        def scores(hd, s_ref):
            st = lax.dot_general(kref[hd], q_ref[hd], _NT, preferred_element_type=F32)
            s_ref[...] = st + bias_ref[...] if use_bias else st

        def softmax_pv(hd, s_ref):
            m_prev = m_ref[hd]
            m_new = jnp.maximum(m_prev, jnp.max(s_ref[...], axis=0, keepdims=True))
            alpha = jnp.exp2(m_prev - m_new)
            p = jnp.exp2(s_ref[...] - m_new)
            l_ref[hd] = alpha * l_ref[hd] + jnp.sum(p, axis=0, keepdims=True)
            acc_ref[hd] = alpha * acc_ref[hd] + jnp.dot(vref[0, hd], p.astype(BF16),
                                                         preferred_element_type=F32)
            m_ref[hd] = m_new

        bufs = (sa, sb)
        scores(0, sa)
        for hd in range(N_HEADS):
            if hd + 1 < N_HEADS:
                scores(hd + 1, bufs[(hd + 1) % 2])
            softmax_pv(hd, bufs[hd % 2])

    @pl.when(masked)
    def _():
        shift = CHUNK.bit_length() - 1
        kc = (kj * tk + lax.broadcasted_iota(jnp.int32, (tk, 1), 0)) >> shift
        qc = (pos0 + qi * tq + lax.broadcasted_iota(jnp.int32, (1, tq), 1)) >> shift
        bias_ref[...] = jnp.where(kc <= qc, 0.0, -jnp.inf).astype(F32)
        block(k_ref, v_ref, True)

    @pl.when(jnp.logical_not(masked))
    def _():
        block(k_ref, v_ref, False)

    if has_new:
        @pl.when(last)
        def _():
            block(kn_ref, vn_ref, False)

    @pl.when(last)
    def _():
        for hd in range(N_HEADS):
            att = (acc_ref[hd] * (1.0 / l_ref[hd])).T
            o_ref[:, hd * D_V:(hd + 1) * D_V] = (att * g_ref[:, hd * D_V:(hd + 1) * D_V]).astype(BF16)


def _attn_schedule(t_len, n_keys, pos0, tq, tk):
    qi, kj, flag = [], [], []
    nk = n_keys // tk
    for i in range(t_len // tq):
        first_chunk = (pos0 + i * tq) // CHUNK
        last_visible = ((pos0 + (i + 1) * tq - 1) // CHUNK) * CHUNK + CHUNK - 1
        last_blk = min(nk - 1, last_visible // tk)
        for j in range(last_blk + 1):
            needs_mask = ((j + 1) * tk - 1) // CHUNK > first_chunk
            qi.append(i)
            kj.append(j)
            flag.append(int(needs_mask) + 2 * int(j == last_blk))
    return tuple(np.asarray(a, np.int32) for a in (qi, kj, flag))


def _attn_call(q, k, vt, k_new, vt_new, gate, batch, t_len, n_keys, pos0, tq, tk):
    rows = batch * t_len
    nq, nk = t_len // tq, n_keys // tk
    has_new = k_new is not None
    qi, kj, flag = _attn_schedule(t_len, n_keys, pos0, tq, tk)
    q_map = lambda b, n, qi, kj, fl: (0, b * nq + qi[n], 0)
    row_map = lambda b, n, qi, kj, fl: (b * nq + qi[n], 0)
    in_specs = [
        pl.BlockSpec((N_HEADS, tq, QK_PAD), q_map),
        pl.BlockSpec((N_HEADS, tk, QK_PAD), lambda b, n, qi, kj, fl: (0, b * nk + kj[n], 0)),
        pl.BlockSpec((1, N_HEADS, D_V, tk), lambda b, n, qi, kj, fl: (b * nk + kj[n], 0, 0, 0)),
    ]
    args = [q, k, vt]
    if has_new:
        assert pos0 % CHUNK == 0 and t_len <= CHUNK and tq == t_len
        in_specs += [
            pl.BlockSpec((N_HEADS, t_len, QK_PAD), lambda b, n, qi, kj, fl: (0, b, 0)),
            pl.BlockSpec((1, N_HEADS, D_V, t_len), lambda b, n, qi, kj, fl: (b, 0, 0, 0)),
        ]
        args += [k_new, vt_new]
    in_specs.append(pl.BlockSpec((tq, D_ATT), row_map))
    args.append(gate)
    grid_spec = pltpu.PrefetchScalarGridSpec(
        num_scalar_prefetch=3,
        grid=(batch, len(qi)),
        in_specs=in_specs,
        out_specs=pl.BlockSpec((tq, D_ATT), row_map),
        scratch_shapes=[
            pltpu.VMEM((N_HEADS, 1, tq), F32),
            pltpu.VMEM((N_HEADS, 1, tq), F32),
            pltpu.VMEM((N_HEADS, D_V, tq), F32),
            pltpu.VMEM((tk, tq), F32),
            pltpu.VMEM((tk, tq), F32),
            pltpu.VMEM((tk, tq), F32),
        ],
    )
    return pl.pallas_call(
        functools.partial(_attn_kernel, tq=tq, tk=tk, pos0=pos0, has_new=has_new),
        grid_spec=grid_spec,
        out_shape=jax.ShapeDtypeStruct((rows, D_ATT), BF16),
        compiler_params=_params(("arbitrary", "arbitrary")),
        name="attn",
    )(jnp.asarray(qi), jnp.asarray(kj), jnp.asarray(flag), *args)


def _out_kernel(a_ref, p_ref, w_ref, x_ref, y_ref):
    y_ref[...] = x_ref[...] + (jnp.dot(a_ref[...], w_ref[0], preferred_element_type=F32)
                               + jnp.dot(p_ref[...], w_ref[1], preferred_element_type=F32))


def _out_call(att, pool, w_out, x2, tm, tn):
    rows = x2.shape[0]
    return pl.pallas_call(
        _out_kernel,
        grid=(D_MODEL // tn, rows // tm),
        in_specs=[
            pl.BlockSpec((tm, D_ATT), lambda j, i: (i, 0)),
            pl.BlockSpec((tm, D_POOL), lambda j, i: (i, 0)),
            pl.BlockSpec((2, D_ATT, tn), lambda j, i: (0, 0, j)),
            pl.BlockSpec((tm, tn), lambda j, i: (i, j)),
        ],
        out_specs=pl.BlockSpec((tm, tn), lambda j, i: (i, j)),
        out_shape=jax.ShapeDtypeStruct((rows, D_MODEL), F32),
        compiler_params=_params(("arbitrary", "arbitrary")),
        name="out_proj",
    )(att, pool, w_out, x2)


def _rope_tables(pos0, t_len):
    half = D_ROPE // 2
    freqs = ROPE_THETA ** (-jnp.arange(half, dtype=F32) / half)
    ang = (pos0 + jnp.arange(t_len, dtype=jnp.int32)).astype(F32)[:, None] * freqs[None, :]
    cos, sin = jnp.cos(ang), jnp.sin(ang)
    zero = jnp.zeros_like(cos)
    pad = jnp.zeros((t_len, ROPE_PAD - D_ROPE), F32)
    return (jnp.concatenate([cos, cos, pad], axis=1),
            jnp.concatenate([zero, sin, pad], axis=1),
            jnp.concatenate([-sin, zero, pad], axis=1))


O_Z = Q_LORA + KV_LORA + D_ROPE
O_U = O_Z + D_ATT
O_ZP = O_U + D_POOL
D_IN = O_ZP + D_POOL


def _low_lanes(x, n):
    lane = lax.broadcasted_iota(jnp.int32, x.shape, 1)
    return jnp.where(lane < n, x, 0.0)


def _prep_in_kernel(w_ref, lat_ref, z_ref, pool_ref):
    aligned = Q_LORA + KV_LORA
    lat_ref[:, :aligned] = w_ref[:, :aligned].astype(BF16)
    lat_ref[:, aligned:] = _low_lanes(w_ref[:, aligned:LAT_PAD], D_ROPE).astype(BF16)
    z_ref[...] = w_ref[:, O_Z:O_U].astype(BF16)
    for g in range(N_POOL_GROUPS):
        lo = 2 * g * D_POOL_GROUP
        pool_ref[:, lo:lo + D_POOL_GROUP] = w_ref[:, O_U + g * D_POOL_GROUP:O_U + (g + 1) * D_POOL_GROUP].astype(BF16)
        pool_ref[:, lo + D_POOL_GROUP:lo + 2 * D_POOL_GROUP] = (
            w_ref[:, O_ZP + g * D_POOL_GROUP:O_ZP + (g + 1) * D_POOL_GROUP].astype(BF16))


def _prep_in_call(w_in, tr):
    return pl.pallas_call(
        _prep_in_kernel,
        grid=(D_MODEL // tr,),
        in_specs=[pl.BlockSpec((tr, D_IN), lambda i: (i, 0))],
        out_specs=[pl.BlockSpec((tr, LAT_PAD), lambda i: (i, 0)),
                   pl.BlockSpec((tr, D_ATT), lambda i: (i, 0)),
                   pl.BlockSpec((tr, 2 * D_POOL), lambda i: (i, 0))],
        out_shape=[jax.ShapeDtypeStruct((D_MODEL, LAT_PAD), BF16),
                   jax.ShapeDtypeStruct((D_MODEL, D_ATT), BF16),
                   jax.ShapeDtypeStruct((D_MODEL, 2 * D_POOL), BF16)],
        compiler_params=_params(("arbitrary",)),
        name="prep_w_in",
    )(w_in)


def _prep_uq_kernel(w_ref, o_ref):
    for hd in range(N_HEADS):
        o_ref[:, hd * QK_PAD:hd * QK_PAD + D_NOPE] = w_ref[:, hd * D_QK:hd * D_QK + D_NOPE].astype(BF16)
        rope = jnp.concatenate([w_ref[:, hd * D_QK + D_NOPE:(hd + 1) * D_QK],
                                jnp.zeros((w_ref.shape[0], ROPE_PAD - D_ROPE), F32)], axis=1)
        o_ref[:, hd * QK_PAD + D_NOPE:(hd + 1) * QK_PAD] = rope.astype(BF16)


def _prep_uq_call(w_uq, tr):
    return pl.pallas_call(
        _prep_uq_kernel,
        grid=(Q_LORA // tr,),
        in_specs=[pl.BlockSpec((tr, N_HEADS * D_QK), lambda i: (i, 0))],
        out_specs=pl.BlockSpec((tr, N_HEADS * QK_PAD), lambda i: (i, 0)),
        out_shape=jax.ShapeDtypeStruct((Q_LORA, N_HEADS * QK_PAD), BF16),
        compiler_params=_params(("arbitrary",)),
        name="prep_w_uq",
    )(w_uq.reshape(Q_LORA, N_HEADS * D_QK))


def _prep_weights(g_norm, w_in, g_q_lat, w_uq, g_qn, g_qr, g_kv_lat, g_kr, w_ukv, g_kn,
                  w_pool, pool_scale, w_out):
    row = lambda v: v.reshape(1, -1).astype(F32)
    pad_row = lambda v: jnp.pad(v, (0, ROPE_PAD - D_ROPE)).reshape(1, -1).astype(F32)
    w_lat, w_z, w_pool_in = _prep_in_call(w_in, 256)
    return {
        "g_norm": row(g_norm), "g_q_lat": row(g_q_lat), "g_qn": row(g_qn), "g_qr": pad_row(g_qr),
        "g_kv_lat": row(g_kv_lat), "g_kr": pad_row(g_kr), "g_kn": row(g_kn), "pool_scale": row(pool_scale),
        "w_lat": w_lat, "w_z": w_z, "w_pool_in": w_pool_in,
        "w_uq": _prep_uq_call(w_uq, 256),
        "w_ukv": w_ukv.reshape(KV_LORA, N_HEADS * (D_NOPE + D_V)).astype(BF16),
        "w_pool": w_pool.astype(BF16),
        "w_out": w_out.reshape(2, D_ATT, D_MODEL).astype(BF16),
    }


def _tile(n, pref):
    return pref if n % pref == 0 else n


def _mixer_layer(x, ckv_past, kr_past, pool_past, w):
    batch, t_len, _ = x.shape
    rows = batch * t_len
    pos0 = 0 if ckv_past is None else ckv_past.shape[1]
    x2 = x.reshape(rows, D_MODEL)

    h, q, ckv, kr = _latent_call(x2, _rope_tables(pos0, t_len), w, t_len, 256)
    gate = _gate_call(h, w["w_z"], 512, 1024)
    pool, pool_state = _pool_call(h, pool_past, w, batch, t_len, pos0, 512)
    tk = 512
    if ckv_past is None:
        k, vt = _kv_call(ckv, kr, w, tk)
        att = _attn_call(q, k, vt, None, None, gate, batch, t_len, t_len, pos0, _tile(t_len, 512), tk)
    else:
        k, vt = _kv_call(ckv_past.reshape(batch * pos0, KV_LORA), kr_past.reshape(batch * pos0, D_ROPE), w, tk)
        k_new, vt_new = _kv_call(ckv, kr, w, t_len)
        att = _attn_call(q, k, vt, k_new, vt_new, gate, batch, t_len, pos0, pos0, t_len, tk)
    y = _out_call(att, pool, w["w_out"], x2, 512, 1024)
    return (y.reshape(batch, t_len, D_MODEL), ckv.reshape(batch, t_len, KV_LORA),
            kr.reshape(batch, t_len, D_ROPE), pool_state)


def kernel(x_prompt, x_sample, cache_ckv, cache_krope, state_pool, g_norm, w_in, g_q_lat, w_uq, g_qn, g_qr,
           g_kv_lat, g_kr, w_ukv, g_kn, w_pool, pool_scale, w_out):
    depth = w_in.shape[0]
    batch = x_prompt.shape[0]
    yp, ys = x_prompt, x_sample
    outs = [[] for _ in range(6)]
    for l in range(depth):
        w = _prep_weights(g_norm[l], w_in[l], g_q_lat[l], w_uq[l], g_qn[l], g_qr[l], g_kv_lat[l], g_kr[l],
                          w_ukv[l], g_kn[l], w_pool[l], pool_scale[l], w_out[l])
        yp, a, b, c = _mixer_layer(yp, None, None, jnp.zeros((batch, POOL_BUF, D_POOL), F32), w)
        outs[0].append(a); outs[1].append(b); outs[2].append(c)
        ys, a, b, c = _mixer_layer(ys, cache_ckv[l], cache_krope[l], state_pool[l], w)
        outs[3].append(a); outs[4].append(b); outs[5].append(c)
    return (yp, ys) + tuple(jnp.stack(o) for o in outs)
```

```python
import functools

import numpy as np
import jax
import jax.numpy as jnp
from jax import lax
from jax.experimental import pallas as pl
from jax.experimental.pallas import tpu as pltpu

D_MODEL = 4096
CHUNK = 64
N_HEADS = 16
D_NOPE = 128
D_ROPE = 64
D_QK = D_NOPE + D_ROPE
D_V = 128
D_ATT = N_HEADS * D_V
Q_LORA = 1024
KV_LORA = 512
D_POOL = 2048
POOL_WINDOWS = (2, 4, 8, 16)
N_POOL_GROUPS = len(POOL_WINDOWS)
D_POOL_GROUP = D_POOL // N_POOL_GROUPS
POOL_BUF = max(POOL_WINDOWS) - 1
ROPE_THETA = 10000.0
EPS = 1e-6

O_Z = Q_LORA + KV_LORA + D_ROPE
O_U = O_Z + D_ATT
O_ZP = O_U + D_POOL
D_IN = O_ZP + D_POOL

LANES = 128
SUBLANES = 8
ROPE_PAD = LANES
QK_PAD = D_NOPE + ROPE_PAD
LAT_PAD = Q_LORA + KV_LORA + ROPE_PAD
HIST = 2 * (POOL_BUF + 1)
VMEM_LIMIT = 56 * 1024 * 1024

PREP_ROWS = 64
ROW_Z = 2048
ROW_POOL = ROW_Z + D_ATT
ROWS_ALL = ROW_POOL + 2 * D_POOL

F32 = jnp.float32
BF16 = jnp.bfloat16
_NT = (((1,), (1,)), ((), ()))


def _params(semantics):
    return pltpu.CompilerParams(dimension_semantics=semantics, vmem_limit_bytes=VMEM_LIMIT)


def _rms(x, g, n=None):
    n = x.shape[-1] if n is None else n
    ms = jnp.sum(x * x, axis=-1, keepdims=True) * (1.0 / n)
    return x * lax.rsqrt(ms + EPS) * g


def _rope(x, cos, sin_hi, sin_lo):
    return x * cos + pltpu.roll(x, D_ROPE // 2, 1) * sin_hi + pltpu.roll(x, ROPE_PAD - D_ROPE // 2, 1) * sin_lo


def _silu(z):
    return z * (1.0 / (1.0 + jnp.exp(-z)))


def _dot_nt(a, b):
    return lax.dot_general(a, b, _NT, preferred_element_type=F32)


def _prep_in_kernel(src_ref, w_ref, o_ref):
    n = pl.program_id(0)

    @pl.when(src_ref[n] >= 0)
    def _():
        o_ref[...] = w_ref[...].astype(BF16)

    @pl.when(src_ref[n] < 0)
    def _():
        o_ref[...] = jnp.zeros(o_ref.shape, BF16)


def _prep_in_schedule():
    per_group = D_POOL_GROUP // PREP_ROWS
    src = []
    for n in range(ROWS_ALL // PREP_ROWS):
        row = n * PREP_ROWS
        if row < O_Z:
            src.append(n)
        elif row < ROW_Z:
            src.append(-1)
        elif row < ROW_POOL:
            src.append((O_Z + row - ROW_Z) // PREP_ROWS)
        else:
            g, t = divmod((row - ROW_POOL) // PREP_ROWS, 2 * per_group)
            base = O_U if t < per_group else O_ZP - D_POOL_GROUP
            src.append((base + g * D_POOL_GROUP) // PREP_ROWS + t)
    return np.asarray(src, np.int32)


def _prep_in_call(w_in_t):
    src = _prep_in_schedule()
    grid_spec = pltpu.PrefetchScalarGridSpec(
        num_scalar_prefetch=1,
        grid=(len(src),),
        in_specs=[pl.BlockSpec((PREP_ROWS, D_MODEL), lambda n, src: (jnp.maximum(src[n], 0), 0))],
        out_specs=pl.BlockSpec((PREP_ROWS, D_MODEL), lambda n, src: (n, 0)),
    )
    return pl.pallas_call(
        _prep_in_kernel,
        grid_spec=grid_spec,
        out_shape=jax.ShapeDtypeStruct((ROWS_ALL, D_MODEL), BF16),
        compiler_params=_params(("arbitrary",)),
        name="prep_w_in",
    )(jnp.asarray(src), w_in_t)


def _prep_uq_kernel(w_ref, o_ref):
    o_ref[0, :D_QK, :] = w_ref[0].astype(BF16)
    o_ref[0, D_QK:, :] = jnp.zeros((QK_PAD - D_QK, Q_LORA), BF16)


def _prep_uq_call(w_uq_t):
    return pl.pallas_call(
        _prep_uq_kernel,
        grid=(N_HEADS,),
        in_specs=[pl.BlockSpec((1, D_QK, Q_LORA), lambda h: (h, 0, 0))],
        out_specs=pl.BlockSpec((1, QK_PAD, Q_LORA), lambda h: (h, 0, 0)),
        out_shape=jax.ShapeDtypeStruct((N_HEADS, QK_PAD, Q_LORA), BF16),
        compiler_params=_params(("arbitrary",)),
        name="prep_w_uq",
    )(w_uq_t)


def _latent_kernel(x_ref, gn_ref, wlat_ref, gql_ref, wuq_ref, gqn_ref, gqr_ref, gkv_ref, gkr_ref,
                   cos_ref, shi_ref, slo_ref, h_ref, q_ref, ckv_ref, kr_ref, *, scale):
    h = _rms(x_ref[...], gn_ref[...]).astype(BF16)
    h_ref[...] = h
    c = _dot_nt(h, wlat_ref[...])
    qlat = _rms(c[:, :Q_LORA], gql_ref[...]).astype(BF16)
    ckv_ref[...] = _rms(c[:, Q_LORA:Q_LORA + KV_LORA], gkv_ref[...])
    cos, shi, slo = cos_ref[...], shi_ref[...], slo_ref[...]
    kr = _rope(_rms(c[:, Q_LORA + KV_LORA:], gkr_ref[...], D_ROPE), cos, shi, slo)
    kr_ref[...] = kr[:, :D_ROPE]
    for hd in range(N_HEADS):
        q = _dot_nt(qlat, wuq_ref[hd])
        qn = _rms(q[:, :D_NOPE], gqn_ref[...]) * scale
        qr = _rope(_rms(q[:, D_NOPE:], gqr_ref[...], D_ROPE), cos, shi, slo) * scale
        q_ref[hd, :, :D_NOPE] = qn.astype(BF16)
        q_ref[hd, :, D_NOPE:] = qr.astype(BF16)


def _latent_call(x2, tabs, w, t_len, tm):
    rows = x2.shape[0]
    if tm > t_len:
        tabs = [jnp.tile(t, (tm // t_len, 1)) for t in tabs]
        t_len = tm
    nt = t_len // tm
    resident = lambda shape: pl.BlockSpec(shape, lambda i: (0,) * len(shape), pipeline_mode=pl.Buffered(1))
    tab_spec = pl.BlockSpec((tm, ROPE_PAD), lambda i: (i % nt, 0))
    return pl.pallas_call(
        functools.partial(_latent_kernel, scale=float(D_QK ** -0.5 * np.log2(np.e))),
        grid=(rows // tm,),
        in_specs=[
            pl.BlockSpec((tm, D_MODEL), lambda i: (i, 0)),
            resident((1, D_MODEL)),
            resident((LAT_PAD, D_MODEL)),
            resident((1, Q_LORA)),
            resident((N_HEADS, QK_PAD, Q_LORA)),
            resident((1, D_NOPE)),
            resident((1, ROPE_PAD)),
            resident((1, KV_LORA)),
            resident((1, ROPE_PAD)),
            tab_spec, tab_spec, tab_spec,
        ],
        out_specs=[
            pl.BlockSpec((tm, D_MODEL), lambda i: (i, 0)),
            pl.BlockSpec((N_HEADS, tm, QK_PAD), lambda i: (0, i, 0)),
            pl.BlockSpec((tm, KV_LORA), lambda i: (i, 0)),
            pl.BlockSpec((tm, D_ROPE), lambda i: (i, 0)),
        ],
        out_shape=[
            jax.ShapeDtypeStruct((rows, D_MODEL), BF16),
            jax.ShapeDtypeStruct((N_HEADS, rows, QK_PAD), BF16),
            jax.ShapeDtypeStruct((rows, KV_LORA), F32),
            jax.ShapeDtypeStruct((rows, D_ROPE), F32),
        ],
        compiler_params=_params(("arbitrary",)),
        name="latent",
    )(x2, w["g_norm"], w["w_in"], w["g_q_lat"], w["w_uq"], w["g_qn"], w["g_qr"], w["g_kv_lat"], w["g_kr"],
      *tabs)


def _gate_kernel(h_ref, w_ref, o_ref):
    o_ref[...] = _silu(_dot_nt(h_ref[...], w_ref[...]))


def _gate_call(h, w_all, tm, tn):
    rows = h.shape[0]
    first = ROW_Z // tn
    return pl.pallas_call(
        _gate_kernel,
        grid=(D_ATT // tn, rows // tm),
        in_specs=[pl.BlockSpec((tm, D_MODEL), lambda j, i: (i, 0)),
                  pl.BlockSpec((tn, D_MODEL), lambda j, i: (first + j, 0))],
        out_specs=pl.BlockSpec((tm, tn), lambda j, i: (i, j)),
        out_shape=jax.ShapeDtypeStruct((rows, D_ATT), F32),
        compiler_params=_params(("arbitrary", "arbitrary")),
        name="gate",
    )(h, w_all)


def _pool_kernel(h_ref, w_ref, past_ref, wp_ref, ps_ref, o_ref, st_ref, ext_ref, b1_ref, b2_ref, p_ref,
                 *, nb, tt, pos0):
    g = pl.program_id(0)
    i = pl.program_id(2)
    u = _dot_nt(h_ref[...], w_ref[:D_POOL_GROUP, :])

    @pl.when(i == 0)
    def _():
        for b in range(nb):
            ext_ref[b, 0:HIST - POOL_BUF, :] = jnp.zeros((HIST - POOL_BUF, D_POOL_GROUP), F32)
            ext_ref[b, HIST - POOL_BUF:HIST, :] = past_ref[b]

    @pl.when(i > 0)
    def _():
        for b in range(nb):
            ext_ref[b, 0:HIST, :] = ext_ref[b, tt:tt + HIST, :]

    for b in range(nb):
        ext_ref[b, HIST:HIST + tt, :] = u[b * tt:(b + 1) * tt]
    pos = pos0 + i * tt + lax.broadcasted_iota(jnp.int32, (tt, 1), 0)

    for gi, win in enumerate(POOL_WINDOWS):
        @pl.when(g == gi)
        def _(win=win):
            steps = win.bit_length() - 1
            src = ext_ref
            for k in range(steps):
                dst = (b1_ref, b2_ref)[k % 2]
                lo, sh = SUBLANES * (k + 1), 1 << k
                for b in range(nb):
                    dst[b, lo:HIST + tt, :] = src[b, lo:HIST + tt, :] + src[b, lo - sh:HIST + tt - sh, :]
                src = dst
            cnt = jnp.minimum(pos + 1, win).astype(F32)
            for b in range(nb):
                p_ref[b * tt:(b + 1) * tt, :] = (src[b, HIST:HIST + tt, :] / cnt - u[b * tt:(b + 1) * tt]).astype(BF16)
            gate = _silu(_dot_nt(h_ref[...], w_ref[D_POOL_GROUP:, :]))
            ph = jnp.dot(p_ref[...], wp_ref[0], preferred_element_type=F32)
            o_ref[...] = (ph * ps_ref[...] * gate).astype(BF16)

    for b in range(nb):
        st_ref[b] = ext_ref[b, HIST + tt - POOL_BUF:HIST + tt, :]


def _pool_call(h, pool_past, w, batch, t_len, pos0, tm):
    rows = h.shape[0]
    tt = min(tm, t_len)
    nb = tm // tt
    nt = t_len // tt
    first = ROW_POOL // (2 * D_POOL_GROUP)
    buf = pltpu.VMEM((nb, tt + HIST, D_POOL_GROUP), F32)
    return pl.pallas_call(
        functools.partial(_pool_kernel, nb=nb, tt=tt, pos0=pos0),
        grid=(N_POOL_GROUPS, batch // nb, nt),
        in_specs=[
            pl.BlockSpec((tm, D_MODEL), lambda g, b, i: (b * nt + i, 0)),
            pl.BlockSpec((2 * D_POOL_GROUP, D_MODEL), lambda g, b, i: (first + g, 0)),
            pl.BlockSpec((nb, POOL_BUF, D_POOL_GROUP), lambda g, b, i: (b, 0, g)),
            pl.BlockSpec((1, D_POOL_GROUP, D_POOL_GROUP), lambda g, b, i: (g, 0, 0)),
            pl.BlockSpec((1, D_POOL_GROUP), lambda g, b, i: (0, g)),
        ],
        out_specs=[
            pl.BlockSpec((tm, D_POOL_GROUP), lambda g, b, i: (b * nt + i, g)),
            pl.BlockSpec((nb, POOL_BUF, D_POOL_GROUP), lambda g, b, i: (b, 0, g)),
        ],
        out_shape=[
            jax.ShapeDtypeStruct((rows, D_POOL), BF16),
            jax.ShapeDtypeStruct((batch, POOL_BUF, D_POOL), F32),
        ],
        scratch_shapes=[buf, buf, buf, pltpu.VMEM((tm, D_POOL_GROUP), BF16)],
        compiler_params=_params(("arbitrary", "arbitrary", "arbitrary")),
        name="pool",
    )(h, w["w_in"], pool_past, w["w_pool"], w["pool_scale"])


def _kv_kernel(ckv_ref, kr_ref, wk_ref, wv_ref, gkn_ref, k_ref, v_ref):
    ckv = ckv_ref[...].astype(BF16)
    kr = kr_ref[...].astype(BF16)
    tm = ckv.shape[0]
    pair = 2 * D_NOPE
    for j in range(N_HEADS // 2):
        kk = jnp.dot(ckv, wk_ref[:, j * pair:(j + 1) * pair], preferred_element_type=F32)
        for t in range(2):
            hd = 2 * j + t
            k_ref[hd, :, :D_NOPE] = _rms(kk[:, t * D_NOPE:(t + 1) * D_NOPE], gkn_ref[...]).astype(BF16)
            k_ref[hd, :, D_NOPE:D_QK] = kr
            k_ref[hd, :, D_QK:] = jnp.zeros((tm, QK_PAD - D_QK), BF16)
    vt = _dot_nt(wv_ref[...], ckv)
    for hd in range(N_HEADS):
        v_ref[0, hd] = vt[hd * D_V:(hd + 1) * D_V].astype(BF16)


def _kv_call(ckv, kr, w, tm):
    rows = ckv.shape[0]
    return pl.pallas_call(
        _kv_kernel,
        grid=(rows // tm,),
        in_specs=[
            pl.BlockSpec((tm, KV_LORA), lambda i: (i, 0)),
            pl.BlockSpec((tm, D_ROPE), lambda i: (i, 0)),
            pl.BlockSpec((KV_LORA, N_HEADS * D_NOPE), lambda i: (0, 0)),
            pl.BlockSpec((N_HEADS * D_V, KV_LORA), lambda i: (0, 0)),
            pl.BlockSpec((1, D_NOPE), lambda i: (0, 0)),
        ],
        out_specs=[
            pl.BlockSpec((N_HEADS, tm, QK_PAD), lambda i: (0, i, 0)),
            pl.BlockSpec((1, N_HEADS, D_V, tm), lambda i: (i, 0, 0, 0)),
        ],
        out_shape=[
            jax.ShapeDtypeStruct((N_HEADS, rows, QK_PAD), BF16),
            jax.ShapeDtypeStruct((rows // tm, N_HEADS, D_V, tm), BF16),
        ],
        compiler_params=_params(("arbitrary",)),
        name="kv",
    )(ckv, kr, w["w_uk"], w["w_uv"], w["g_kn"])


def _attn_kernel(qi_ref, kj_ref, flag_ref, q_ref, k_ref, v_ref, *rest, tq, tk, pos0, has_new):
    if has_new:
        kn_ref, vn_ref, g_ref, o_ref, m_ref, l_ref, acc_ref, bias_ref, sa_ref, sb_ref = rest
    else:
        g_ref, o_ref, m_ref, l_ref, acc_ref, bias_ref, sa_ref, sb_ref = rest
    n = pl.program_id(1)
    qi, kj, flag = qi_ref[n], kj_ref[n], flag_ref[n]
    masked = (flag & 1) == 1
    last = (flag & 2) == 2

    @pl.when(kj == 0)
    def _():
        m_ref[...] = jnp.full(m_ref.shape, -jnp.inf, F32)
        l_ref[...] = jnp.zeros(l_ref.shape, F32)
        acc_ref[...] = jnp.zeros(acc_ref.shape, F32)

    def block(kref, vref, use_bias):
        nkeys = kref.shape[1]
        bufs = (sa_ref.at[0:nkeys], sb_ref.at[0:nkeys])

        def scores(hd, s_ref):
            st = _dot_nt(kref[hd], q_ref[hd])
            s_ref[...] = st + bias_ref[...] if use_bias else st

        def softmax_pv(hd, s_ref):
            m_prev = m_ref[hd]
            m_new = jnp.maximum(m_prev, jnp.max(s_ref[...], axis=0, keepdims=True))
            alpha = jnp.exp2(m_prev - m_new)
            p = jnp.exp2(s_ref[...] - m_new)
            l_ref[hd] = alpha * l_ref[hd] + jnp.sum(p, axis=0, keepdims=True)
            acc_ref[hd] = alpha * acc_ref[hd] + jnp.dot(vref[0, hd], p.astype(BF16),
                                                         preferred_element_type=F32)
            m_ref[hd] = m_new

        scores(0, bufs[0])
        for hd in range(N_HEADS):
            if hd + 1 < N_HEADS:
                scores(hd + 1, bufs[(hd + 1) % 2])
            softmax_pv(hd, bufs[hd % 2])

    @pl.when(masked)
    def _():
        shift = CHUNK.bit_length() - 1
        kc = (kj * tk + lax.broadcasted_iota(jnp.int32, (tk, 1), 0)) >> shift
        qc = (pos0 + qi * tq + lax.broadcasted_iota(jnp.int32, (1, tq), 1)) >> shift
        bias_ref[...] = jnp.where(kc <= qc, 0.0, -jnp.inf).astype(F32)
        block(k_ref, v_ref, True)

    @pl.when(jnp.logical_not(masked))
    def _():
        block(k_ref, v_ref, False)

    if has_new:
        @pl.when(last)
        def _():
            block(kn_ref, vn_ref, False)

    @pl.when(last)
    def _():
        for hd in range(N_HEADS):
            att = (acc_ref[hd] * (1.0 / l_ref[hd])).T
            o_ref[:, hd * D_V:(hd + 1) * D_V] = (att * g_ref[:, hd * D_V:(hd + 1) * D_V]).astype(BF16)


def _attn_schedule(t_len, n_keys, pos0, tq, tk):
    qi, kj, flag = [], [], []
    nk = n_keys // tk
    for i in range(t_len // tq):
        first_chunk = (pos0 + i * tq) // CHUNK
        last_visible = ((pos0 + (i + 1) * tq - 1) // CHUNK) * CHUNK + CHUNK - 1
        last_blk = min(nk - 1, last_visible // tk)
        for j in range(last_blk + 1):
            needs_mask = ((j + 1) * tk - 1) // CHUNK > first_chunk
            qi.append(i)
            kj.append(j)
            flag.append(int(needs_mask) + 2 * int(j == last_blk))
    return tuple(np.asarray(a, np.int32) for a in (qi, kj, flag))


def _attn_call(q, k, vt, k_new, vt_new, gate, batch, t_len, n_keys, pos0, tq, tk):
    rows = batch * t_len
    nq, nk = t_len // tq, n_keys // tk
    has_new = k_new is not None
    qi, kj, flag = _attn_schedule(t_len, n_keys, pos0, tq, tk)
    row_map = lambda b, n, qi, kj, fl: (b * nq + qi[n], 0)
    in_specs = [
        pl.BlockSpec((N_HEADS, tq, QK_PAD), lambda b, n, qi, kj, fl: (0, b * nq + qi[n], 0)),
        pl.BlockSpec((N_HEADS, tk, QK_PAD), lambda b, n, qi, kj, fl: (0, b * nk + kj[n], 0)),
        pl.BlockSpec((1, N_HEADS, D_V, tk), lambda b, n, qi, kj, fl: (b * nk + kj[n], 0, 0, 0)),
    ]
    args = [q, k, vt]
    if has_new:
        assert pos0 % CHUNK == 0 and t_len <= CHUNK and tq == t_len
        in_specs += [
            pl.BlockSpec((N_HEADS, t_len, QK_PAD), lambda b, n, qi, kj, fl: (0, b, 0)),
            pl.BlockSpec((1, N_HEADS, D_V, t_len), lambda b, n, qi, kj, fl: (b, 0, 0, 0)),
        ]
        args += [k_new, vt_new]
    in_specs.append(pl.BlockSpec((tq, D_ATT), row_map))
    args.append(gate)
    grid_spec = pltpu.PrefetchScalarGridSpec(
        num_scalar_prefetch=3,
        grid=(batch, len(qi)),
        in_specs=in_specs,
        out_specs=pl.BlockSpec((tq, D_ATT), row_map),
        scratch_shapes=[
            pltpu.VMEM((N_HEADS, 1, tq), F32),
            pltpu.VMEM((N_HEADS, 1, tq), F32),
            pltpu.VMEM((N_HEADS, D_V, tq), F32),
            pltpu.VMEM((tk, tq), F32),
            pltpu.VMEM((tk, tq), F32),
            pltpu.VMEM((tk, tq), F32),
        ],
    )
    return pl.pallas_call(
        functools.partial(_attn_kernel, tq=tq, tk=tk, pos0=pos0, has_new=has_new),
        grid_spec=grid_spec,
        out_shape=jax.ShapeDtypeStruct((rows, D_ATT), BF16),
        compiler_params=_params(("arbitrary", "arbitrary")),
        name="attn",
    )(jnp.asarray(qi), jnp.asarray(kj), jnp.asarray(flag), *args)


def _out_kernel(a_ref, p_ref, w_ref, x_ref, y_ref):
    y_ref[...] = x_ref[...] + (jnp.dot(a_ref[...], w_ref[0], preferred_element_type=F32)
                               + jnp.dot(p_ref[...], w_ref[1], preferred_element_type=F32))


def _out_call(att, pool, w_out, x2, tm, tn):
    rows = x2.shape[0]
    return pl.pallas_call(
        _out_kernel,
        grid=(D_MODEL // tn, rows // tm),
        in_specs=[
            pl.BlockSpec((tm, D_ATT), lambda j, i: (i, 0)),
            pl.BlockSpec((tm, D_POOL), lambda j, i: (i, 0)),
            pl.BlockSpec((2, D_ATT, tn), lambda j, i: (0, 0, j)),
            pl.BlockSpec((tm, tn), lambda j, i: (i, j)),
        ],
        out_specs=pl.BlockSpec((tm, tn), lambda j, i: (i, j)),
        out_shape=jax.ShapeDtypeStruct((rows, D_MODEL), F32),
        compiler_params=_params(("arbitrary", "arbitrary")),
        name="out_proj",
    )(att, pool, w_out, x2)


def _rope_tables(pos0, t_len):
    half = D_ROPE // 2
    freqs = ROPE_THETA ** (-jnp.arange(half, dtype=F32) / half)
    ang = (pos0 + jnp.arange(t_len, dtype=jnp.int32)).astype(F32)[:, None] * freqs[None, :]
    cos, sin = jnp.cos(ang), jnp.sin(ang)
    zero = jnp.zeros_like(cos)
    pad = jnp.zeros((t_len, ROPE_PAD - D_ROPE), F32)
    return (jnp.concatenate([cos, cos, pad], axis=1),
            jnp.concatenate([zero, sin, pad], axis=1),
            jnp.concatenate([-sin, zero, pad], axis=1))


def _prep_weights(g_norm, w_in, g_q_lat, w_uq, g_qn, g_qr, g_kv_lat, g_kr, w_ukv, g_kn,
                  w_pool, pool_scale, w_out):
    row = lambda v: v.reshape(1, -1).astype(F32)
    pad_row = lambda v: jnp.pad(v, (0, ROPE_PAD - D_ROPE)).reshape(1, -1).astype(F32)
    return {
        "g_norm": row(g_norm), "g_q_lat": row(g_q_lat), "g_qn": row(g_qn), "g_qr": pad_row(g_qr),
        "g_kv_lat": row(g_kv_lat), "g_kr": pad_row(g_kr), "g_kn": row(g_kn), "pool_scale": row(pool_scale),
        "w_in": _prep_in_call(jnp.transpose(w_in)),
        "w_uq": _prep_uq_call(jnp.transpose(w_uq, (1, 2, 0))),
        "w_uk": w_ukv[:, :, :D_NOPE].reshape(KV_LORA, N_HEADS * D_NOPE).astype(BF16),
        "w_uv": jnp.transpose(w_ukv[:, :, D_NOPE:], (1, 2, 0)).reshape(N_HEADS * D_V, KV_LORA).astype(BF16),
        "w_pool": w_pool.astype(BF16),
        "w_out": w_out.reshape(2, D_ATT, D_MODEL).astype(BF16),
    }


def _mixer_layer(x, ckv_past, kr_past, pool_past, w):
    batch, t_len, _ = x.shape
    rows = batch * t_len
    pos0 = 0 if ckv_past is None else ckv_past.shape[1]
    x2 = x.reshape(rows, D_MODEL)

    h, q, ckv, kr = _latent_call(x2, _rope_tables(pos0, t_len), w, t_len, 256)
    gate = _gate_call(h, w["w_in"], 512, 1024)
    pool, pool_state = _pool_call(h, pool_past, w, batch, t_len, pos0, 512)
    tk = 512
    if ckv_past is None:
        k, vt = _kv_call(ckv, kr, w, tk)
        att = _attn_call(q, k, vt, None, None, gate, batch, t_len, t_len, pos0, 512, tk)
    else:
        k, vt = _kv_call(ckv_past.reshape(batch * pos0, KV_LORA), kr_past.reshape(batch * pos0, D_ROPE), w, tk)
        k_new, vt_new = _kv_call(ckv, kr, w, t_len)
        att = _attn_call(q, k, vt, k_new, vt_new, gate, batch, t_len, pos0, pos0, t_len, tk)
    y = _out_call(att, pool, w["w_out"], x2, 512, 1024)
    return (y.reshape(batch, t_len, D_MODEL), ckv.reshape(batch, t_len, KV_LORA),
            kr.reshape(batch, t_len, D_ROPE), pool_state)


def kernel(x_prompt, x_sample, cache_ckv, cache_krope, state_pool, g_norm, w_in, g_q_lat, w_uq, g_qn, g_qr,
           g_kv_lat, g_kr, w_ukv, g_kn, w_pool, pool_scale, w_out):
    depth = w_in.shape[0]
    batch = x_prompt.shape[0]
    yp, ys = x_prompt, x_sample
    outs = [[] for _ in range(6)]
    for l in range(depth):
        w = _prep_weights(g_norm[l], w_in[l], g_q_lat[l], w_uq[l], g_qn[l], g_qr[l], g_kv_lat[l], g_kr[l],
                          w_ukv[l], g_kn[l], w_pool[l], pool_scale[l], w_out[l])
        yp, a, b, c = _mixer_layer(yp, None, None, jnp.zeros((batch, POOL_BUF, D_POOL), F32), w)
        outs[0].append(a); outs[1].append(b); outs[2].append(c)
        ys, a, b, c = _mixer_layer(ys, cache_ckv[l], cache_krope[l], state_pool[l], w)
        outs[3].append(a); outs[4].append(b); outs[5].append(c)
    return (yp, ys) + tuple(jnp.stack(o) for o in outs)
```

```python
import functools

import numpy as np
import jax
import jax.numpy as jnp
from jax import lax
from jax.experimental import pallas as pl
from jax.experimental.pallas import tpu as pltpu

D_MODEL = 4096
CHUNK = 64
N_HEADS = 16
D_NOPE = 128
D_ROPE = 64
D_QK = D_NOPE + D_ROPE
D_V = 128
D_ATT = N_HEADS * D_V
Q_LORA = 1024
KV_LORA = 512
D_POOL = 2048
POOL_WINDOWS = (2, 4, 8, 16)
N_POOL_GROUPS = len(POOL_WINDOWS)
D_POOL_GROUP = D_POOL // N_POOL_GROUPS
POOL_BUF = max(POOL_WINDOWS) - 1
ROPE_THETA = 10000.0
EPS = 1e-6

O_Z = Q_LORA + KV_LORA + D_ROPE
O_U = O_Z + D_ATT
O_ZP = O_U + D_POOL
D_IN = O_ZP + D_POOL

LANES = 128
SUBLANES = 8
ROPE_PAD = LANES
QK_PAD = D_NOPE + ROPE_PAD
LAT_PAD = Q_LORA + KV_LORA + ROPE_PAD
HIST = 2 * (POOL_BUF + 1)
VMEM_LIMIT = 56 * 1024 * 1024

PREP_ROWS = 64
ROW_Z = 2048
ROW_POOL = ROW_Z + D_ATT
ROWS_ALL = ROW_POOL + 2 * D_POOL

F32 = jnp.float32
BF16 = jnp.bfloat16
_NT = (((1,), (1,)), ((), ()))


def _params(semantics):
    return pltpu.CompilerParams(dimension_semantics=semantics, vmem_limit_bytes=VMEM_LIMIT)


def _rms(x, g, n=None):
    n = x.shape[-1] if n is None else n
    ms = jnp.sum(x * x, axis=-1, keepdims=True) * (1.0 / n)
    return x * lax.rsqrt(ms + EPS) * g


def _rope(x, cos, sin_hi, sin_lo):
    return x * cos + pltpu.roll(x, D_ROPE // 2, 1) * sin_hi + pltpu.roll(x, ROPE_PAD - D_ROPE // 2, 1) * sin_lo


def _silu(z):
    return z * (1.0 / (1.0 + jnp.exp(-z)))


def _dot_nt(a, b):
    return lax.dot_general(a, b, _NT, preferred_element_type=F32)


def _prep_in_kernel(src_ref, w_ref, o_ref):
    n = pl.program_id(0)

    @pl.when(src_ref[n] >= 0)
    def _():
        o_ref[...] = w_ref[...].astype(BF16)

    @pl.when(src_ref[n] < 0)
    def _():
        o_ref[...] = jnp.zeros(o_ref.shape, BF16)


def _prep_in_schedule():
    per_group = D_POOL_GROUP // PREP_ROWS
    src = []
    for n in range(ROWS_ALL // PREP_ROWS):
        row = n * PREP_ROWS
        if row < O_Z:
            src.append(n)
        elif row < ROW_Z:
            src.append(-1)
        elif row < ROW_POOL:
            src.append((O_Z + row - ROW_Z) // PREP_ROWS)
        else:
            g, t = divmod((row - ROW_POOL) // PREP_ROWS, 2 * per_group)
            base = O_U if t < per_group else O_ZP - D_POOL_GROUP
            src.append((base + g * D_POOL_GROUP) // PREP_ROWS + t)
    return np.asarray(src, np.int32)


def _prep_in_call(w_in_t):
    src = _prep_in_schedule()
    grid_spec = pltpu.PrefetchScalarGridSpec(
        num_scalar_prefetch=1,
        grid=(len(src),),
        in_specs=[pl.BlockSpec((PREP_ROWS, D_MODEL), lambda n, src: (jnp.maximum(src[n], 0), 0))],
        out_specs=pl.BlockSpec((PREP_ROWS, D_MODEL), lambda n, src: (n, 0)),
    )
    return pl.pallas_call(
        _prep_in_kernel,
        grid_spec=grid_spec,
        out_shape=jax.ShapeDtypeStruct((ROWS_ALL, D_MODEL), BF16),
        compiler_params=_params(("arbitrary",)),
        name="prep_w_in",
    )(jnp.asarray(src), w_in_t)


def _prep_uq_kernel(w_ref, o_ref):
    o_ref[0, :D_QK, :] = w_ref[0].astype(BF16)
    o_ref[0, D_QK:, :] = jnp.zeros((QK_PAD - D_QK, Q_LORA), BF16)


def _prep_uq_call(w_uq_t):
    return pl.pallas_call(
        _prep_uq_kernel,
        grid=(N_HEADS,),
        in_specs=[pl.BlockSpec((1, D_QK, Q_LORA), lambda h: (h, 0, 0))],
        out_specs=pl.BlockSpec((1, QK_PAD, Q_LORA), lambda h: (h, 0, 0)),
        out_shape=jax.ShapeDtypeStruct((N_HEADS, QK_PAD, Q_LORA), BF16),
        compiler_params=_params(("arbitrary",)),
        name="prep_w_uq",
    )(w_uq_t)


def _latent_kernel(x_ref, gn_ref, wlat_ref, gql_ref, wuq_ref, gqn_ref, gqr_ref, gkv_ref, gkr_ref,
                   cos_ref, shi_ref, slo_ref, h_ref, q_ref, ckv_ref, kr_ref, *, scale):
    h = _rms(x_ref[...], gn_ref[...]).astype(BF16)
    h_ref[...] = h
    c = _dot_nt(h, wlat_ref[...])
    qlat = _rms(c[:, :Q_LORA], gql_ref[...]).astype(BF16)
    ckv_ref[...] = _rms(c[:, Q_LORA:Q_LORA + KV_LORA], gkv_ref[...])
    cos, shi, slo = cos_ref[...], shi_ref[...], slo_ref[...]
    kr = _rope(_rms(c[:, Q_LORA + KV_LORA:], gkr_ref[...], D_ROPE), cos, shi, slo)
    kr_ref[...] = kr[:, :D_ROPE]
    for hd in range(N_HEADS):
        q = _dot_nt(qlat, wuq_ref[hd])
        qn = _rms(q[:, :D_NOPE], gqn_ref[...]) * scale
        qr = _rope(_rms(q[:, D_NOPE:], gqr_ref[...], D_ROPE), cos, shi, slo) * scale
        q_ref[hd, :, :D_NOPE] = qn.astype(BF16)
        q_ref[hd, :, D_NOPE:] = qr.astype(BF16)


def _latent_call(x2, tabs, w, t_len, tm):
    rows = x2.shape[0]
    if tm > t_len:
        tabs = [jnp.tile(t, (tm // t_len, 1)) for t in tabs]
        t_len = tm
    nt = t_len // tm
    resident = lambda shape: pl.BlockSpec(shape, lambda i: (0,) * len(shape), pipeline_mode=pl.Buffered(1))
    tab_spec = pl.BlockSpec((tm, ROPE_PAD), lambda i: (i % nt, 0))
    return pl.pallas_call(
        functools.partial(_latent_kernel, scale=float(D_QK ** -0.5 * np.log2(np.e))),
        grid=(rows // tm,),
        in_specs=[
            pl.BlockSpec((tm, D_MODEL), lambda i: (i, 0)),
            resident((1, D_MODEL)),
            resident((LAT_PAD, D_MODEL)),
            resident((1, Q_LORA)),
            resident((N_HEADS, QK_PAD, Q_LORA)),
            resident((1, D_NOPE)),
            resident((1, ROPE_PAD)),
            resident((1, KV_LORA)),
            resident((1, ROPE_PAD)),
            tab_spec, tab_spec, tab_spec,
        ],
        out_specs=[
            pl.BlockSpec((tm, D_MODEL), lambda i: (i, 0)),
            pl.BlockSpec((N_HEADS, tm, QK_PAD), lambda i: (0, i, 0)),
            pl.BlockSpec((tm, KV_LORA), lambda i: (i, 0)),
            pl.BlockSpec((tm, D_ROPE), lambda i: (i, 0)),
        ],
        out_shape=[
            jax.ShapeDtypeStruct((rows, D_MODEL), BF16),
            jax.ShapeDtypeStruct((N_HEADS, rows, QK_PAD), BF16),
            jax.ShapeDtypeStruct((rows, KV_LORA), F32),
            jax.ShapeDtypeStruct((rows, D_ROPE), F32),
        ],
        compiler_params=_params(("arbitrary",)),
        name="latent",
    )(x2, w["g_norm"], w["w_in"], w["g_q_lat"], w["w_uq"], w["g_qn"], w["g_qr"], w["g_kv_lat"], w["g_kr"],
      *tabs)


def _gate_kernel(h_ref, w_ref, o_ref):
    o_ref[...] = _silu(_dot_nt(h_ref[...], w_ref[...]))


def _gate_call(h, w_all, tm, tn):
    rows = h.shape[0]
    first = ROW_Z // tn
    return pl.pallas_call(
        _gate_kernel,
        grid=(D_ATT // tn, rows // tm),
        in_specs=[pl.BlockSpec((tm, D_MODEL), lambda j, i: (i, 0)),
                  pl.BlockSpec((tn, D_MODEL), lambda j, i: (first + j, 0))],
        out_specs=pl.BlockSpec((tm, tn), lambda j, i: (i, j)),
        out_shape=jax.ShapeDtypeStruct((rows, D_ATT), F32),
        compiler_params=_params(("arbitrary", "arbitrary")),
        name="gate",
    )(h, w_all)


def _pool_kernel(h_ref, w_ref, past_ref, wp_ref, ps_ref, o_ref, st_ref, ext_ref, b1_ref, b2_ref, p_ref,
                 *, nb, tt, pos0):
    g = pl.program_id(0)
    i = pl.program_id(2)
    u = _dot_nt(h_ref[...], w_ref[:D_POOL_GROUP, :])

    @pl.when(i == 0)
    def _():
        for b in range(nb):
            ext_ref[b, 0:HIST - POOL_BUF, :] = jnp.zeros((HIST - POOL_BUF, D_POOL_GROUP), F32)
            ext_ref[b, HIST - POOL_BUF:HIST, :] = past_ref[b]

    @pl.when(i > 0)
    def _():
        for b in range(nb):
            ext_ref[b, 0:HIST, :] = ext_ref[b, tt:tt + HIST, :]

    for b in range(nb):
        ext_ref[b, HIST:HIST + tt, :] = u[b * tt:(b + 1) * tt]
    pos = pos0 + i * tt + lax.broadcasted_iota(jnp.int32, (tt, 1), 0)

    for gi, win in enumerate(POOL_WINDOWS):
        @pl.when(g == gi)
        def _(win=win):
            steps = win.bit_length() - 1
            src = ext_ref
            for k in range(steps):
                dst = (b1_ref, b2_ref)[k % 2]
                lo, sh = SUBLANES * (k + 1), 1 << k
                for b in range(nb):
                    dst[b, lo:HIST + tt, :] = src[b, lo:HIST + tt, :] + src[b, lo - sh:HIST + tt - sh, :]
                src = dst
            cnt = jnp.minimum(pos + 1, win).astype(F32)
            for b in range(nb):
                p_ref[b * tt:(b + 1) * tt, :] = (src[b, HIST:HIST + tt, :] / cnt - u[b * tt:(b + 1) * tt]).astype(BF16)
            gate = _silu(_dot_nt(h_ref[...], w_ref[D_POOL_GROUP:, :]))
            ph = jnp.dot(p_ref[...], wp_ref[0], preferred_element_type=F32)
            o_ref[...] = (ph * ps_ref[...] * gate).astype(BF16)

    for b in range(nb):
        st_ref[b] = ext_ref[b, HIST + tt - POOL_BUF:HIST + tt, :]


def _pool_call(h, pool_past, w, batch, t_len, pos0, tm):
    rows = h.shape[0]
    tt = min(tm, t_len)
    nb = tm // tt
    nt = t_len // tt
    first = ROW_POOL // (2 * D_POOL_GROUP)
    buf = pltpu.VMEM((nb, tt + HIST, D_POOL_GROUP), F32)
    return pl.pallas_call(
        functools.partial(_pool_kernel, nb=nb, tt=tt, pos0=pos0),
        grid=(N_POOL_GROUPS, batch // nb, nt),
        in_specs=[
            pl.BlockSpec((tm, D_MODEL), lambda g, b, i: (b * nt + i, 0)),
            pl.BlockSpec((2 * D_POOL_GROUP, D_MODEL), lambda g, b, i: (first + g, 0)),
            pl.BlockSpec((nb, POOL_BUF, D_POOL_GROUP), lambda g, b, i: (b, 0, g)),
            pl.BlockSpec((1, D_POOL_GROUP, D_POOL_GROUP), lambda g, b, i: (g, 0, 0)),
            pl.BlockSpec((1, D_POOL_GROUP), lambda g, b, i: (0, g)),
        ],
        out_specs=[
            pl.BlockSpec((tm, D_POOL_GROUP), lambda g, b, i: (b * nt + i, g)),
            pl.BlockSpec((nb, POOL_BUF, D_POOL_GROUP), lambda g, b, i: (b, 0, g)),
        ],
        out_shape=[
            jax.ShapeDtypeStruct((rows, D_POOL), BF16),
            jax.ShapeDtypeStruct((batch, POOL_BUF, D_POOL), F32),
        ],
        scratch_shapes=[buf, buf, buf, pltpu.VMEM((tm, D_POOL_GROUP), BF16)],
        compiler_params=_params(("arbitrary", "arbitrary", "arbitrary")),
        name="pool",
    )(h, w["w_in"], pool_past, w["w_pool"], w["pool_scale"])


def _kv_kernel(ckv_ref, kr_ref, wk_ref, wv_ref, gkn_ref, k_ref, v_ref):
    ckv = ckv_ref[...].astype(BF16)
    kr = kr_ref[...].astype(BF16)
    tm = ckv.shape[0]
    pair = 2 * D_NOPE
    for j in range(N_HEADS // 2):
        kk = jnp.dot(ckv, wk_ref[:, j * pair:(j + 1) * pair], preferred_element_type=F32)
        for t in range(2):
            hd = 2 * j + t
            k_ref[hd, :, :D_NOPE] = _rms(kk[:, t * D_NOPE:(t + 1) * D_NOPE], gkn_ref[...]).astype(BF16)
            k_ref[hd, :, D_NOPE:D_QK] = kr
            k_ref[hd, :, D_QK:] = jnp.zeros((tm, QK_PAD - D_QK), BF16)
    vt = _dot_nt(wv_ref[...], ckv)
    for hd in range(N_HEADS):
        v_ref[0, hd] = vt[hd * D_V:(hd + 1) * D_V].astype(BF16)


def _kv_call(ckv, kr, w, tm):
    rows = ckv.shape[0]
    return pl.pallas_call(
        _kv_kernel,
        grid=(rows // tm,),
        in_specs=[
            pl.BlockSpec((tm, KV_LORA), lambda i: (i, 0)),
            pl.BlockSpec((tm, D_ROPE), lambda i: (i, 0)),
            pl.BlockSpec((KV_LORA, N_HEADS * D_NOPE), lambda i: (0, 0)),
            pl.BlockSpec((N_HEADS * D_V, KV_LORA), lambda i: (0, 0)),
            pl.BlockSpec((1, D_NOPE), lambda i: (0, 0)),
        ],
        out_specs=[
            pl.BlockSpec((N_HEADS, tm, QK_PAD), lambda i: (0, i, 0)),
            pl.BlockSpec((1, N_HEADS, D_V, tm), lambda i: (i, 0, 0, 0)),
        ],
        out_shape=[
            jax.ShapeDtypeStruct((N_HEADS, rows, QK_PAD), BF16),
            jax.ShapeDtypeStruct((rows // tm, N_HEADS, D_V, tm), BF16),
        ],
        compiler_params=_params(("arbitrary",)),
        name="kv",
    )(ckv, kr, w["w_uk"], w["w_uv"], w["g_kn"])


def _attn_kernel(qi_ref, kj_ref, flag_ref, q_ref, k_ref, v_ref, g_ref, o_ref,
                 m_ref, l_ref, acc_ref, bias_ref, sa_ref, sb_ref, *, tq, tk):
    n = pl.program_id(1)
    qi, kj, flag = qi_ref[n], kj_ref[n], flag_ref[n]
    masked = (flag & 1) == 1
    last = (flag & 2) == 2

    @pl.when(kj == 0)
    def _():
        m_ref[...] = jnp.full(m_ref.shape, -jnp.inf, F32)
        l_ref[...] = jnp.zeros(l_ref.shape, F32)
        acc_ref[...] = jnp.zeros(acc_ref.shape, F32)

    def block(use_bias):
        bufs = (sa_ref, sb_ref)

        def scores(hd, s_ref):
            st = _dot_nt(k_ref[hd], q_ref[hd])
            s_ref[...] = st + bias_ref[...] if use_bias else st

        def softmax_pv(hd, s_ref):
            m_prev = m_ref[hd]
            m_new = jnp.maximum(m_prev, jnp.max(s_ref[...], axis=0, keepdims=True))
            alpha = jnp.exp2(m_prev - m_new)
            p = jnp.exp2(s_ref[...] - m_new)
            l_ref[hd] = alpha * l_ref[hd] + jnp.sum(p, axis=0, keepdims=True)
            acc_ref[hd] = alpha * acc_ref[hd] + jnp.dot(v_ref[0, hd], p.astype(BF16),
                                                         preferred_element_type=F32)
            m_ref[hd] = m_new

        scores(0, bufs[0])
        for hd in range(N_HEADS):
            if hd + 1 < N_HEADS:
                scores(hd + 1, bufs[(hd + 1) % 2])
            softmax_pv(hd, bufs[hd % 2])

    @pl.when(masked)
    def _():
        shift = CHUNK.bit_length() - 1
        kc = (kj * tk + lax.broadcasted_iota(jnp.int32, (tk, 1), 0)) >> shift
        qc = (qi * tq + lax.broadcasted_iota(jnp.int32, (1, tq), 1)) >> shift
        bias_ref[...] = jnp.where(kc <= qc, 0.0, -jnp.inf).astype(F32)
        block(True)

    @pl.when(jnp.logical_not(masked))
    def _():
        block(False)

    @pl.when(last)
    def _():
        for hd in range(N_HEADS):
            att = (acc_ref[hd] * (1.0 / l_ref[hd])).T
            o_ref[:, hd * D_V:(hd + 1) * D_V] = (att * g_ref[:, hd * D_V:(hd + 1) * D_V]).astype(BF16)


def _attn_schedule(t_len, tq, tk):
    qi, kj, flag = [], [], []
    for i in range(t_len // tq):
        first_chunk = (i * tq) // CHUNK
        last_blk = ((i + 1) * tq - 1) // tk
        for j in range(last_blk + 1):
            needs_mask = ((j + 1) * tk - 1) // CHUNK > first_chunk
            qi.append(i)
            kj.append(j)
            flag.append(int(needs_mask) + 2 * int(j == last_blk))
    return tuple(np.asarray(a, np.int32) for a in (qi, kj, flag))


def _attn_call(q, k, vt, gate, batch, t_len, tq, tk):
    assert tq % CHUNK == 0 and t_len % tq == 0 and t_len % tk == 0
    rows = batch * t_len
    nq, nk = t_len // tq, t_len // tk
    qi, kj, flag = _attn_schedule(t_len, tq, tk)
    row_map = lambda b, n, qi, kj, fl: (b * nq + qi[n], 0)
    grid_spec = pltpu.PrefetchScalarGridSpec(
        num_scalar_prefetch=3,
        grid=(batch, len(qi)),
        in_specs=[
            pl.BlockSpec((N_HEADS, tq, QK_PAD), lambda b, n, qi, kj, fl: (0, b * nq + qi[n], 0)),
            pl.BlockSpec((N_HEADS, tk, QK_PAD), lambda b, n, qi, kj, fl: (0, b * nk + kj[n], 0)),
            pl.BlockSpec((1, N_HEADS, D_V, tk), lambda b, n, qi, kj, fl: (b * nk + kj[n], 0, 0, 0)),
            pl.BlockSpec((tq, D_ATT), row_map),
        ],
        out_specs=pl.BlockSpec((tq, D_ATT), row_map),
        scratch_shapes=[
            pltpu.VMEM((N_HEADS, 1, tq), F32),
            pltpu.VMEM((N_HEADS, 1, tq), F32),
            pltpu.VMEM((N_HEADS, D_V, tq), F32),
            pltpu.VMEM((tk, tq), F32),
            pltpu.VMEM((tk, tq), F32),
            pltpu.VMEM((tk, tq), F32),
        ],
    )
    return pl.pallas_call(
        functools.partial(_attn_kernel, tq=tq, tk=tk),
        grid_spec=grid_spec,
        out_shape=jax.ShapeDtypeStruct((rows, D_ATT), BF16),
        compiler_params=_params(("arbitrary", "arbitrary")),
        name="attn",
    )(jnp.asarray(qi), jnp.asarray(kj), jnp.asarray(flag), q, k, vt, gate)


def _cache_attn_kernel(q_ref, ckv_ref, krt_ref, ckvn_ref, krn_ref, wk_ref, wv_ref, gkn_ref, g_ref, o_ref,
                       lhs_ref, qr_ref, s_ref, s2_ref, sn_ref, m_ref, l_ref, acc_ref, *, t_len, sub):
    kj = pl.program_id(1)
    n_w = N_HEADS * D_NOPE

    @pl.when(kj == 0)
    def _():
        lhs_ref[0:n_w, :] = wk_ref[...]
        for hd in range(N_HEADS):
            qn = (q_ref[hd, :, :D_NOPE].astype(F32) * gkn_ref[...]).astype(BF16)
            lhs_ref[n_w + hd * t_len:n_w + (hd + 1) * t_len, :] = jnp.dot(
                qn, wk_ref[hd * D_NOPE:(hd + 1) * D_NOPE, :], preferred_element_type=F32).astype(BF16)
            qr_ref[hd * t_len:(hd + 1) * t_len, :] = q_ref[hd, :, D_NOPE:D_QK]
        m_ref[...] = jnp.full(m_ref.shape, -jnp.inf, F32)
        l_ref[...] = jnp.zeros(l_ref.shape, F32)
        acc_ref[...] = jnp.zeros(acc_ref.shape, F32)

    def scores(ckv, s_rope, sc_ref):
        both = _dot_nt(lhs_ref[...], ckv)
        for hd in range(N_HEADS):
            kpre = both[hd * D_NOPE:(hd + 1) * D_NOPE]
            r = lax.rsqrt(jnp.sum(kpre * kpre, axis=0, keepdims=True) * (1.0 / D_NOPE) + EPS)
            rows = slice(hd * t_len, (hd + 1) * t_len)
            sc_ref[rows, :] = both[n_w + hd * t_len:n_w + (hd + 1) * t_len] * r + s_rope[rows]

    def softmax_pv(ckv, sc_ref):
        sc = sc_ref[...]
        m_prev = m_ref[...]
        m_new = jnp.maximum(m_prev, jnp.max(sc, axis=1, keepdims=True))
        alpha = jnp.exp2(m_prev - m_new)
        p = jnp.exp2(sc - m_new)
        l_ref[...] = alpha * l_ref[...] + jnp.sum(p, axis=1, keepdims=True)
        acc_ref[...] = alpha * acc_ref[...] + jnp.dot(p.astype(BF16), ckv, preferred_element_type=F32)
        m_ref[...] = m_new

    def sub_ckv(c):
        return ckv_ref[c * sub:(c + 1) * sub, :].astype(BF16)

    def sub_scores(c):
        s_rope = jnp.dot(qr_ref[...], krt_ref[0, :, c * sub:(c + 1) * sub].astype(BF16), preferred_element_type=F32)
        scores(sub_ckv(c), s_rope, (s_ref, s2_ref)[c % 2])

    n_sub = ckv_ref.shape[0] // sub
    sub_scores(0)
    for c in range(n_sub):
        if c + 1 < n_sub:
            sub_scores(c + 1)
        softmax_pv(sub_ckv(c), (s_ref, s2_ref)[c % 2])

    @pl.when(kj == pl.num_programs(1) - 1)
    def _():
        ckv_new = ckvn_ref[...].astype(BF16)
        scores(ckv_new, _dot_nt(qr_ref[...], krn_ref[...].astype(BF16)), sn_ref)
        softmax_pv(ckv_new, sn_ref)
        ctx = (acc_ref[...] * (1.0 / l_ref[...])).astype(BF16)
        for hd in range(N_HEADS):
            att = _dot_nt(ctx[hd * t_len:(hd + 1) * t_len], wv_ref[hd * D_V:(hd + 1) * D_V, :])
            o_ref[:, hd * D_V:(hd + 1) * D_V] = (att * g_ref[:, hd * D_V:(hd + 1) * D_V]).astype(BF16)


def _cache_attn_call(q, ckv_past, krt_past, ckv_new, kr_new, gate, w, batch, t_len, n_past, tk, sub):
    assert n_past % CHUNK == 0 and t_len <= CHUNK and n_past % tk == 0 and tk % sub == 0
    rows = batch * t_len
    nk = n_past // tk
    n_q = N_HEADS * t_len
    const = lambda shape: pl.BlockSpec(shape, lambda b, j: (0,) * len(shape))
    return pl.pallas_call(
        functools.partial(_cache_attn_kernel, t_len=t_len, sub=sub),
        grid=(batch, nk),
        in_specs=[
            pl.BlockSpec((N_HEADS, t_len, QK_PAD), lambda b, j: (0, b, 0)),
            pl.BlockSpec((tk, KV_LORA), lambda b, j: (b * nk + j, 0)),
            pl.BlockSpec((1, D_ROPE, tk), lambda b, j: (b, 0, j)),
            pl.BlockSpec((t_len, KV_LORA), lambda b, j: (b, 0)),
            pl.BlockSpec((t_len, D_ROPE), lambda b, j: (b, 0)),
            const((N_HEADS * D_NOPE, KV_LORA)),
            const((N_HEADS * D_V, KV_LORA)),
            const((1, D_NOPE)),
            pl.BlockSpec((t_len, D_ATT), lambda b, j: (b, 0)),
        ],
        out_specs=pl.BlockSpec((t_len, D_ATT), lambda b, j: (b, 0)),
        out_shape=jax.ShapeDtypeStruct((rows, D_ATT), BF16),
        scratch_shapes=[
            pltpu.VMEM((N_HEADS * D_NOPE + n_q, KV_LORA), BF16),
            pltpu.VMEM((n_q, D_ROPE), BF16),
            pltpu.VMEM((n_q, sub), F32),
            pltpu.VMEM((n_q, sub), F32),
            pltpu.VMEM((n_q, t_len), F32),
            pltpu.VMEM((n_q, 1), F32),
            pltpu.VMEM((n_q, 1), F32),
            pltpu.VMEM((n_q, KV_LORA), F32),
        ],
        compiler_params=_params(("arbitrary", "arbitrary")),
        name="cache_attn",
    )(q, ckv_past, krt_past, ckv_new, kr_new, w["w_uk_t"], w["w_uv"], w["g_kn"], gate)


def _out_kernel(a_ref, p_ref, w_ref, x_ref, y_ref):
    y_ref[...] = x_ref[...] + (jnp.dot(a_ref[...], w_ref[0], preferred_element_type=F32)
                               + jnp.dot(p_ref[...], w_ref[1], preferred_element_type=F32))


def _out_call(att, pool, w_out, x2, tm, tn):
    rows = x2.shape[0]
    return pl.pallas_call(
        _out_kernel,
        grid=(D_MODEL // tn, rows // tm),
        in_specs=[
            pl.BlockSpec((tm, D_ATT), lambda j, i: (i, 0)),
            pl.BlockSpec((tm, D_POOL), lambda j, i: (i, 0)),
            pl.BlockSpec((2, D_ATT, tn), lambda j, i: (0, 0, j)),
            pl.BlockSpec((tm, tn), lambda j, i: (i, j)),
        ],
        out_specs=pl.BlockSpec((tm, tn), lambda j, i: (i, j)),
        out_shape=jax.ShapeDtypeStruct((rows, D_MODEL), F32),
        compiler_params=_params(("arbitrary", "arbitrary")),
        name="out_proj",
    )(att, pool, w_out, x2)


def _rope_tables(pos0, t_len):
    half = D_ROPE // 2
    freqs = ROPE_THETA ** (-jnp.arange(half, dtype=F32) / half)
    ang = (pos0 + jnp.arange(t_len, dtype=jnp.int32)).astype(F32)[:, None] * freqs[None, :]
    cos, sin = jnp.cos(ang), jnp.sin(ang)
    zero = jnp.zeros_like(cos)
    pad = jnp.zeros((t_len, ROPE_PAD - D_ROPE), F32)
    return (jnp.concatenate([cos, cos, pad], axis=1),
            jnp.concatenate([zero, sin, pad], axis=1),
            jnp.concatenate([-sin, zero, pad], axis=1))


def _prep_weights(g_norm, w_in, g_q_lat, w_uq, g_qn, g_qr, g_kv_lat, g_kr, w_ukv, g_kn,
                  w_pool, pool_scale, w_out):
    row = lambda v: v.reshape(1, -1).astype(F32)
    pad_row = lambda v: jnp.pad(v, (0, ROPE_PAD - D_ROPE)).reshape(1, -1).astype(F32)
    return {
        "g_norm": row(g_norm), "g_q_lat": row(g_q_lat), "g_qn": row(g_qn), "g_qr": pad_row(g_qr),
        "g_kv_lat": row(g_kv_lat), "g_kr": pad_row(g_kr), "g_kn": row(g_kn), "pool_scale": row(pool_scale),
        "w_in": _prep_in_call(jnp.transpose(w_in)),
        "w_uq": _prep_uq_call(jnp.transpose(w_uq, (1, 2, 0))),
        "w_uk": w_ukv[:, :, :D_NOPE].reshape(KV_LORA, N_HEADS * D_NOPE).astype(BF16),
        "w_uk_t": jnp.transpose(w_ukv[:, :, :D_NOPE], (1, 2, 0)).reshape(N_HEADS * D_NOPE, KV_LORA).astype(BF16),
        "w_uv": jnp.transpose(w_ukv[:, :, D_NOPE:], (1, 2, 0)).reshape(N_HEADS * D_V, KV_LORA).astype(BF16),
        "w_pool": w_pool.astype(BF16),
        "w_out": w_out.reshape(2, D_ATT, D_MODEL).astype(BF16),
    }


def _mixer_layer(x, ckv_past, kr_past, pool_past, w):
    batch, t_len, _ = x.shape
    rows = batch * t_len
    pos0 = 0 if ckv_past is None else ckv_past.shape[1]
    x2 = x.reshape(rows, D_MODEL)

    h, q, ckv, kr = _latent_call(x2, _rope_tables(pos0, t_len), w, t_len, 256)
    gate = _gate_call(h, w["w_in"], 512, 1024)
    pool, pool_state = _pool_call(h, pool_past, w, batch, t_len, pos0, 512)
    tk = 512
    if ckv_past is None:
        k, vt = _kv_call(ckv, kr, w, tk)
        att = _attn_call(q, k, vt, gate, batch, t_len, 512, tk)
    else:
        att = _cache_attn_call(q, ckv_past.reshape(batch * pos0, KV_LORA), jnp.swapaxes(kr_past, 1, 2), ckv, kr,
                               gate, w, batch, t_len, pos0, 2048, 512)
    y = _out_call(att, pool, w["w_out"], x2, 512, 1024)
    return (y.reshape(batch, t_len, D_MODEL), ckv.reshape(batch, t_len, KV_LORA),
            kr.reshape(batch, t_len, D_ROPE), pool_state)


def kernel(x_prompt, x_sample, cache_ckv, cache_krope, state_pool, g_norm, w_in, g_q_lat, w_uq, g_qn, g_qr,
           g_kv_lat, g_kr, w_ukv, g_kn, w_pool, pool_scale, w_out):
    depth = w_in.shape[0]
    batch = x_prompt.shape[0]
    yp, ys = x_prompt, x_sample
    outs = [[] for _ in range(6)]
    for l in range(depth):
        w = _prep_weights(g_norm[l], w_in[l], g_q_lat[l], w_uq[l], g_qn[l], g_qr[l], g_kv_lat[l], g_kr[l],
                          w_ukv[l], g_kn[l], w_pool[l], pool_scale[l], w_out[l])
        yp, a, b, c = _mixer_layer(yp, None, None, jnp.zeros((batch, POOL_BUF, D_POOL), F32), w)
        outs[0].append(a); outs[1].append(b); outs[2].append(c)
        ys, a, b, c = _mixer_layer(ys, cache_ckv[l], cache_krope[l], state_pool[l], w)
        outs[3].append(a); outs[4].append(b); outs[5].append(c)
    return (yp, ys) + tuple(jnp.stack(o) for o in outs)
```

```python
import functools

import numpy as np
import jax
import jax.numpy as jnp
from jax import lax
from jax.experimental import pallas as pl
from jax.experimental.pallas import tpu as pltpu

D_MODEL = 4096
CHUNK = 64
N_HEADS = 16
D_NOPE = 128
D_ROPE = 64
D_QK = D_NOPE + D_ROPE
D_V = 128
D_ATT = N_HEADS * D_V
Q_LORA = 1024
KV_LORA = 512
D_POOL = 2048
POOL_WINDOWS = (2, 4, 8, 16)
N_POOL_GROUPS = len(POOL_WINDOWS)
D_POOL_GROUP = D_POOL // N_POOL_GROUPS
POOL_BUF = max(POOL_WINDOWS) - 1
ROPE_THETA = 10000.0
EPS = 1e-6

O_Z = Q_LORA + KV_LORA + D_ROPE
O_U = O_Z + D_ATT
O_ZP = O_U + D_POOL
D_IN = O_ZP + D_POOL

LANES = 128
SUBLANES = 8
ROPE_PAD = LANES
QK_PAD = D_NOPE + ROPE_PAD
LAT_PAD = Q_LORA + KV_LORA + ROPE_PAD
V_ROWS = D_V + 16
HIST = 2 * (POOL_BUF + 1)
VMEM_LIMIT = 56 * 1024 * 1024

PREP_ROWS = 64
ROW_Z = 2048
ROW_POOL = ROW_Z + D_ATT
ROWS_ALL = ROW_POOL + 2 * D_POOL

F32 = jnp.float32
BF16 = jnp.bfloat16
_NT = (((1,), (1,)), ((), ()))


def _params(semantics):
    return pltpu.CompilerParams(dimension_semantics=semantics, vmem_limit_bytes=VMEM_LIMIT)


def _rms(x, g, n=None):
    n = x.shape[-1] if n is None else n
    ms = jnp.sum(x * x, axis=-1, keepdims=True) * (1.0 / n)
    return x * lax.rsqrt(ms + EPS) * g


def _rope(x, cos, sin_hi, sin_lo):
    return x * cos + pltpu.roll(x, D_ROPE // 2, 1) * sin_hi + pltpu.roll(x, ROPE_PAD - D_ROPE // 2, 1) * sin_lo


def _silu(z):
    return z * (1.0 / (1.0 + jnp.exp(-z)))


def _dot_nt(a, b):
    return lax.dot_general(a, b, _NT, preferred_element_type=F32)


def _prep_in_kernel(src_ref, w_ref, o_ref):
    n = pl.program_id(0)

    @pl.when(src_ref[n] >= 0)
    def _():
        o_ref[...] = w_ref[...].astype(BF16)

    @pl.when(src_ref[n] < 0)
    def _():
        o_ref[...] = jnp.zeros(o_ref.shape, BF16)


def _prep_in_schedule():
    per_group = D_POOL_GROUP // PREP_ROWS
    src = []
    for n in range(ROWS_ALL // PREP_ROWS):
        row = n * PREP_ROWS
        if row < O_Z:
            src.append(n)
        elif row < ROW_Z:
            src.append(-1)
        elif row < ROW_POOL:
            src.append((O_Z + row - ROW_Z) // PREP_ROWS)
        else:
            g, t = divmod((row - ROW_POOL) // PREP_ROWS, 2 * per_group)
            base = O_U if t < per_group else O_ZP - D_POOL_GROUP
            src.append((base + g * D_POOL_GROUP) // PREP_ROWS + t)
    return np.asarray(src, np.int32)


def _prep_in_call(w_in_t):
    src = _prep_in_schedule()
    grid_spec = pltpu.PrefetchScalarGridSpec(
        num_scalar_prefetch=1,
        grid=(len(src),),
        in_specs=[pl.BlockSpec((PREP_ROWS, D_MODEL), lambda n, src: (jnp.maximum(src[n], 0), 0))],
        out_specs=pl.BlockSpec((PREP_ROWS, D_MODEL), lambda n, src: (n, 0)),
    )
    return pl.pallas_call(
        _prep_in_kernel,
        grid_spec=grid_spec,
        out_shape=jax.ShapeDtypeStruct((ROWS_ALL, D_MODEL), BF16),
        compiler_params=_params(("arbitrary",)),
        name="prep_w_in",
    )(jnp.asarray(src), w_in_t)


def _prep_uq_kernel(w_ref, o_ref):
    o_ref[0, :D_QK, :] = w_ref[0].astype(BF16)
    o_ref[0, D_QK:, :] = jnp.zeros((QK_PAD - D_QK, Q_LORA), BF16)


def _prep_uq_call(w_uq_t):
    return pl.pallas_call(
        _prep_uq_kernel,
        grid=(N_HEADS,),
        in_specs=[pl.BlockSpec((1, D_QK, Q_LORA), lambda h: (h, 0, 0))],
        out_specs=pl.BlockSpec((1, QK_PAD, Q_LORA), lambda h: (h, 0, 0)),
        out_shape=jax.ShapeDtypeStruct((N_HEADS, QK_PAD, Q_LORA), BF16),
        compiler_params=_params(("arbitrary",)),
        name="prep_w_uq",
    )(w_uq_t)


def _latent_kernel(x_ref, gn_ref, wlat_ref, gql_ref, wuq_ref, gqn_ref, gqr_ref, gkv_ref, gkr_ref,
                   cos_ref, shi_ref, slo_ref, h_ref, q_ref, ckv_ref, kr_ref, *, scale):
    h = _rms(x_ref[...], gn_ref[...]).astype(BF16)
    h_ref[...] = h
    c = _dot_nt(h, wlat_ref[...])
    qlat = _rms(c[:, :Q_LORA], gql_ref[...]).astype(BF16)
    ckv_ref[...] = _rms(c[:, Q_LORA:Q_LORA + KV_LORA], gkv_ref[...])
    cos, shi, slo = cos_ref[...], shi_ref[...], slo_ref[...]
    kr = _rope(_rms(c[:, Q_LORA + KV_LORA:], gkr_ref[...], D_ROPE), cos, shi, slo)
    kr_ref[...] = kr[:, :D_ROPE]
    for hd in range(N_HEADS):
        q = _dot_nt(qlat, wuq_ref[hd])
        qn = _rms(q[:, :D_NOPE], gqn_ref[...]) * scale
        qr = _rope(_rms(q[:, D_NOPE:], gqr_ref[...], D_ROPE), cos, shi, slo) * scale
        q_ref[hd, :, :D_NOPE] = qn.astype(BF16)
        q_ref[hd, :, D_NOPE:] = qr.astype(BF16)


def _latent_call(x2, tabs, w, t_len, tm):
    rows = x2.shape[0]
    if tm > t_len:
        tabs = [jnp.tile(t, (tm // t_len, 1)) for t in tabs]
        t_len = tm
    nt = t_len // tm
    resident = lambda shape: pl.BlockSpec(shape, lambda i: (0,) * len(shape), pipeline_mode=pl.Buffered(1))
    tab_spec = pl.BlockSpec((tm, ROPE_PAD), lambda i: (i % nt, 0))
    return pl.pallas_call(
        functools.partial(_latent_kernel, scale=float(D_QK ** -0.5 * np.log2(np.e))),
        grid=(rows // tm,),
        in_specs=[
            pl.BlockSpec((tm, D_MODEL), lambda i: (i, 0)),
            resident((1, D_MODEL)),
            resident((LAT_PAD, D_MODEL)),
            resident((1, Q_LORA)),
            resident((N_HEADS, QK_PAD, Q_LORA)),
            resident((1, D_NOPE)),
            resident((1, ROPE_PAD)),
            resident((1, KV_LORA)),
            resident((1, ROPE_PAD)),
            tab_spec, tab_spec, tab_spec,
        ],
        out_specs=[
            pl.BlockSpec((tm, D_MODEL), lambda i: (i, 0)),
            pl.BlockSpec((N_HEADS, tm, QK_PAD), lambda i: (0, i, 0)),
            pl.BlockSpec((tm, KV_LORA), lambda i: (i, 0)),
            pl.BlockSpec((tm, D_ROPE), lambda i: (i, 0)),
        ],
        out_shape=[
            jax.ShapeDtypeStruct((rows, D_MODEL), BF16),
            jax.ShapeDtypeStruct((N_HEADS, rows, QK_PAD), BF16),
            jax.ShapeDtypeStruct((rows, KV_LORA), F32),
            jax.ShapeDtypeStruct((rows, D_ROPE), F32),
        ],
        compiler_params=_params(("arbitrary",)),
        name="latent",
    )(x2, w["g_norm"], w["w_in"], w["g_q_lat"], w["w_uq"], w["g_qn"], w["g_qr"], w["g_kv_lat"], w["g_kr"],
      *tabs)


def _gate_kernel(h_ref, w_ref, o_ref):
    o_ref[...] = _silu(_dot_nt(h_ref[...], w_ref[...]))


def _gate_call(h, w_all, tm, tn):
    rows = h.shape[0]
    first = ROW_Z // tn
    return pl.pallas_call(
        _gate_kernel,
        grid=(D_ATT // tn, rows // tm),
        in_specs=[pl.BlockSpec((tm, D_MODEL), lambda j, i: (i, 0)),
                  pl.BlockSpec((tn, D_MODEL), lambda j, i: (first + j, 0))],
        out_specs=pl.BlockSpec((tm, tn), lambda j, i: (i, j)),
        out_shape=jax.ShapeDtypeStruct((rows, D_ATT), F32),
        compiler_params=_params(("arbitrary", "arbitrary")),
        name="gate",
    )(h, w_all)


def _pool_kernel(h_ref, w_ref, past_ref, wp_ref, ps_ref, o_ref, st_ref, ext_ref, b1_ref, b2_ref, p_ref,
                 *, nb, tt, pos0):
    g = pl.program_id(0)
    i = pl.program_id(2)
    u = _dot_nt(h_ref[...], w_ref[:D_POOL_GROUP, :])

    @pl.when(i == 0)
    def _():
        for b in range(nb):
            ext_ref[b, 0:HIST - POOL_BUF, :] = jnp.zeros((HIST - POOL_BUF, D_POOL_GROUP), F32)
            ext_ref[b, HIST - POOL_BUF:HIST, :] = past_ref[b]

    @pl.when(i > 0)
    def _():
        for b in range(nb):
            ext_ref[b, 0:HIST, :] = ext_ref[b, tt:tt + HIST, :]

    for b in range(nb):
        ext_ref[b, HIST:HIST + tt, :] = u[b * tt:(b + 1) * tt]
    pos = pos0 + i * tt + lax.broadcasted_iota(jnp.int32, (tt, 1), 0)

    for gi, win in enumerate(POOL_WINDOWS):
        @pl.when(g == gi)
        def _(win=win):
            steps = win.bit_length() - 1
            src = ext_ref
            for k in range(steps):
                dst = (b1_ref, b2_ref)[k % 2]
                lo, sh = SUBLANES * (k + 1), 1 << k
                for b in range(nb):
                    dst[b, lo:HIST + tt, :] = src[b, lo:HIST + tt, :] + src[b, lo - sh:HIST + tt - sh, :]
                src = dst
            cnt = jnp.minimum(pos + 1, win).astype(F32)
            for b in range(nb):
                p_ref[b * tt:(b + 1) * tt, :] = (src[b, HIST:HIST + tt, :] / cnt - u[b * tt:(b + 1) * tt]).astype(BF16)
            gate = _silu(_dot_nt(h_ref[...], w_ref[D_POOL_GROUP:, :]))
            ph = jnp.dot(p_ref[...], wp_ref[0], preferred_element_type=F32)
            o_ref[...] = (ph * ps_ref[...] * gate).astype(BF16)

    for b in range(nb):
        st_ref[b] = ext_ref[b, HIST + tt - POOL_BUF:HIST + tt, :]


def _pool_call(h, pool_past, w, batch, t_len, pos0, tm):
    rows = h.shape[0]
    tt = min(tm, t_len)
    nb = tm // tt
    nt = t_len // tt
    first = ROW_POOL // (2 * D_POOL_GROUP)
    buf = pltpu.VMEM((nb, tt + HIST, D_POOL_GROUP), F32)
    return pl.pallas_call(
        functools.partial(_pool_kernel, nb=nb, tt=tt, pos0=pos0),
        grid=(N_POOL_GROUPS, batch // nb, nt),
        in_specs=[
            pl.BlockSpec((tm, D_MODEL), lambda g, b, i: (b * nt + i, 0)),
            pl.BlockSpec((2 * D_POOL_GROUP, D_MODEL), lambda g, b, i: (first + g, 0)),
            pl.BlockSpec((nb, POOL_BUF, D_POOL_GROUP), lambda g, b, i: (b, 0, g)),
            pl.BlockSpec((1, D_POOL_GROUP, D_POOL_GROUP), lambda g, b, i: (g, 0, 0)),
            pl.BlockSpec((1, D_POOL_GROUP), lambda g, b, i: (0, g)),
        ],
        out_specs=[
            pl.BlockSpec((tm, D_POOL_GROUP), lambda g, b, i: (b * nt + i, g)),
            pl.BlockSpec((nb, POOL_BUF, D_POOL_GROUP), lambda g, b, i: (b, 0, g)),
        ],
        out_shape=[
            jax.ShapeDtypeStruct((rows, D_POOL), BF16),
            jax.ShapeDtypeStruct((batch, POOL_BUF, D_POOL), F32),
        ],
        scratch_shapes=[buf, buf, buf, pltpu.VMEM((tm, D_POOL_GROUP), BF16)],
        compiler_params=_params(("arbitrary", "arbitrary", "arbitrary")),
        name="pool",
    )(h, w["w_in"], pool_past, w["w_pool"], w["pool_scale"])


def _kv_kernel(ckv_ref, kr_ref, wk_ref, wv_ref, gkn_ref, k_ref, v_ref):
    ckv = ckv_ref[...].astype(BF16)
    kr = kr_ref[...].astype(BF16)
    tm = ckv.shape[0]
    pair = 2 * D_NOPE
    for j in range(N_HEADS // 2):
        kk = jnp.dot(ckv, wk_ref[:, j * pair:(j + 1) * pair], preferred_element_type=F32)
        for t in range(2):
            hd = 2 * j + t
            k_ref[hd, :, :D_NOPE] = _rms(kk[:, t * D_NOPE:(t + 1) * D_NOPE], gkn_ref[...]).astype(BF16)
            k_ref[hd, :, D_NOPE:D_QK] = kr
            k_ref[hd, :, D_QK:] = jnp.zeros((tm, QK_PAD - D_QK), BF16)
    vt = _dot_nt(wv_ref[...], ckv)
    ones_row = (lax.broadcasted_iota(jnp.int32, (V_ROWS - D_V, tm), 0) == 0).astype(BF16)
    for hd in range(N_HEADS):
        v_ref[0, hd, :D_V, :] = vt[hd * D_V:(hd + 1) * D_V].astype(BF16)
        v_ref[0, hd, D_V:, :] = ones_row


def _kv_call(ckv, kr, w, tm):
    rows = ckv.shape[0]
    return pl.pallas_call(
        _kv_kernel,
        grid=(rows // tm,),
        in_specs=[
            pl.BlockSpec((tm, KV_LORA), lambda i: (i, 0)),
            pl.BlockSpec((tm, D_ROPE), lambda i: (i, 0)),
            pl.BlockSpec((KV_LORA, N_HEADS * D_NOPE), lambda i: (0, 0)),
            pl.BlockSpec((N_HEADS * D_V, KV_LORA), lambda i: (0, 0)),
            pl.BlockSpec((1, D_NOPE), lambda i: (0, 0)),
        ],
        out_specs=[
            pl.BlockSpec((N_HEADS, tm, QK_PAD), lambda i: (0, i, 0)),
            pl.BlockSpec((1, N_HEADS, V_ROWS, tm), lambda i: (i, 0, 0, 0)),
        ],
        out_shape=[
            jax.ShapeDtypeStruct((N_HEADS, rows, QK_PAD), BF16),
            jax.ShapeDtypeStruct((rows // tm, N_HEADS, V_ROWS, tm), BF16),
        ],
        compiler_params=_params(("arbitrary",)),
        name="kv",
    )(ckv, kr, w["w_uk"], w["w_uv"], w["g_kn"])


def _attn_kernel(qi_ref, kj_ref, flag_ref, q_ref, k_ref, v_ref, g_ref, o_ref,
                 m_ref, acc_ref, bias_ref, sa_ref, sb_ref, sc_ref, *, tq, tk):
    n = pl.program_id(1)
    qi, kj, flag = qi_ref[n], kj_ref[n], flag_ref[n]
    masked = (flag & 1) == 1
    last = (flag & 2) == 2

    @pl.when(kj == 0)
    def _():
        m_ref[...] = jnp.full(m_ref.shape, -jnp.inf, F32)
        acc_ref[...] = jnp.zeros(acc_ref.shape, F32)

    def block(use_bias):
        bufs = (sa_ref, sb_ref, sc_ref)

        def scores(hd, s_ref):
            st = _dot_nt(k_ref[hd], q_ref[hd])
            s_ref[...] = st + bias_ref[...] if use_bias else st

        def softmax_pv(hd, s_ref):
            m_prev = m_ref[hd]
            m_new = jnp.maximum(m_prev, jnp.max(s_ref[...], axis=0, keepdims=True))
            alpha = jnp.exp2(m_prev - m_new)
            p = jnp.exp2(s_ref[...] - m_new)
            acc_ref[hd] = alpha * acc_ref[hd] + jnp.dot(v_ref[0, hd], p.astype(BF16),
                                                         preferred_element_type=F32)
            m_ref[hd] = m_new

        scores(0, bufs[0])
        scores(1, bufs[1])
        for hd in range(N_HEADS):
            if hd + 2 < N_HEADS:
                scores(hd + 2, bufs[(hd + 2) % 3])
            softmax_pv(hd, bufs[hd % 3])

    @pl.when(masked)
    def _():
        shift = CHUNK.bit_length() - 1
        kc = (kj * tk + lax.broadcasted_iota(jnp.int32, (tk, 1), 0)) >> shift
        qc = (qi * tq + lax.broadcasted_iota(jnp.int32, (1, tq), 1)) >> shift
        bias_ref[...] = jnp.where(kc <= qc, 0.0, -jnp.inf).astype(F32)
        block(True)

    @pl.when(jnp.logical_not(masked))
    def _():
        block(False)

    @pl.when(last)
    def _():
        for hd in range(N_HEADS):
            att = (acc_ref[hd, :D_V, :] * (1.0 / acc_ref[hd, D_V:D_V + 1, :])).T
            o_ref[:, hd * D_V:(hd + 1) * D_V] = (att * g_ref[:, hd * D_V:(hd + 1) * D_V]).astype(BF16)


def _attn_schedule(t_len, tq, tk):
    qi, kj, flag = [], [], []
    for i in range(t_len // tq):
        first_chunk = (i * tq) // CHUNK
        last_blk = ((i + 1) * tq - 1) // tk
        for j in range(last_blk + 1):
            needs_mask = ((j + 1) * tk - 1) // CHUNK > first_chunk
            qi.append(i)
            kj.append(j)
            flag.append(int(needs_mask) + 2 * int(j == last_blk))
    return tuple(np.asarray(a, np.int32) for a in (qi, kj, flag))


def _attn_call(q, k, vt, gate, batch, t_len, tq, tk):
    assert tq % CHUNK == 0 and t_len % tq == 0 and t_len % tk == 0
    rows = batch * t_len
    nq, nk = t_len // tq, t_len // tk
    qi, kj, flag = _attn_schedule(t_len, tq, tk)
    row_map = lambda b, n, qi, kj, fl: (b * nq + qi[n], 0)
    grid_spec = pltpu.PrefetchScalarGridSpec(
        num_scalar_prefetch=3,
        grid=(batch, len(qi)),
        in_specs=[
            pl.BlockSpec((N_HEADS, tq, QK_PAD), lambda b, n, qi, kj, fl: (0, b * nq + qi[n], 0)),
            pl.BlockSpec((N_HEADS, tk, QK_PAD), lambda b, n, qi, kj, fl: (0, b * nk + kj[n], 0)),
            pl.BlockSpec((1, N_HEADS, V_ROWS, tk), lambda b, n, qi, kj, fl: (b * nk + kj[n], 0, 0, 0)),
            pl.BlockSpec((tq, D_ATT), row_map),
        ],
        out_specs=pl.BlockSpec((tq, D_ATT), row_map),
        scratch_shapes=[
            pltpu.VMEM((N_HEADS, 1, tq), F32),
            pltpu.VMEM((N_HEADS, V_ROWS, tq), F32),
            pltpu.VMEM((tk, tq), F32),
            pltpu.VMEM((tk, tq), F32),
            pltpu.VMEM((tk, tq), F32),
            pltpu.VMEM((tk, tq), F32),
        ],
    )
    return pl.pallas_call(
        functools.partial(_attn_kernel, tq=tq, tk=tk),
        grid_spec=grid_spec,
        out_shape=jax.ShapeDtypeStruct((rows, D_ATT), BF16),
        compiler_params=_params(("arbitrary", "arbitrary")),
        name="attn",
    )(jnp.asarray(qi), jnp.asarray(kj), jnp.asarray(flag), q, k, vt, gate)


def _cache_attn_kernel(q_ref, ckv_ref, krt_ref, ckvn_ref, krn_ref, wk_ref, wv_ref, gkn_ref, g_ref, o_ref,
                       lhs_ref, qr_ref, s_ref, s2_ref, sn_ref, m_ref, l_ref, acc_ref, *, t_len, sub):
    kj = pl.program_id(1)
    n_w = N_HEADS * D_NOPE

    @pl.when(kj == 0)
    def _():
        lhs_ref[0:n_w, :] = wk_ref[...]
        for hd in range(N_HEADS):
            qn = (q_ref[hd, :, :D_NOPE].astype(F32) * gkn_ref[...]).astype(BF16)
            lhs_ref[n_w + hd * t_len:n_w + (hd + 1) * t_len, :] = jnp.dot(
                qn, wk_ref[hd * D_NOPE:(hd + 1) * D_NOPE, :], preferred_element_type=F32).astype(BF16)
            qr_ref[hd * t_len:(hd + 1) * t_len, :] = q_ref[hd, :, D_NOPE:D_QK]
        m_ref[...] = jnp.full(m_ref.shape, -jnp.inf, F32)
        l_ref[...] = jnp.zeros(l_ref.shape, F32)
        acc_ref[...] = jnp.zeros(acc_ref.shape, F32)

    def scores(ckv, s_rope, sc_ref):
        both = _dot_nt(lhs_ref[...], ckv)
        for hd in range(N_HEADS):
            kpre = both[hd * D_NOPE:(hd + 1) * D_NOPE]
            r = lax.rsqrt(jnp.sum(kpre * kpre, axis=0, keepdims=True) * (1.0 / D_NOPE) + EPS)
            rows = slice(hd * t_len, (hd + 1) * t_len)
            sc_ref[rows, :] = both[n_w + hd * t_len:n_w + (hd + 1) * t_len] * r + s_rope[rows]

    def softmax_pv(ckv, sc_ref):
        sc = sc_ref[...]
        m_prev = m_ref[...]
        m_new = jnp.maximum(m_prev, jnp.max(sc, axis=1, keepdims=True))
        alpha = jnp.exp2(m_prev - m_new)
        p = jnp.exp2(sc - m_new)
        l_ref[...] = alpha * l_ref[...] + jnp.sum(p, axis=1, keepdims=True)
        acc_ref[...] = alpha * acc_ref[...] + jnp.dot(p.astype(BF16), ckv, preferred_element_type=F32)
        m_ref[...] = m_new

    def sub_ckv(c):
        return ckv_ref[c * sub:(c + 1) * sub, :].astype(BF16)

    def sub_scores(c):
        s_rope = jnp.dot(qr_ref[...], krt_ref[0, :, c * sub:(c + 1) * sub].astype(BF16), preferred_element_type=F32)
        scores(sub_ckv(c), s_rope, (s_ref, s2_ref)[c % 2])

    n_sub = ckv_ref.shape[0] // sub
    sub_scores(0)
    for c in range(n_sub):
        if c + 1 < n_sub:
            sub_scores(c + 1)
        softmax_pv(sub_ckv(c), (s_ref, s2_ref)[c % 2])

    @pl.when(kj == pl.num_programs(1) - 1)
    def _():
        ckv_new = ckvn_ref[...].astype(BF16)
        scores(ckv_new, _dot_nt(qr_ref[...], krn_ref[...].astype(BF16)), sn_ref)
        softmax_pv(ckv_new, sn_ref)
        ctx = (acc_ref[...] * (1.0 / l_ref[...])).astype(BF16)
        for hd in range(N_HEADS):
            att = _dot_nt(ctx[hd * t_len:(hd + 1) * t_len], wv_ref[hd * D_V:(hd + 1) * D_V, :])
            o_ref[:, hd * D_V:(hd + 1) * D_V] = (att * g_ref[:, hd * D_V:(hd + 1) * D_V]).astype(BF16)


def _cache_attn_call(q, ckv_past, krt_past, ckv_new, kr_new, gate, w, batch, t_len, n_past, tk, sub):
    assert n_past % CHUNK == 0 and t_len <= CHUNK and n_past % tk == 0 and tk % sub == 0
    rows = batch * t_len
    nk = n_past // tk
    n_q = N_HEADS * t_len
    const = lambda shape: pl.BlockSpec(shape, lambda b, j: (0,) * len(shape))
    return pl.pallas_call(
        functools.partial(_cache_attn_kernel, t_len=t_len, sub=sub),
        grid=(batch, nk),
        in_specs=[
            pl.BlockSpec((N_HEADS, t_len, QK_PAD), lambda b, j: (0, b, 0)),
            pl.BlockSpec((tk, KV_LORA), lambda b, j: (b * nk + j, 0)),
            pl.BlockSpec((1, D_ROPE, tk), lambda b, j: (b, 0, j)),
            pl.BlockSpec((t_len, KV_LORA), lambda b, j: (b, 0)),
            pl.BlockSpec((t_len, D_ROPE), lambda b, j: (b, 0)),
            const((N_HEADS * D_NOPE, KV_LORA)),
            const((N_HEADS * D_V, KV_LORA)),
            const((1, D_NOPE)),
            pl.BlockSpec((t_len, D_ATT), lambda b, j: (b, 0)),
        ],
        out_specs=pl.BlockSpec((t_len, D_ATT), lambda b, j: (b, 0)),
        out_shape=jax.ShapeDtypeStruct((rows, D_ATT), BF16),
        scratch_shapes=[
            pltpu.VMEM((N_HEADS * D_NOPE + n_q, KV_LORA), BF16),
            pltpu.VMEM((n_q, D_ROPE), BF16),
            pltpu.VMEM((n_q, sub), F32),
            pltpu.VMEM((n_q, sub), F32),
            pltpu.VMEM((n_q, t_len), F32),
            pltpu.VMEM((n_q, 1), F32),
            pltpu.VMEM((n_q, 1), F32),
            pltpu.VMEM((n_q, KV_LORA), F32),
        ],
        compiler_params=_params(("arbitrary", "arbitrary")),
        name="cache_attn",
    )(q, ckv_past, krt_past, ckv_new, kr_new, w["w_uk_t"], w["w_uv"], w["g_kn"], gate)


def _out_kernel(a_ref, p_ref, w_ref, x_ref, y_ref):
    y_ref[...] = x_ref[...] + (jnp.dot(a_ref[...], w_ref[0], preferred_element_type=F32)
                               + jnp.dot(p_ref[...], w_ref[1], preferred_element_type=F32))


def _out_call(att, pool, w_out, x2, tm, tn):
    rows = x2.shape[0]
    return pl.pallas_call(
        _out_kernel,
        grid=(D_MODEL // tn, rows // tm),
        in_specs=[
            pl.BlockSpec((tm, D_ATT), lambda j, i: (i, 0)),
            pl.BlockSpec((tm, D_POOL), lambda j, i: (i, 0)),
            pl.BlockSpec((2, D_ATT, tn), lambda j, i: (0, 0, j)),
            pl.BlockSpec((tm, tn), lambda j, i: (i, j)),
        ],
        out_specs=pl.BlockSpec((tm, tn), lambda j, i: (i, j)),
        out_shape=jax.ShapeDtypeStruct((rows, D_MODEL), F32),
        compiler_params=_params(("arbitrary", "arbitrary")),
        name="out_proj",
    )(att, pool, w_out, x2)


def _rope_tables(pos0, t_len):
    half = D_ROPE // 2
    freqs = ROPE_THETA ** (-jnp.arange(half, dtype=F32) / half)
    ang = (pos0 + jnp.arange(t_len, dtype=jnp.int32)).astype(F32)[:, None] * freqs[None, :]
    cos, sin = jnp.cos(ang), jnp.sin(ang)
    zero = jnp.zeros_like(cos)
    pad = jnp.zeros((t_len, ROPE_PAD - D_ROPE), F32)
    return (jnp.concatenate([cos, cos, pad], axis=1),
            jnp.concatenate([zero, sin, pad], axis=1),
            jnp.concatenate([-sin, zero, pad], axis=1))


def _prep_weights(g_norm, w_in, g_q_lat, w_uq, g_qn, g_qr, g_kv_lat, g_kr, w_ukv, g_kn,
                  w_pool, pool_scale, w_out):
    row = lambda v: v.reshape(1, -1).astype(F32)
    pad_row = lambda v: jnp.pad(v, (0, ROPE_PAD - D_ROPE)).reshape(1, -1).astype(F32)
    return {
        "g_norm": row(g_norm), "g_q_lat": row(g_q_lat), "g_qn": row(g_qn), "g_qr": pad_row(g_qr),
        "g_kv_lat": row(g_kv_lat), "g_kr": pad_row(g_kr), "g_kn": row(g_kn), "pool_scale": row(pool_scale),
        "w_in": _prep_in_call(jnp.transpose(w_in)),
        "w_uq": _prep_uq_call(jnp.transpose(w_uq, (1, 2, 0))),
        "w_uk": w_ukv[:, :, :D_NOPE].reshape(KV_LORA, N_HEADS * D_NOPE).astype(BF16),
        "w_uk_t": jnp.transpose(w_ukv[:, :, :D_NOPE], (1, 2, 0)).reshape(N_HEADS * D_NOPE, KV_LORA).astype(BF16),
        "w_uv": jnp.transpose(w_ukv[:, :, D_NOPE:], (1, 2, 0)).reshape(N_HEADS * D_V, KV_LORA).astype(BF16),
        "w_pool": w_pool.astype(BF16),
        "w_out": w_out.reshape(2, D_ATT, D_MODEL).astype(BF16),
    }


def _mixer_layer(x, ckv_past, kr_past, pool_past, w):
    batch, t_len, _ = x.shape
    rows = batch * t_len
    pos0 = 0 if ckv_past is None else ckv_past.shape[1]
    x2 = x.reshape(rows, D_MODEL)

    h, q, ckv, kr = _latent_call(x2, _rope_tables(pos0, t_len), w, t_len, 256)
    gate = _gate_call(h, w["w_in"], 512, 1024)
    pool, pool_state = _pool_call(h, pool_past, w, batch, t_len, pos0, 512)
    tk = 512
    if ckv_past is None:
        k, vt = _kv_call(ckv, kr, w, tk)
        att = _attn_call(q, k, vt, gate, batch, t_len, 512, tk)
    else:
        att = _cache_attn_call(q, ckv_past.reshape(batch * pos0, KV_LORA), jnp.swapaxes(kr_past, 1, 2), ckv, kr,
                               gate, w, batch, t_len, pos0, 2048, 512)
    y = _out_call(att, pool, w["w_out"], x2, 512, 1024)
    return (y.reshape(batch, t_len, D_MODEL), ckv.reshape(batch, t_len, KV_LORA),
            kr.reshape(batch, t_len, D_ROPE), pool_state)


def kernel(x_prompt, x_sample, cache_ckv, cache_krope, state_pool, g_norm, w_in, g_q_lat, w_uq, g_qn, g_qr,
           g_kv_lat, g_kr, w_ukv, g_kn, w_pool, pool_scale, w_out):
    depth = w_in.shape[0]
    batch = x_prompt.shape[0]
    yp, ys = x_prompt, x_sample
    outs = [[] for _ in range(6)]
    for l in range(depth):
        w = _prep_weights(g_norm[l], w_in[l], g_q_lat[l], w_uq[l], g_qn[l], g_qr[l], g_kv_lat[l], g_kr[l],
                          w_ukv[l], g_kn[l], w_pool[l], pool_scale[l], w_out[l])
        yp, a, b, c = _mixer_layer(yp, None, None, jnp.zeros((batch, POOL_BUF, D_POOL), F32), w)
        outs[0].append(a); outs[1].append(b); outs[2].append(c)
        ys, a, b, c = _mixer_layer(ys, cache_ckv[l], cache_krope[l], state_pool[l], w)
        outs[3].append(a); outs[4].append(b); outs[5].append(c)
    return (yp, ys) + tuple(jnp.stack(o) for o in outs)
```

```python
import functools

import numpy as np
import jax
import jax.numpy as jnp
from jax import lax
from jax.experimental import pallas as pl
from jax.experimental.pallas import tpu as pltpu

D_MODEL = 4096
CHUNK = 64
N_HEADS = 16
D_NOPE = 128
D_ROPE = 64
D_QK = D_NOPE + D_ROPE
D_V = 128
D_ATT = N_HEADS * D_V
Q_LORA = 1024
KV_LORA = 512
D_POOL = 2048
POOL_WINDOWS = (2, 4, 8, 16)
N_POOL_GROUPS = len(POOL_WINDOWS)
D_POOL_GROUP = D_POOL // N_POOL_GROUPS
POOL_BUF = max(POOL_WINDOWS) - 1
ROPE_THETA = 10000.0
EPS = 1e-6

O_Z = Q_LORA + KV_LORA + D_ROPE
O_U = O_Z + D_ATT
O_ZP = O_U + D_POOL
D_IN = O_ZP + D_POOL

LANES = 128
SUBLANES = 8
ROPE_PAD = LANES
QK_PAD = D_NOPE + ROPE_PAD
LAT_PAD = Q_LORA + KV_LORA + ROPE_PAD
V_ROWS = D_V + 16
HIST = 2 * (POOL_BUF + 1)
VMEM_LIMIT = 56 * 1024 * 1024

W_IN_ALIGN = 64

F32 = jnp.float32
BF16 = jnp.bfloat16
_NT = (((1,), (1,)), ((), ()))


def _params(semantics):
    return pltpu.CompilerParams(dimension_semantics=semantics, vmem_limit_bytes=VMEM_LIMIT)


def _rms(x, g, n=None):
    n = x.shape[-1] if n is None else n
    ms = jnp.sum(x * x, axis=-1, keepdims=True) * (1.0 / n)
    return x * lax.rsqrt(ms + EPS) * g


def _rope(x, cos, sin_hi, sin_lo):
    return x * cos + pltpu.roll(x, D_ROPE // 2, 1) * sin_hi + pltpu.roll(x, ROPE_PAD - D_ROPE // 2, 1) * sin_lo


def _silu(z):
    return z * (1.0 / (1.0 + jnp.exp(-z)))


def _dot_nt(a, b):
    return lax.dot_general(a, b, _NT, preferred_element_type=F32)


def _cast_kernel(w_ref, o_ref):
    o_ref[...] = w_ref[...].astype(BF16)


def _prep_in_call(w_in_t, tr):
    return pl.pallas_call(
        _cast_kernel,
        grid=(D_IN // tr,),
        in_specs=[pl.BlockSpec((tr, D_MODEL), lambda n: (n, 0))],
        out_specs=pl.BlockSpec((tr, D_MODEL), lambda n: (n, 0)),
        out_shape=jax.ShapeDtypeStruct((D_IN, D_MODEL), BF16),
        compiler_params=_params(("arbitrary",)),
        name="prep_w_in",
    )(w_in_t)


def _w_in_window(n_rows, row_of):
    return pl.BlockSpec((pl.Element(n_rows), pl.Element(D_MODEL)),
                        lambda *idx: (pl.multiple_of(row_of(*idx), W_IN_ALIGN), 0))


def _prep_uq_kernel(w_ref, o_ref):
    o_ref[0, :D_QK, :] = w_ref[0].astype(BF16)
    o_ref[0, D_QK:, :] = jnp.zeros((QK_PAD - D_QK, Q_LORA), BF16)


def _prep_uq_call(w_uq_t):
    return pl.pallas_call(
        _prep_uq_kernel,
        grid=(N_HEADS,),
        in_specs=[pl.BlockSpec((1, D_QK, Q_LORA), lambda h: (h, 0, 0))],
        out_specs=pl.BlockSpec((1, QK_PAD, Q_LORA), lambda h: (h, 0, 0)),
        out_shape=jax.ShapeDtypeStruct((N_HEADS, QK_PAD, Q_LORA), BF16),
        compiler_params=_params(("arbitrary",)),
        name="prep_w_uq",
    )(w_uq_t)


def _latent_kernel(x_ref, gn_ref, wlat_ref, gql_ref, wuq_ref, gqn_ref, gqr_ref, gkv_ref, gkr_ref,
                   cos_ref, shi_ref, slo_ref, h_ref, q_ref, ckv_ref, kr_ref, *, scale):
    h = _rms(x_ref[...], gn_ref[...]).astype(BF16)
    h_ref[...] = h
    c = _dot_nt(h, wlat_ref[...])
    qlat = _rms(c[:, :Q_LORA], gql_ref[...]).astype(BF16)
    ckv_ref[...] = _rms(c[:, Q_LORA:Q_LORA + KV_LORA], gkv_ref[...])
    cos, shi, slo = cos_ref[...], shi_ref[...], slo_ref[...]
    c_kr = c[:, Q_LORA + KV_LORA:]
    c_kr = jnp.where(lax.broadcasted_iota(jnp.int32, c_kr.shape, 1) < D_ROPE, c_kr, 0.0)
    kr = _rope(_rms(c_kr, gkr_ref[...], D_ROPE), cos, shi, slo)
    kr_ref[...] = kr[:, :D_ROPE]
    for hd in range(N_HEADS):
        q = _dot_nt(qlat, wuq_ref[hd])
        qn = _rms(q[:, :D_NOPE], gqn_ref[...]) * scale
        qr = _rope(_rms(q[:, D_NOPE:], gqr_ref[...], D_ROPE), cos, shi, slo) * scale
        q_ref[hd, :, :D_NOPE] = qn.astype(BF16)
        q_ref[hd, :, D_NOPE:] = qr.astype(BF16)


def _latent_call(x2, tabs, w, t_len, tm):
    rows = x2.shape[0]
    if tm > t_len:
        tabs = [jnp.tile(t, (tm // t_len, 1)) for t in tabs]
        t_len = tm
    nt = t_len // tm
    resident = lambda shape: pl.BlockSpec(shape, lambda i: (0,) * len(shape), pipeline_mode=pl.Buffered(1))
    tab_spec = pl.BlockSpec((tm, ROPE_PAD), lambda i: (i % nt, 0))
    return pl.pallas_call(
        functools.partial(_latent_kernel, scale=float(D_QK ** -0.5 * np.log2(np.e))),
        grid=(rows // tm,),
        in_specs=[
            pl.BlockSpec((tm, D_MODEL), lambda i: (i, 0)),
            resident((1, D_MODEL)),
            resident((LAT_PAD, D_MODEL)),
            resident((1, Q_LORA)),
            resident((N_HEADS, QK_PAD, Q_LORA)),
            resident((1, D_NOPE)),
            resident((1, ROPE_PAD)),
            resident((1, KV_LORA)),
            resident((1, ROPE_PAD)),
            tab_spec, tab_spec, tab_spec,
        ],
        out_specs=[
            pl.BlockSpec((tm, D_MODEL), lambda i: (i, 0)),
            pl.BlockSpec((N_HEADS, tm, QK_PAD), lambda i: (0, i, 0)),
            pl.BlockSpec((tm, KV_LORA), lambda i: (i, 0)),
            pl.BlockSpec((tm, D_ROPE), lambda i: (i, 0)),
        ],
        out_shape=[
            jax.ShapeDtypeStruct((rows, D_MODEL), BF16),
            jax.ShapeDtypeStruct((N_HEADS, rows, QK_PAD), BF16),
            jax.ShapeDtypeStruct((rows, KV_LORA), F32),
            jax.ShapeDtypeStruct((rows, D_ROPE), F32),
        ],
        compiler_params=_params(("arbitrary",)),
        name="latent",
    )(x2, w["g_norm"], w["w_in"], w["g_q_lat"], w["w_uq"], w["g_qn"], w["g_qr"], w["g_kv_lat"], w["g_kr"],
      *tabs)


def _gate_kernel(h_ref, w_ref, o_ref):
    o_ref[...] = _silu(_dot_nt(h_ref[...], w_ref[...]))


def _gate_call(h, w_all, tm, tn):
    rows = h.shape[0]
    return pl.pallas_call(
        _gate_kernel,
        grid=(D_ATT // tn, rows // tm),
        in_specs=[pl.BlockSpec((tm, D_MODEL), lambda j, i: (i, 0)),
                  _w_in_window(tn, lambda j, i: O_Z + j * tn)],
        out_specs=pl.BlockSpec((tm, tn), lambda j, i: (i, j)),
        out_shape=jax.ShapeDtypeStruct((rows, D_ATT), F32),
        compiler_params=_params(("arbitrary", "arbitrary")),
        name="gate",
    )(h, w_all)


def _pool_kernel(*refs, nb, tt, pos0, lookahead):
    if lookahead:
        h_ref, hn_ref, wu_ref, wz_ref, past_ref, wp_ref, ps_ref, o_ref, st_ref, ext_ref, b1_ref, b2_ref, p_ref, un_ref = refs
    else:
        h_ref, wu_ref, wz_ref, past_ref, wp_ref, ps_ref, o_ref, st_ref, ext_ref, b1_ref, b2_ref, p_ref = refs
    g = pl.program_id(0)
    i = pl.program_id(2)

    @pl.when(i == 0)
    def _():
        u0 = _dot_nt(h_ref[...], wu_ref[...])
        for b in range(nb):
            ext_ref[b, 0:HIST - POOL_BUF, :] = jnp.zeros((HIST - POOL_BUF, D_POOL_GROUP), F32)
            ext_ref[b, HIST - POOL_BUF:HIST, :] = past_ref[b]
            ext_ref[b, HIST:HIST + tt, :] = u0[b * tt:(b + 1) * tt]

    if lookahead:
        @pl.when(i > 0)
        def _():
            for b in range(nb):
                ext_ref[b, 0:HIST, :] = ext_ref[b, tt:tt + HIST, :]
                ext_ref[b, HIST:HIST + tt, :] = un_ref[b * tt:(b + 1) * tt, :]

    pos = pos0 + i * tt + lax.broadcasted_iota(jnp.int32, (tt, 1), 0)

    for gi, win in enumerate(POOL_WINDOWS):
        @pl.when(g == gi)
        def _(win=win):
            steps = win.bit_length() - 1
            src = ext_ref
            for k in range(steps):
                dst = (b1_ref, b2_ref)[k % 2]
                lo, sh = SUBLANES * (k + 1), 1 << k
                for b in range(nb):
                    dst[b, lo:HIST + tt, :] = src[b, lo:HIST + tt, :] + src[b, lo - sh:HIST + tt - sh, :]
                src = dst
            cnt = jnp.minimum(pos + 1, win).astype(F32)
            for b in range(nb):
                p_ref[b * tt:(b + 1) * tt, :] = (src[b, HIST:HIST + tt, :] / cnt
                                                 - ext_ref[b, HIST:HIST + tt, :]).astype(BF16)
            gate = _silu(_dot_nt(h_ref[...], wz_ref[...]))
            if lookahead:
                un_ref[...] = _dot_nt(hn_ref[...], wu_ref[...])
            ph = jnp.dot(p_ref[...], wp_ref[0], preferred_element_type=F32)
            o_ref[...] = (ph * ps_ref[...] * gate).astype(BF16)

    for b in range(nb):
        st_ref[b] = ext_ref[b, HIST + tt - POOL_BUF:HIST + tt, :]


def _pool_call(h, pool_past, w, batch, t_len, pos0, tm):
    rows = h.shape[0]
    tt = min(tm, t_len)
    nb = tm // tt
    nt = t_len // tt
    buf = pltpu.VMEM((nb, tt + HIST, D_POOL_GROUP), F32)
    lookahead = nt > 1
    h_specs = [pl.BlockSpec((tm, D_MODEL), lambda g, b, i: (b * nt + i, 0))]
    scratch = [buf, buf, buf, pltpu.VMEM((tm, D_POOL_GROUP), BF16)]
    if lookahead:
        h_specs.append(pl.BlockSpec((tm, D_MODEL), lambda g, b, i: (b * nt + jnp.minimum(i + 1, nt - 1), 0)))
        scratch.append(pltpu.VMEM((tm, D_POOL_GROUP), F32))
    return pl.pallas_call(
        functools.partial(_pool_kernel, nb=nb, tt=tt, pos0=pos0, lookahead=lookahead),
        grid=(N_POOL_GROUPS, batch // nb, nt),
        in_specs=h_specs + [
            _w_in_window(D_POOL_GROUP, lambda g, b, i: O_U + g * D_POOL_GROUP),
            _w_in_window(D_POOL_GROUP, lambda g, b, i: O_ZP + g * D_POOL_GROUP),
            pl.BlockSpec((nb, POOL_BUF, D_POOL_GROUP), lambda g, b, i: (b, 0, g)),
            pl.BlockSpec((1, D_POOL_GROUP, D_POOL_GROUP), lambda g, b, i: (g, 0, 0)),
            pl.BlockSpec((1, D_POOL_GROUP), lambda g, b, i: (0, g)),
        ],
        out_specs=[
            pl.BlockSpec((tm, D_POOL_GROUP), lambda g, b, i: (b * nt + i, g)),
            pl.BlockSpec((nb, POOL_BUF, D_POOL_GROUP), lambda g, b, i: (b, 0, g)),
        ],
        out_shape=[
            jax.ShapeDtypeStruct((rows, D_POOL), BF16),
            jax.ShapeDtypeStruct((batch, POOL_BUF, D_POOL), F32),
        ],
        scratch_shapes=scratch,
        compiler_params=_params(("arbitrary", "arbitrary", "arbitrary")),
        name="pool",
    )(*([h] * len(h_specs)), w["w_in"], w["w_in"], pool_past, w["w_pool"], w["pool_scale"])


def _kv_kernel(ckv_ref, kr_ref, wk_ref, wv_ref, gkn_ref, k_ref, v_ref):
    ckv = ckv_ref[...].astype(BF16)
    kr = kr_ref[...].astype(BF16)
    tm = ckv.shape[0]
    pair = 2 * D_NOPE
    for j in range(N_HEADS // 2):
        kk = jnp.dot(ckv, wk_ref[:, j * pair:(j + 1) * pair], preferred_element_type=F32)
        for t in range(2):
            hd = 2 * j + t
            k_ref[hd, :, :D_NOPE] = _rms(kk[:, t * D_NOPE:(t + 1) * D_NOPE], gkn_ref[...]).astype(BF16)
            k_ref[hd, :, D_NOPE:D_QK] = kr
            k_ref[hd, :, D_QK:] = jnp.zeros((tm, QK_PAD - D_QK), BF16)
    vt = _dot_nt(wv_ref[...], ckv)
    ones_row = (lax.broadcasted_iota(jnp.int32, (V_ROWS - D_V, tm), 0) == 0).astype(BF16)
    for hd in range(N_HEADS):
        v_ref[0, hd, :D_V, :] = vt[hd * D_V:(hd + 1) * D_V].astype(BF16)
        v_ref[0, hd, D_V:, :] = ones_row


def _kv_call(ckv, kr, w, tm):
    rows = ckv.shape[0]
    return pl.pallas_call(
        _kv_kernel,
        grid=(rows // tm,),
        in_specs=[
            pl.BlockSpec((tm, KV_LORA), lambda i: (i, 0)),
            pl.BlockSpec((tm, D_ROPE), lambda i: (i, 0)),
            pl.BlockSpec((KV_LORA, N_HEADS * D_NOPE), lambda i: (0, 0)),
            pl.BlockSpec((N_HEADS * D_V, KV_LORA), lambda i: (0, 0)),
            pl.BlockSpec((1, D_NOPE), lambda i: (0, 0)),
        ],
        out_specs=[
            pl.BlockSpec((N_HEADS, tm, QK_PAD), lambda i: (0, i, 0)),
            pl.BlockSpec((1, N_HEADS, V_ROWS, tm), lambda i: (i, 0, 0, 0)),
        ],
        out_shape=[
            jax.ShapeDtypeStruct((N_HEADS, rows, QK_PAD), BF16),
            jax.ShapeDtypeStruct((rows // tm, N_HEADS, V_ROWS, tm), BF16),
        ],
        compiler_params=_params(("arbitrary",)),
        name="kv",
    )(ckv, kr, w["w_uk"], w["w_uv"], w["g_kn"])


def _attn_kernel(qi_ref, kj_ref, flag_ref, q_ref, k_ref, v_ref, g_ref, o_ref,
                 m_ref, acc_ref, bias_ref, sa_ref, sb_ref, sc_ref, *, tq, tk):
    n = pl.program_id(1)
    qi, kj, flag = qi_ref[n], kj_ref[n], flag_ref[n]
    masked = (flag & 1) == 1
    last = (flag & 2) == 2

    @pl.when(kj == 0)
    def _():
        m_ref[...] = jnp.full(m_ref.shape, -jnp.inf, F32)
        acc_ref[...] = jnp.zeros(acc_ref.shape, F32)

    def run_heads(scores, softmax_pv):
        bufs = (sa_ref, sb_ref, sc_ref)
        scores(0, bufs[0])
        scores(1, bufs[1])
        for hd in range(N_HEADS):
            if hd + 2 < N_HEADS:
                scores(hd + 2, bufs[(hd + 2) % 3])
            softmax_pv(hd, bufs[hd % 3])

    def full_block():
        def scores(hd, s_ref):
            s_ref[...] = _dot_nt(k_ref[hd], q_ref[hd])

        def softmax_pv(hd, s_ref):
            m_prev = m_ref[hd]
            m_new = jnp.maximum(m_prev, jnp.max(s_ref[...], axis=0, keepdims=True))
            p = jnp.exp2(s_ref[...] - m_new)
            acc_ref[hd] = jnp.exp2(m_prev - m_new) * acc_ref[hd] + jnp.dot(
                v_ref[0, hd], p.astype(BF16), preferred_element_type=F32)
            m_ref[hd] = m_new

        run_heads(scores, softmax_pv)

    def diagonal_block():
        half = tk // 2

        def scores(hd, s_ref):
            s_ref[:half, :] = _dot_nt(k_ref[hd, :half, :], q_ref[hd]) + bias_ref[:half, :]
            s_ref[half:, half:] = _dot_nt(k_ref[hd, half:, :], q_ref[hd, half:, :]) + bias_ref[half:, half:]

        def softmax_pv(hd, s_ref):
            lo, hi = slice(0, half), slice(half, tq)
            m_lo = jnp.maximum(m_ref[hd, :, lo], jnp.max(s_ref[:half, lo], axis=0, keepdims=True))
            m_hi = jnp.maximum(m_ref[hd, :, hi], jnp.maximum(jnp.max(s_ref[:half, hi], axis=0, keepdims=True),
                                                            jnp.max(s_ref[half:, hi], axis=0, keepdims=True)))
            p_lo = jnp.exp2(s_ref[:half, lo] - m_lo)
            p_hi = jnp.exp2(s_ref[:, hi] - m_hi)
            acc_ref[hd, :, lo] = jnp.exp2(m_ref[hd, :, lo] - m_lo) * acc_ref[hd, :, lo] + jnp.dot(
                v_ref[0, hd, :, :half], p_lo.astype(BF16), preferred_element_type=F32)
            acc_ref[hd, :, hi] = jnp.exp2(m_ref[hd, :, hi] - m_hi) * acc_ref[hd, :, hi] + jnp.dot(
                v_ref[0, hd], p_hi.astype(BF16), preferred_element_type=F32)
            m_ref[hd, :, lo] = m_lo
            m_ref[hd, :, hi] = m_hi

        run_heads(scores, softmax_pv)

    @pl.when(masked)
    def _():
        shift = CHUNK.bit_length() - 1
        kc = (kj * tk + lax.broadcasted_iota(jnp.int32, (tk, 1), 0)) >> shift
        qc = (qi * tq + lax.broadcasted_iota(jnp.int32, (1, tq), 1)) >> shift
        bias_ref[...] = jnp.where(kc <= qc, 0.0, -jnp.inf).astype(F32)
        diagonal_block()

    @pl.when(jnp.logical_not(masked))
    def _():
        full_block()

    @pl.when(last)
    def _():
        for hd in range(N_HEADS):
            att = (acc_ref[hd, :D_V, :] * (1.0 / acc_ref[hd, D_V:D_V + 1, :])).T
            o_ref[:, hd * D_V:(hd + 1) * D_V] = (att * g_ref[:, hd * D_V:(hd + 1) * D_V]).astype(BF16)


def _attn_schedule(t_len, tq, tk):
    qi, kj, flag = [], [], []
    for i in range(t_len // tq):
        first_chunk = (i * tq) // CHUNK
        last_blk = ((i + 1) * tq - 1) // tk
        for j in range(last_blk + 1):
            needs_mask = ((j + 1) * tk - 1) // CHUNK > first_chunk
            qi.append(i)
            kj.append(j)
            flag.append(int(needs_mask) + 2 * int(j == last_blk))
    return tuple(np.asarray(a, np.int32) for a in (qi, kj, flag))


def _attn_call(q, k, vt, gate, batch, t_len, tq, tk):
    assert tq == tk and tq % (2 * CHUNK) == 0 and tq % (2 * LANES) == 0 and t_len % tq == 0
    rows = batch * t_len
    nq, nk = t_len // tq, t_len // tk
    qi, kj, flag = _attn_schedule(t_len, tq, tk)
    row_map = lambda b, n, qi, kj, fl: (b * nq + qi[n], 0)
    grid_spec = pltpu.PrefetchScalarGridSpec(
        num_scalar_prefetch=3,
        grid=(batch, len(qi)),
        in_specs=[
            pl.BlockSpec((N_HEADS, tq, QK_PAD), lambda b, n, qi, kj, fl: (0, b * nq + qi[n], 0)),
            pl.BlockSpec((N_HEADS, tk, QK_PAD), lambda b, n, qi, kj, fl: (0, b * nk + kj[n], 0)),
            pl.BlockSpec((1, N_HEADS, V_ROWS, tk), lambda b, n, qi, kj, fl: (b * nk + kj[n], 0, 0, 0)),
            pl.BlockSpec((tq, D_ATT), row_map),
        ],
        out_specs=pl.BlockSpec((tq, D_ATT), row_map),
        scratch_shapes=[
            pltpu.VMEM((N_HEADS, 1, tq), F32),
            pltpu.VMEM((N_HEADS, V_ROWS, tq), F32),
            pltpu.VMEM((tk, tq), F32),
            pltpu.VMEM((tk, tq), F32),
            pltpu.VMEM((tk, tq), F32),
            pltpu.VMEM((tk, tq), F32),
        ],
    )
    return pl.pallas_call(
        functools.partial(_attn_kernel, tq=tq, tk=tk),
        grid_spec=grid_spec,
        out_shape=jax.ShapeDtypeStruct((rows, D_ATT), BF16),
        compiler_params=_params(("arbitrary", "arbitrary")),
        name="attn",
    )(jnp.asarray(qi), jnp.asarray(kj), jnp.asarray(flag), q, k, vt, gate)


def _cache_attn_kernel(q_ref, ckv_ref, krt_ref, ckvn_ref, krn_ref, wk_ref, wv_ref, gkn_ref, g_ref, o_ref,
                       lhs_ref, qr_ref, s_ref, s2_ref, sn_ref, m_ref, l_ref, acc_ref, *, t_len, sub):
    kj = pl.program_id(1)
    n_w = N_HEADS * D_NOPE

    @pl.when(kj == 0)
    def _():
        lhs_ref[0:n_w, :] = wk_ref[...]
        for hd in range(N_HEADS):
            qn = (q_ref[hd, :, :D_NOPE].astype(F32) * gkn_ref[...]).astype(BF16)
            lhs_ref[n_w + hd * t_len:n_w + (hd + 1) * t_len, :] = jnp.dot(
                qn, wk_ref[hd * D_NOPE:(hd + 1) * D_NOPE, :], preferred_element_type=F32).astype(BF16)
            qr_ref[hd * t_len:(hd + 1) * t_len, :] = q_ref[hd, :, D_NOPE:D_QK]
        m_ref[...] = jnp.full(m_ref.shape, -jnp.inf, F32)
        l_ref[...] = jnp.zeros(l_ref.shape, F32)
        acc_ref[...] = jnp.zeros(acc_ref.shape, F32)

    def scores(ckv, s_rope, sc_ref):
        both = _dot_nt(lhs_ref[...], ckv)
        for hd in range(N_HEADS):
            kpre = both[hd * D_NOPE:(hd + 1) * D_NOPE]
            r = lax.rsqrt(jnp.sum(kpre * kpre, axis=0, keepdims=True) * (1.0 / D_NOPE) + EPS)
            rows = slice(hd * t_len, (hd + 1) * t_len)
            sc_ref[rows, :] = both[n_w + hd * t_len:n_w + (hd + 1) * t_len] * r + s_rope[rows]

    def softmax_pv(ckv, sc_ref):
        sc = sc_ref[...]
        m_prev = m_ref[...]
        m_new = jnp.maximum(m_prev, jnp.max(sc, axis=1, keepdims=True))
        alpha = jnp.exp2(m_prev - m_new)
        p = jnp.exp2(sc - m_new)
        l_ref[...] = alpha * l_ref[...] + jnp.sum(p, axis=1, keepdims=True)
        acc_ref[...] = alpha * acc_ref[...] + jnp.dot(p.astype(BF16), ckv, preferred_element_type=F32)
        m_ref[...] = m_new

    def sub_ckv(c):
        return ckv_ref[c * sub:(c + 1) * sub, :].astype(BF16)

    def sub_scores(c):
        s_rope = jnp.dot(qr_ref[...], krt_ref[0, :, c * sub:(c + 1) * sub].astype(BF16), preferred_element_type=F32)
        scores(sub_ckv(c), s_rope, (s_ref, s2_ref)[c % 2])

    n_sub = ckv_ref.shape[0] // sub
    sub_scores(0)
    for c in range(n_sub):
        if c + 1 < n_sub:
            sub_scores(c + 1)
        softmax_pv(sub_ckv(c), (s_ref, s2_ref)[c % 2])

    @pl.when(kj == pl.num_programs(1) - 1)
    def _():
        ckv_new = ckvn_ref[...].astype(BF16)
        scores(ckv_new, _dot_nt(qr_ref[...], krn_ref[...].astype(BF16)), sn_ref)
        softmax_pv(ckv_new, sn_ref)
        ctx = (acc_ref[...] * (1.0 / l_ref[...])).astype(BF16)
        for hd in range(N_HEADS):
            att = _dot_nt(ctx[hd * t_len:(hd + 1) * t_len], wv_ref[hd * D_V:(hd + 1) * D_V, :])
            o_ref[:, hd * D_V:(hd + 1) * D_V] = (att * g_ref[:, hd * D_V:(hd + 1) * D_V]).astype(BF16)


def _cache_attn_call(q, ckv_past, krt_past, ckv_new, kr_new, gate, w, batch, t_len, n_past, tk, sub):
    assert n_past % CHUNK == 0 and t_len <= CHUNK and n_past % tk == 0 and tk % sub == 0
    rows = batch * t_len
    nk = n_past // tk
    n_q = N_HEADS * t_len
    const = lambda shape: pl.BlockSpec(shape, lambda b, j: (0,) * len(shape))
    return pl.pallas_call(
        functools.partial(_cache_attn_kernel, t_len=t_len, sub=sub),
        grid=(batch, nk),
        in_specs=[
            pl.BlockSpec((N_HEADS, t_len, QK_PAD), lambda b, j: (0, b, 0)),
            pl.BlockSpec((tk, KV_LORA), lambda b, j: (b * nk + j, 0)),
            pl.BlockSpec((1, D_ROPE, tk), lambda b, j: (b, 0, j)),
            pl.BlockSpec((t_len, KV_LORA), lambda b, j: (b, 0)),
            pl.BlockSpec((t_len, D_ROPE), lambda b, j: (b, 0)),
            const((N_HEADS * D_NOPE, KV_LORA)),
            const((N_HEADS * D_V, KV_LORA)),
            const((1, D_NOPE)),
            pl.BlockSpec((t_len, D_ATT), lambda b, j: (b, 0)),
        ],
        out_specs=pl.BlockSpec((t_len, D_ATT), lambda b, j: (b, 0)),
        out_shape=jax.ShapeDtypeStruct((rows, D_ATT), BF16),
        scratch_shapes=[
            pltpu.VMEM((N_HEADS * D_NOPE + n_q, KV_LORA), BF16),
            pltpu.VMEM((n_q, D_ROPE), BF16),
            pltpu.VMEM((n_q, sub), F32),
            pltpu.VMEM((n_q, sub), F32),
            pltpu.VMEM((n_q, t_len), F32),
            pltpu.VMEM((n_q, 1), F32),
            pltpu.VMEM((n_q, 1), F32),
            pltpu.VMEM((n_q, KV_LORA), F32),
        ],
        compiler_params=_params(("arbitrary", "arbitrary")),
        name="cache_attn",
    )(q, ckv_past, krt_past, ckv_new, kr_new, w["w_uk_t"], w["w_uv"], w["g_kn"], gate)


def _out_kernel(a_ref, p_ref, w_ref, x_ref, y_ref):
    y_ref[...] = x_ref[...] + (jnp.dot(a_ref[...], w_ref[0], preferred_element_type=F32)
                               + jnp.dot(p_ref[...], w_ref[1], preferred_element_type=F32))


def _out_call(att, pool, w_out, x2, tm, tn):
    rows = x2.shape[0]
    return pl.pallas_call(
        _out_kernel,
        grid=(D_MODEL // tn, rows // tm),
        in_specs=[
            pl.BlockSpec((tm, D_ATT), lambda j, i: (i, 0)),
            pl.BlockSpec((tm, D_POOL), lambda j, i: (i, 0)),
            pl.BlockSpec((2, D_ATT, tn), lambda j, i: (0, 0, j)),
            pl.BlockSpec((tm, tn), lambda j, i: (i, j)),
        ],
        out_specs=pl.BlockSpec((tm, tn), lambda j, i: (i, j)),
        out_shape=jax.ShapeDtypeStruct((rows, D_MODEL), F32),
        compiler_params=_params(("arbitrary", "arbitrary")),
        name="out_proj",
    )(att, pool, w_out, x2)


def _rope_tables(pos0, t_len):
    half = D_ROPE // 2
    freqs = ROPE_THETA ** (-jnp.arange(half, dtype=F32) / half)
    ang = (pos0 + jnp.arange(t_len, dtype=jnp.int32)).astype(F32)[:, None] * freqs[None, :]
    cos, sin = jnp.cos(ang), jnp.sin(ang)
    zero = jnp.zeros_like(cos)
    pad = jnp.zeros((t_len, ROPE_PAD - D_ROPE), F32)
    return (jnp.concatenate([cos, cos, pad], axis=1),
            jnp.concatenate([zero, sin, pad], axis=1),
            jnp.concatenate([-sin, zero, pad], axis=1))


def _prep_weights(g_norm, w_in, g_q_lat, w_uq, g_qn, g_qr, g_kv_lat, g_kr, w_ukv, g_kn,
                  w_pool, pool_scale, w_out):
    row = lambda v: v.reshape(1, -1).astype(F32)
    pad_row = lambda v: jnp.pad(v, (0, ROPE_PAD - D_ROPE)).reshape(1, -1).astype(F32)
    return {
        "g_norm": row(g_norm), "g_q_lat": row(g_q_lat), "g_qn": row(g_qn), "g_qr": pad_row(g_qr),
        "g_kv_lat": row(g_kv_lat), "g_kr": pad_row(g_kr), "g_kn": row(g_kn), "pool_scale": row(pool_scale),
        "w_in": _prep_in_call(jnp.transpose(w_in), D_IN // 22),
        "w_uq": _prep_uq_call(jnp.transpose(w_uq, (1, 2, 0))),
        "w_uk": w_ukv[:, :, :D_NOPE].reshape(KV_LORA, N_HEADS * D_NOPE).astype(BF16),
        "w_uk_t": jnp.transpose(w_ukv[:, :, :D_NOPE], (1, 2, 0)).reshape(N_HEADS * D_NOPE, KV_LORA).astype(BF16),
        "w_uv": jnp.transpose(w_ukv[:, :, D_NOPE:], (1, 2, 0)).reshape(N_HEADS * D_V, KV_LORA).astype(BF16),
        "w_pool": w_pool.astype(BF16),
        "w_out": w_out.reshape(2, D_ATT, D_MODEL).astype(BF16),
    }


def _mixer_layer(x, ckv_past, kr_past, pool_past, w):
    batch, t_len, _ = x.shape
    rows = batch * t_len
    pos0 = 0 if ckv_past is None else ckv_past.shape[1]
    x2 = x.reshape(rows, D_MODEL)

    h, q, ckv, kr = _latent_call(x2, _rope_tables(pos0, t_len), w, t_len, 256)
    gate = _gate_call(h, w["w_in"], 1024, 1024)
    pool, pool_state = _pool_call(h, pool_past, w, batch, t_len, pos0, 512)
    tk = 512
    if ckv_past is None:
        k, vt = _kv_call(ckv, kr, w, tk)
        att = _attn_call(q, k, vt, gate, batch, t_len, 512, tk)
    else:
        att = _cache_attn_call(q, ckv_past.reshape(batch * pos0, KV_LORA), jnp.swapaxes(kr_past, 1, 2), ckv, kr,
                               gate, w, batch, t_len, pos0, 2048, 512)
    y = _out_call(att, pool, w["w_out"], x2, 1024, 1024)
    return (y.reshape(batch, t_len, D_MODEL), ckv.reshape(batch, t_len, KV_LORA),
            kr.reshape(batch, t_len, D_ROPE), pool_state)


def kernel(x_prompt, x_sample, cache_ckv, cache_krope, state_pool, g_norm, w_in, g_q_lat, w_uq, g_qn, g_qr,
           g_kv_lat, g_kr, w_ukv, g_kn, w_pool, pool_scale, w_out):
    depth = w_in.shape[0]
    batch = x_prompt.shape[0]
    yp, ys = x_prompt, x_sample
    outs = [[] for _ in range(6)]
    for l in range(depth):
        w = _prep_weights(g_norm[l], w_in[l], g_q_lat[l], w_uq[l], g_qn[l], g_qr[l], g_kv_lat[l], g_kr[l],
                          w_ukv[l], g_kn[l], w_pool[l], pool_scale[l], w_out[l])
        yp, a, b, c = _mixer_layer(yp, None, None, jnp.zeros((batch, POOL_BUF, D_POOL), F32), w)
        outs[0].append(a); outs[1].append(b); outs[2].append(c)
        ys, a, b, c = _mixer_layer(ys, cache_ckv[l], cache_krope[l], state_pool[l], w)
        outs[3].append(a); outs[4].append(b); outs[5].append(c)
    return (yp, ys) + tuple(jnp.stack(o) for o in outs)
```

```python
import functools
from typing import NamedTuple

import numpy as np
import jax
import jax.numpy as jnp
from jax import lax
from jax.experimental import pallas as pl
from jax.experimental.pallas import tpu as pltpu

D_MODEL = 4096
CHUNK = 64
N_HEADS = 16
D_NOPE = 128
D_ROPE = 64
D_QK = D_NOPE + D_ROPE
D_V = 128
D_ATT = N_HEADS * D_V
Q_LORA = 1024
KV_LORA = 512
D_POOL = 2048
POOL_WINDOWS = (2, 4, 8, 16)
N_POOL_GROUPS = len(POOL_WINDOWS)
D_POOL_GROUP = D_POOL // N_POOL_GROUPS
POOL_BUF = max(POOL_WINDOWS) - 1
ROPE_THETA = 10000.0
EPS = 1e-6

O_Z = Q_LORA + KV_LORA + D_ROPE
O_U = O_Z + D_ATT
O_ZP = O_U + D_POOL
D_IN = O_ZP + D_POOL

LANES = 128
SUBLANES = 8
ROPE_PAD = LANES
QK_PAD = D_NOPE + ROPE_PAD
LAT_PAD = Q_LORA + KV_LORA + ROPE_PAD
V_ROWS = D_V + 16
HIST = 2 * (POOL_BUF + 1)
VMEM_LIMIT = 56 * 1024 * 1024

W_IN_ALIGN = 64

F32 = jnp.float32
BF16 = jnp.bfloat16


class _Tiles(NamedTuple):
    prep_rows: int = D_IN // 22
    latent_rows: int = 256
    gate_rows: int = 1024
    gate_cols: int = 1024
    pool_rows: int = 512
    attn_block: int = 512
    cache_keys: int = 2048
    cache_sub: int = 512
    out_rows: int = 1024
    out_cols: int = 1024


TILES = _Tiles()
_NT = (((1,), (1,)), ((), ()))


def _params(semantics):
    return pltpu.CompilerParams(dimension_semantics=semantics, vmem_limit_bytes=VMEM_LIMIT)


def _rms(x, g, n=None):
    n = x.shape[-1] if n is None else n
    ms = jnp.sum(x * x, axis=-1, keepdims=True) * (1.0 / n)
    return x * lax.rsqrt(ms + EPS) * g


def _rope(x, cos, sin_hi, sin_lo):
    return x * cos + pltpu.roll(x, D_ROPE // 2, 1) * sin_hi + pltpu.roll(x, ROPE_PAD - D_ROPE // 2, 1) * sin_lo


def _silu(z):
    return z * (1.0 / (1.0 + jnp.exp(-z)))


def _dot_nt(a, b):
    return lax.dot_general(a, b, _NT, preferred_element_type=F32)


def _cast_kernel(w_ref, o_ref):
    o_ref[...] = w_ref[...].astype(BF16)


def _prep_in_call(w_in_t, tr):
    return pl.pallas_call(
        _cast_kernel,
        grid=(D_IN // tr,),
        in_specs=[pl.BlockSpec((tr, D_MODEL), lambda n: (n, 0))],
        out_specs=pl.BlockSpec((tr, D_MODEL), lambda n: (n, 0)),
        out_shape=jax.ShapeDtypeStruct((D_IN, D_MODEL), BF16),
        compiler_params=_params(("arbitrary",)),
        name="prep_w_in",
    )(w_in_t)


def _w_in_window(n_rows, row_of):
    return pl.BlockSpec((pl.Element(n_rows), pl.Element(D_MODEL)),
                        lambda *idx: (pl.multiple_of(row_of(*idx), W_IN_ALIGN), 0))


def _prep_uq_kernel(w_ref, o_ref):
    o_ref[0, :D_QK, :] = w_ref[0].astype(BF16)
    o_ref[0, D_QK:, :] = jnp.zeros((QK_PAD - D_QK, Q_LORA), BF16)


def _prep_uq_call(w_uq_t):
    return pl.pallas_call(
        _prep_uq_kernel,
        grid=(N_HEADS,),
        in_specs=[pl.BlockSpec((1, D_QK, Q_LORA), lambda h: (h, 0, 0))],
        out_specs=pl.BlockSpec((1, QK_PAD, Q_LORA), lambda h: (h, 0, 0)),
        out_shape=jax.ShapeDtypeStruct((N_HEADS, QK_PAD, Q_LORA), BF16),
        compiler_params=_params(("arbitrary",)),
        name="prep_w_uq",
    )(w_uq_t)


def _latent_kernel(x_ref, gn_ref, wlat_ref, gql_ref, wuq_ref, gqn_ref, gqr_ref, gkv_ref, gkr_ref,
                   cos_ref, shi_ref, slo_ref, h_ref, q_ref, ckv_ref, kr_ref, *, scale):
    h = _rms(x_ref[...], gn_ref[...]).astype(BF16)
    h_ref[...] = h
    c = _dot_nt(h, wlat_ref[...])
    qlat = _rms(c[:, :Q_LORA], gql_ref[...]).astype(BF16)
    ckv_ref[...] = _rms(c[:, Q_LORA:Q_LORA + KV_LORA], gkv_ref[...])
    cos, shi, slo = cos_ref[...], shi_ref[...], slo_ref[...]
    c_kr = c[:, Q_LORA + KV_LORA:]
    c_kr = jnp.where(lax.broadcasted_iota(jnp.int32, c_kr.shape, 1) < D_ROPE, c_kr, 0.0)
    kr = _rope(_rms(c_kr, gkr_ref[...], D_ROPE), cos, shi, slo)
    kr_ref[...] = kr[:, :D_ROPE]
    for hd in range(N_HEADS):
        q = _dot_nt(qlat, wuq_ref[hd])
        qn = _rms(q[:, :D_NOPE], gqn_ref[...]) * scale
        qr = _rope(_rms(q[:, D_NOPE:], gqr_ref[...], D_ROPE), cos, shi, slo) * scale
        q_ref[hd, :, :D_NOPE] = qn.astype(BF16)
        q_ref[hd, :, D_NOPE:] = qr.astype(BF16)


def _latent_call(x2, tabs, w, t_len, tm):
    rows = x2.shape[0]
    if tm > t_len:
        tabs = [jnp.tile(t, (tm // t_len, 1)) for t in tabs]
        t_len = tm
    nt = t_len // tm
    resident = lambda shape: pl.BlockSpec(shape, lambda i: (0,) * len(shape), pipeline_mode=pl.Buffered(1))
    tab_spec = pl.BlockSpec((tm, ROPE_PAD), lambda i: (i % nt, 0))
    return pl.pallas_call(
        functools.partial(_latent_kernel, scale=float(D_QK ** -0.5 * np.log2(np.e))),
        grid=(rows // tm,),
        in_specs=[
            pl.BlockSpec((tm, D_MODEL), lambda i: (i, 0)),
            resident((1, D_MODEL)),
            resident((LAT_PAD, D_MODEL)),
            resident((1, Q_LORA)),
            resident((N_HEADS, QK_PAD, Q_LORA)),
            resident((1, D_NOPE)),
            resident((1, ROPE_PAD)),
            resident((1, KV_LORA)),
            resident((1, ROPE_PAD)),
            tab_spec, tab_spec, tab_spec,
        ],
        out_specs=[
            pl.BlockSpec((tm, D_MODEL), lambda i: (i, 0)),
            pl.BlockSpec((N_HEADS, tm, QK_PAD), lambda i: (0, i, 0)),
            pl.BlockSpec((tm, KV_LORA), lambda i: (i, 0)),
            pl.BlockSpec((tm, D_ROPE), lambda i: (i, 0)),
        ],
        out_shape=[
            jax.ShapeDtypeStruct((rows, D_MODEL), BF16),
            jax.ShapeDtypeStruct((N_HEADS, rows, QK_PAD), BF16),
            jax.ShapeDtypeStruct((rows, KV_LORA), F32),
            jax.ShapeDtypeStruct((rows, D_ROPE), F32),
        ],
        compiler_params=_params(("arbitrary",)),
        name="latent",
    )(x2, w["g_norm"], w["w_in"], w["g_q_lat"], w["w_uq"], w["g_qn"], w["g_qr"], w["g_kv_lat"], w["g_kr"],
      *tabs)


def _gate_kernel(h_ref, w_ref, o_ref):
    o_ref[...] = _silu(_dot_nt(h_ref[...], w_ref[...]))


def _gate_call(h, w_all, tm, tn):
    rows = h.shape[0]
    return pl.pallas_call(
        _gate_kernel,
        grid=(D_ATT // tn, rows // tm),
        in_specs=[pl.BlockSpec((tm, D_MODEL), lambda j, i: (i, 0)),
                  _w_in_window(tn, lambda j, i: O_Z + j * tn)],
        out_specs=pl.BlockSpec((tm, tn), lambda j, i: (i, j)),
        out_shape=jax.ShapeDtypeStruct((rows, D_ATT), F32),
        compiler_params=_params(("arbitrary", "arbitrary")),
        name="gate",
    )(h, w_all)


def _pool_kernel(*refs, nb, tt, pos0, lookahead):
    if lookahead:
        h_ref, hn_ref, wu_ref, wz_ref, past_ref, wp_ref, ps_ref, o_ref, st_ref, ext_ref, b1_ref, b2_ref, p_ref, un_ref = refs
    else:
        h_ref, wu_ref, wz_ref, past_ref, wp_ref, ps_ref, o_ref, st_ref, ext_ref, b1_ref, b2_ref, p_ref = refs
    g = pl.program_id(0)
    i = pl.program_id(2)

    @pl.when(i == 0)
    def _():
        for b in range(nb):
            ext_ref[b, 0:HIST - POOL_BUF, :] = jnp.zeros((HIST - POOL_BUF, D_POOL_GROUP), F32)
            ext_ref[b, HIST - POOL_BUF:HIST, :] = past_ref[b]

    first = jnp.logical_and(pl.program_id(1) == 0, i == 0) if lookahead else i == 0

    @pl.when(first)
    def _():
        u0 = _dot_nt(h_ref[...], wu_ref[...])
        for b in range(nb):
            ext_ref[b, HIST:HIST + tt, :] = u0[b * tt:(b + 1) * tt]

    if lookahead:
        @pl.when(i > 0)
        def _():
            for b in range(nb):
                ext_ref[b, 0:HIST, :] = ext_ref[b, tt:tt + HIST, :]

        @pl.when(jnp.logical_not(first))
        def _():
            for b in range(nb):
                ext_ref[b, HIST:HIST + tt, :] = un_ref[b * tt:(b + 1) * tt, :]

    pos = pos0 + i * tt + lax.broadcasted_iota(jnp.int32, (tt, 1), 0)

    for gi, win in enumerate(POOL_WINDOWS):
        @pl.when(g == gi)
        def _(win=win):
            steps = win.bit_length() - 1
            src = ext_ref
            for k in range(steps):
                dst = (b1_ref, b2_ref)[k % 2]
                lo, sh = SUBLANES * (k + 1), 1 << k
                for b in range(nb):
                    dst[b, lo:HIST + tt, :] = src[b, lo:HIST + tt, :] + src[b, lo - sh:HIST + tt - sh, :]
                src = dst
            cnt = jnp.minimum(pos + 1, win).astype(F32)
            for b in range(nb):
                p_ref[b * tt:(b + 1) * tt, :] = (src[b, HIST:HIST + tt, :] / cnt
                                                 - ext_ref[b, HIST:HIST + tt, :]).astype(BF16)
            gate = _silu(_dot_nt(h_ref[...], wz_ref[...]))
            if lookahead:
                un_ref[...] = _dot_nt(hn_ref[...], wu_ref[...])
            ph = jnp.dot(p_ref[...], wp_ref[0], preferred_element_type=F32)
            o_ref[...] = (ph * ps_ref[...] * gate).astype(BF16)

    for b in range(nb):
        st_ref[b] = ext_ref[b, HIST + tt - POOL_BUF:HIST + tt, :]


def _pool_call(h, pool_past, w, batch, t_len, pos0, tm):
    rows = h.shape[0]
    tt = min(tm, t_len)
    nb = tm // tt
    nt = t_len // tt
    buf = pltpu.VMEM((nb, tt + HIST, D_POOL_GROUP), F32)
    lookahead = rows // tm > 1
    h_specs = [pl.BlockSpec((tm, D_MODEL), lambda g, b, i: (b * nt + i, 0))]
    scratch = [buf, buf, buf, pltpu.VMEM((tm, D_POOL_GROUP), BF16)]
    if lookahead:
        h_specs.append(pl.BlockSpec((tm, D_MODEL), lambda g, b, i: (jnp.minimum(b * nt + i + 1, rows // tm - 1), 0)))
        scratch.append(pltpu.VMEM((tm, D_POOL_GROUP), F32))
    return pl.pallas_call(
        functools.partial(_pool_kernel, nb=nb, tt=tt, pos0=pos0, lookahead=lookahead),
        grid=(N_POOL_GROUPS, batch // nb, nt),
        in_specs=h_specs + [
            _w_in_window(D_POOL_GROUP, lambda g, b, i: O_U + g * D_POOL_GROUP),
            _w_in_window(D_POOL_GROUP, lambda g, b, i: O_ZP + g * D_POOL_GROUP),
            pl.BlockSpec((nb, POOL_BUF, D_POOL_GROUP), lambda g, b, i: (b, 0, g)),
            pl.BlockSpec((1, D_POOL_GROUP, D_POOL_GROUP), lambda g, b, i: (g, 0, 0)),
            pl.BlockSpec((1, D_POOL_GROUP), lambda g, b, i: (0, g)),
        ],
        out_specs=[
            pl.BlockSpec((tm, D_POOL_GROUP), lambda g, b, i: (b * nt + i, g)),
            pl.BlockSpec((nb, POOL_BUF, D_POOL_GROUP), lambda g, b, i: (b, 0, g)),
        ],
        out_shape=[
            jax.ShapeDtypeStruct((rows, D_POOL), BF16),
            jax.ShapeDtypeStruct((batch, POOL_BUF, D_POOL), F32),
        ],
        scratch_shapes=scratch,
        compiler_params=_params(("arbitrary", "arbitrary", "arbitrary")),
        name="pool",
    )(*([h] * len(h_specs)), w["w_in"], w["w_in"], pool_past, w["w_pool"], w["pool_scale"])


def _kv_kernel(ckv_ref, kr_ref, wk_ref, wv_ref, gkn_ref, k_ref, v_ref):
    ckv = ckv_ref[...].astype(BF16)
    kr = kr_ref[...].astype(BF16)
    tm = ckv.shape[0]
    pair = 2 * D_NOPE
    for j in range(N_HEADS // 2):
        kk = jnp.dot(ckv, wk_ref[:, j * pair:(j + 1) * pair], preferred_element_type=F32)
        for t in range(2):
            hd = 2 * j + t
            k_ref[hd, :, :D_NOPE] = _rms(kk[:, t * D_NOPE:(t + 1) * D_NOPE], gkn_ref[...]).astype(BF16)
            k_ref[hd, :, D_NOPE:D_QK] = kr
            k_ref[hd, :, D_QK:] = jnp.zeros((tm, QK_PAD - D_QK), BF16)
    vt = _dot_nt(wv_ref[...], ckv)
    ones_row = (lax.broadcasted_iota(jnp.int32, (V_ROWS - D_V, tm), 0) == 0).astype(BF16)
    for hd in range(N_HEADS):
        v_ref[0, hd, :D_V, :] = vt[hd * D_V:(hd + 1) * D_V].astype(BF16)
        v_ref[0, hd, D_V:, :] = ones_row


def _kv_call(ckv, kr, w, tm):
    rows = ckv.shape[0]
    return pl.pallas_call(
        _kv_kernel,
        grid=(rows // tm,),
        in_specs=[
            pl.BlockSpec((tm, KV_LORA), lambda i: (i, 0)),
            pl.BlockSpec((tm, D_ROPE), lambda i: (i, 0)),
            pl.BlockSpec((KV_LORA, N_HEADS * D_NOPE), lambda i: (0, 0)),
            pl.BlockSpec((N_HEADS * D_V, KV_LORA), lambda i: (0, 0)),
            pl.BlockSpec((1, D_NOPE), lambda i: (0, 0)),
        ],
        out_specs=[
            pl.BlockSpec((N_HEADS, tm, QK_PAD), lambda i: (0, i, 0)),
            pl.BlockSpec((1, N_HEADS, V_ROWS, tm), lambda i: (i, 0, 0, 0)),
        ],
        out_shape=[
            jax.ShapeDtypeStruct((N_HEADS, rows, QK_PAD), BF16),
            jax.ShapeDtypeStruct((rows // tm, N_HEADS, V_ROWS, tm), BF16),
        ],
        compiler_params=_params(("arbitrary",)),
        name="kv",
    )(ckv, kr, w["w_uk"], w["w_uv"], w["g_kn"])


def _attn_kernel(qi_ref, kj_ref, flag_ref, q_ref, k_ref, v_ref, g_ref, o_ref,
                 m_ref, acc_ref, bias_ref, sa_ref, sb_ref, sc_ref, *, tq, tk):
    n = pl.program_id(1)
    qi, kj, flag = qi_ref[n], kj_ref[n], flag_ref[n]
    masked = (flag & 1) == 1
    last = (flag & 2) == 2

    @pl.when(kj == 0)
    def _():
        m_ref[...] = jnp.full(m_ref.shape, -jnp.inf, F32)
        acc_ref[...] = jnp.zeros(acc_ref.shape, F32)

    def run_heads(scores, softmax_pv):
        bufs = (sa_ref, sb_ref, sc_ref)
        scores(0, bufs[0])
        scores(1, bufs[1])
        for hd in range(N_HEADS):
            if hd + 2 < N_HEADS:
                scores(hd + 2, bufs[(hd + 2) % 3])
            softmax_pv(hd, bufs[hd % 3])

    def full_block():
        def scores(hd, s_ref):
            s_ref[...] = _dot_nt(k_ref[hd], q_ref[hd])

        def softmax_pv(hd, s_ref):
            m_prev = m_ref[hd]
            m_new = jnp.maximum(m_prev, jnp.max(s_ref[...], axis=0, keepdims=True))
            p = jnp.exp2(s_ref[...] - m_new)
            acc_ref[hd] = jnp.exp2(m_prev - m_new) * acc_ref[hd] + jnp.dot(
                v_ref[0, hd], p.astype(BF16), preferred_element_type=F32)
            m_ref[hd] = m_new

        run_heads(scores, softmax_pv)

    def diagonal_block():
        half = tk // 2

        def scores(hd, s_ref):
            s_ref[:half, :] = _dot_nt(k_ref[hd, :half, :], q_ref[hd]) + bias_ref[:half, :]
            s_ref[half:, half:] = _dot_nt(k_ref[hd, half:, :], q_ref[hd, half:, :]) + bias_ref[half:, half:]

        def softmax_pv(hd, s_ref):
            lo, hi = slice(0, half), slice(half, tq)
            m_lo = jnp.maximum(m_ref[hd, :, lo], jnp.max(s_ref[:half, lo], axis=0, keepdims=True))
            m_hi = jnp.maximum(m_ref[hd, :, hi], jnp.maximum(jnp.max(s_ref[:half, hi], axis=0, keepdims=True),
                                                            jnp.max(s_ref[half:, hi], axis=0, keepdims=True)))
            p_lo = jnp.exp2(s_ref[:half, lo] - m_lo)
            p_hi = jnp.exp2(s_ref[:, hi] - m_hi)
            acc_ref[hd, :, lo] = jnp.exp2(m_ref[hd, :, lo] - m_lo) * acc_ref[hd, :, lo] + jnp.dot(
                v_ref[0, hd, :, :half], p_lo.astype(BF16), preferred_element_type=F32)
            acc_ref[hd, :, hi] = jnp.exp2(m_ref[hd, :, hi] - m_hi) * acc_ref[hd, :, hi] + jnp.dot(
                v_ref[0, hd], p_hi.astype(BF16), preferred_element_type=F32)
            m_ref[hd, :, lo] = m_lo
            m_ref[hd, :, hi] = m_hi

        run_heads(scores, softmax_pv)

    @pl.when(masked)
    def _():
        shift = CHUNK.bit_length() - 1
        kc = (kj * tk + lax.broadcasted_iota(jnp.int32, (tk, 1), 0)) >> shift
        qc = (qi * tq + lax.broadcasted_iota(jnp.int32, (1, tq), 1)) >> shift
        bias_ref[...] = jnp.where(kc <= qc, 0.0, -jnp.inf).astype(F32)
        diagonal_block()

    @pl.when(jnp.logical_not(masked))
    def _():
        full_block()

    @pl.when(last)
    def _():
        for hd in range(N_HEADS):
            att = (acc_ref[hd, :D_V, :] * (1.0 / acc_ref[hd, D_V:D_V + 1, :])).T
            o_ref[:, hd * D_V:(hd + 1) * D_V] = (att * g_ref[:, hd * D_V:(hd + 1) * D_V]).astype(BF16)


def _attn_schedule(t_len, tq, tk):
    qi, kj, flag = [], [], []
    for i in range(t_len // tq):
        first_chunk = (i * tq) // CHUNK
        last_blk = ((i + 1) * tq - 1) // tk
        for j in range(last_blk + 1):
            needs_mask = ((j + 1) * tk - 1) // CHUNK > first_chunk
            qi.append(i)
            kj.append(j)
            flag.append(int(needs_mask) + 2 * int(j == last_blk))
    return tuple(np.asarray(a, np.int32) for a in (qi, kj, flag))


def _attn_call(q, k, vt, gate, batch, t_len, tq, tk):
    assert tq == tk and tq % (2 * CHUNK) == 0 and tq % (2 * LANES) == 0 and t_len % tq == 0
    rows = batch * t_len
    nq, nk = t_len // tq, t_len // tk
    qi, kj, flag = _attn_schedule(t_len, tq, tk)
    row_map = lambda b, n, qi, kj, fl: (b * nq + qi[n], 0)
    grid_spec = pltpu.PrefetchScalarGridSpec(
        num_scalar_prefetch=3,
        grid=(batch, len(qi)),
        in_specs=[
            pl.BlockSpec((N_HEADS, tq, QK_PAD), lambda b, n, qi, kj, fl: (0, b * nq + qi[n], 0)),
            pl.BlockSpec((N_HEADS, tk, QK_PAD), lambda b, n, qi, kj, fl: (0, b * nk + kj[n], 0)),
            pl.BlockSpec((1, N_HEADS, V_ROWS, tk), lambda b, n, qi, kj, fl: (b * nk + kj[n], 0, 0, 0)),
            pl.BlockSpec((tq, D_ATT), row_map),
        ],
        out_specs=pl.BlockSpec((tq, D_ATT), row_map),
        scratch_shapes=[
            pltpu.VMEM((N_HEADS, 1, tq), F32),
            pltpu.VMEM((N_HEADS, V_ROWS, tq), F32),
            pltpu.VMEM((tk, tq), F32),
            pltpu.VMEM((tk, tq), F32),
            pltpu.VMEM((tk, tq), F32),
            pltpu.VMEM((tk, tq), F32),
        ],
    )
    return pl.pallas_call(
        functools.partial(_attn_kernel, tq=tq, tk=tk),
        grid_spec=grid_spec,
        out_shape=jax.ShapeDtypeStruct((rows, D_ATT), BF16),
        compiler_params=_params(("arbitrary", "arbitrary")),
        name="attn",
    )(jnp.asarray(qi), jnp.asarray(kj), jnp.asarray(flag), q, k, vt, gate)


def _cache_attn_kernel(q_ref, ckv_ref, krt_ref, ckvn_ref, krn_ref, wk_ref, wv_ref, gkn_ref, g_ref, o_ref,
                       lhs_ref, qr_ref, s_ref, s2_ref, sn_ref, m_ref, l_ref, acc_ref, *, t_len, sub):
    kj = pl.program_id(1)
    n_w = N_HEADS * D_NOPE

    @pl.when(kj == 0)
    def _():
        lhs_ref[0:n_w, :] = wk_ref[...]
        for hd in range(N_HEADS):
            qn = (q_ref[hd, :, :D_NOPE].astype(F32) * gkn_ref[...]).astype(BF16)
            lhs_ref[n_w + hd * t_len:n_w + (hd + 1) * t_len, :] = jnp.dot(
                qn, wk_ref[hd * D_NOPE:(hd + 1) * D_NOPE, :], preferred_element_type=F32).astype(BF16)
            qr_ref[hd * t_len:(hd + 1) * t_len, :] = q_ref[hd, :, D_NOPE:D_QK]
        m_ref[...] = jnp.full(m_ref.shape, -jnp.inf, F32)
        l_ref[...] = jnp.zeros(l_ref.shape, F32)
        acc_ref[...] = jnp.zeros(acc_ref.shape, F32)

    def scores(ckv, s_rope, sc_ref):
        both = _dot_nt(lhs_ref[...], ckv)
        for hd in range(N_HEADS):
            kpre = both[hd * D_NOPE:(hd + 1) * D_NOPE]
            r = lax.rsqrt(jnp.sum(kpre * kpre, axis=0, keepdims=True) * (1.0 / D_NOPE) + EPS)
            rows = slice(hd * t_len, (hd + 1) * t_len)
            sc_ref[rows, :] = both[n_w + hd * t_len:n_w + (hd + 1) * t_len] * r + s_rope[rows]

    def softmax_pv(ckv, sc_ref):
        sc = sc_ref[...]
        m_prev = m_ref[...]
        m_new = jnp.maximum(m_prev, jnp.max(sc, axis=1, keepdims=True))
        alpha = jnp.exp2(m_prev - m_new)
        p = jnp.exp2(sc - m_new)
        l_ref[...] = alpha * l_ref[...] + jnp.sum(p, axis=1, keepdims=True)
        acc_ref[...] = alpha * acc_ref[...] + jnp.dot(p.astype(BF16), ckv, preferred_element_type=F32)
        m_ref[...] = m_new

    def sub_ckv(c):
        return ckv_ref[c * sub:(c + 1) * sub, :].astype(BF16)

    def sub_scores(c):
        s_rope = jnp.dot(qr_ref[...], krt_ref[0, :, c * sub:(c + 1) * sub].astype(BF16), preferred_element_type=F32)
        scores(sub_ckv(c), s_rope, (s_ref, s2_ref)[c % 2])

    n_sub = ckv_ref.shape[0] // sub
    sub_scores(0)
    for c in range(n_sub):
        if c + 1 < n_sub:
            sub_scores(c + 1)
        softmax_pv(sub_ckv(c), (s_ref, s2_ref)[c % 2])

    @pl.when(kj == pl.num_programs(1) - 1)
    def _():
        ckv_new = ckvn_ref[...].astype(BF16)
        scores(ckv_new, _dot_nt(qr_ref[...], krn_ref[...].astype(BF16)), sn_ref)
        softmax_pv(ckv_new, sn_ref)
        ctx = (acc_ref[...] * (1.0 / l_ref[...])).astype(BF16)
        for hd in range(N_HEADS):
            att = _dot_nt(ctx[hd * t_len:(hd + 1) * t_len], wv_ref[hd * D_V:(hd + 1) * D_V, :])
            o_ref[:, hd * D_V:(hd + 1) * D_V] = (att * g_ref[:, hd * D_V:(hd + 1) * D_V]).astype(BF16)


def _cache_attn_call(q, ckv_past, krt_past, ckv_new, kr_new, gate, w, batch, t_len, n_past, tk, sub):
    assert n_past % CHUNK == 0 and t_len <= CHUNK and n_past % tk == 0 and tk % sub == 0
    rows = batch * t_len
    nk = n_past // tk
    n_q = N_HEADS * t_len
    const = lambda shape: pl.BlockSpec(shape, lambda b, j: (0,) * len(shape))
    return pl.pallas_call(
        functools.partial(_cache_attn_kernel, t_len=t_len, sub=sub),
        grid=(batch, nk),
        in_specs=[
            pl.BlockSpec((N_HEADS, t_len, QK_PAD), lambda b, j: (0, b, 0)),
            pl.BlockSpec((tk, KV_LORA), lambda b, j: (b * nk + j, 0)),
            pl.BlockSpec((1, D_ROPE, tk), lambda b, j: (b, 0, j)),
            pl.BlockSpec((t_len, KV_LORA), lambda b, j: (b, 0)),
            pl.BlockSpec((t_len, D_ROPE), lambda b, j: (b, 0)),
            const((N_HEADS * D_NOPE, KV_LORA)),
            const((N_HEADS * D_V, KV_LORA)),
            const((1, D_NOPE)),
            pl.BlockSpec((t_len, D_ATT), lambda b, j: (b, 0)),
        ],
        out_specs=pl.BlockSpec((t_len, D_ATT), lambda b, j: (b, 0)),
        out_shape=jax.ShapeDtypeStruct((rows, D_ATT), BF16),
        scratch_shapes=[
            pltpu.VMEM((N_HEADS * D_NOPE + n_q, KV_LORA), BF16),
            pltpu.VMEM((n_q, D_ROPE), BF16),
            pltpu.VMEM((n_q, sub), F32),
            pltpu.VMEM((n_q, sub), F32),
            pltpu.VMEM((n_q, t_len), F32),
            pltpu.VMEM((n_q, 1), F32),
            pltpu.VMEM((n_q, 1), F32),
            pltpu.VMEM((n_q, KV_LORA), F32),
        ],
        compiler_params=_params(("arbitrary", "arbitrary")),
        name="cache_attn",
    )(q, ckv_past, krt_past, ckv_new, kr_new, w["w_uk_t"], w["w_uv"], w["g_kn"], gate)


def _out_kernel(a_ref, p_ref, w_ref, x_ref, y_ref):
    y_ref[...] = x_ref[...] + (jnp.dot(a_ref[...], w_ref[0], preferred_element_type=F32)
                               + jnp.dot(p_ref[...], w_ref[1], preferred_element_type=F32))


def _out_call(att, pool, w_out, x2, tm, tn):
    rows = x2.shape[0]
    return pl.pallas_call(
        _out_kernel,
        grid=(D_MODEL // tn, rows // tm),
        in_specs=[
            pl.BlockSpec((tm, D_ATT), lambda j, i: (i, 0)),
            pl.BlockSpec((tm, D_POOL), lambda j, i: (i, 0)),
            pl.BlockSpec((2, D_ATT, tn), lambda j, i: (0, 0, j)),
            pl.BlockSpec((tm, tn), lambda j, i: (i, j)),
        ],
        out_specs=pl.BlockSpec((tm, tn), lambda j, i: (i, j)),
        out_shape=jax.ShapeDtypeStruct((rows, D_MODEL), F32),
        compiler_params=_params(("arbitrary", "arbitrary")),
        name="out_proj",
    )(att, pool, w_out, x2)


def _rope_tables(pos0, t_len):
    half = D_ROPE // 2
    freqs = ROPE_THETA ** (-jnp.arange(half, dtype=F32) / half)
    ang = (pos0 + jnp.arange(t_len, dtype=jnp.int32)).astype(F32)[:, None] * freqs[None, :]
    cos, sin = jnp.cos(ang), jnp.sin(ang)
    zero = jnp.zeros_like(cos)
    pad = jnp.zeros((t_len, ROPE_PAD - D_ROPE), F32)
    return (jnp.concatenate([cos, cos, pad], axis=1),
            jnp.concatenate([zero, sin, pad], axis=1),
            jnp.concatenate([-sin, zero, pad], axis=1))


def _prep_weights(g_norm, w_in, g_q_lat, w_uq, g_qn, g_qr, g_kv_lat, g_kr, w_ukv, g_kn,
                  w_pool, pool_scale, w_out):
    row = lambda v: v.reshape(1, -1).astype(F32)
    pad_row = lambda v: jnp.pad(v, (0, ROPE_PAD - D_ROPE)).reshape(1, -1).astype(F32)
    return {
        "g_norm": row(g_norm), "g_q_lat": row(g_q_lat), "g_qn": row(g_qn), "g_qr": pad_row(g_qr),
        "g_kv_lat": row(g_kv_lat), "g_kr": pad_row(g_kr), "g_kn": row(g_kn), "pool_scale": row(pool_scale),
        "w_in": _prep_in_call(jnp.transpose(w_in), TILES.prep_rows),
        "w_uq": _prep_uq_call(jnp.transpose(w_uq, (1, 2, 0))),
        "w_uk": w_ukv[:, :, :D_NOPE].reshape(KV_LORA, N_HEADS * D_NOPE).astype(BF16),
        "w_uk_t": jnp.transpose(w_ukv[:, :, :D_NOPE], (1, 2, 0)).reshape(N_HEADS * D_NOPE, KV_LORA).astype(BF16),
        "w_uv": jnp.transpose(w_ukv[:, :, D_NOPE:], (1, 2, 0)).reshape(N_HEADS * D_V, KV_LORA).astype(BF16),
        "w_pool": w_pool.astype(BF16),
        "w_out": w_out.reshape(2, D_ATT, D_MODEL).astype(BF16),
    }


def _mixer_layer(x, ckv_past, kr_past, pool_past, w):
    batch, t_len, _ = x.shape
    rows = batch * t_len
    pos0 = 0 if ckv_past is None else ckv_past.shape[1]
    x2 = x.reshape(rows, D_MODEL)

    h, q, ckv, kr = _latent_call(x2, _rope_tables(pos0, t_len), w, t_len, TILES.latent_rows)
    gate = _gate_call(h, w["w_in"], TILES.gate_rows, TILES.gate_cols)
    pool, pool_state = _pool_call(h, pool_past, w, batch, t_len, pos0, TILES.pool_rows)
    if ckv_past is None:
        k, vt = _kv_call(ckv, kr, w, TILES.attn_block)
        att = _attn_call(q, k, vt, gate, batch, t_len, TILES.attn_block, TILES.attn_block)
    else:
        att = _cache_attn_call(q, ckv_past.reshape(batch * pos0, KV_LORA), jnp.swapaxes(kr_past, 1, 2), ckv, kr,
                               gate, w, batch, t_len, pos0, TILES.cache_keys, TILES.cache_sub)
    y = _out_call(att, pool, w["w_out"], x2, TILES.out_rows, TILES.out_cols)
    return (y.reshape(batch, t_len, D_MODEL), ckv.reshape(batch, t_len, KV_LORA),
            kr.reshape(batch, t_len, D_ROPE), pool_state)


def kernel(x_prompt, x_sample, cache_ckv, cache_krope, state_pool, g_norm, w_in, g_q_lat, w_uq, g_qn, g_qr,
           g_kv_lat, g_kr, w_ukv, g_kn, w_pool, pool_scale, w_out):
    depth = w_in.shape[0]
    batch = x_prompt.shape[0]
    yp, ys = x_prompt, x_sample
    outs = [[] for _ in range(6)]
    for l in range(depth):
        w = _prep_weights(g_norm[l], w_in[l], g_q_lat[l], w_uq[l], g_qn[l], g_qr[l], g_kv_lat[l], g_kr[l],
                          w_ukv[l], g_kn[l], w_pool[l], pool_scale[l], w_out[l])
        yp, a, b, c = _mixer_layer(yp, None, None, jnp.zeros((batch, POOL_BUF, D_POOL), F32), w)
        outs[0].append(a); outs[1].append(b); outs[2].append(c)
        ys, a, b, c = _mixer_layer(ys, cache_ckv[l], cache_krope[l], state_pool[l], w)
        outs[3].append(a); outs[4].append(b); outs[5].append(c)
    return (yp, ys) + tuple(jnp.stack(o) for o in outs)
```

```python
import functools
from typing import NamedTuple

import numpy as np
import jax
import jax.numpy as jnp
from jax import lax
from jax.experimental import pallas as pl
from jax.experimental.pallas import tpu as pltpu

D_MODEL = 4096
CHUNK = 64
N_HEADS = 16
D_NOPE = 128
D_ROPE = 64
D_QK = D_NOPE + D_ROPE
D_V = 128
D_ATT = N_HEADS * D_V
Q_LORA = 1024
KV_LORA = 512
D_POOL = 2048
POOL_WINDOWS = (2, 4, 8, 16)
N_POOL_GROUPS = len(POOL_WINDOWS)
D_POOL_GROUP = D_POOL // N_POOL_GROUPS
POOL_BUF = max(POOL_WINDOWS) - 1
ROPE_THETA = 10000.0
EPS = 1e-6

O_Z = Q_LORA + KV_LORA + D_ROPE
O_U = O_Z + D_ATT
O_ZP = O_U + D_POOL
D_IN = O_ZP + D_POOL

LANES = 128
SUBLANES = 8
ROPE_PAD = LANES
QK_PAD = D_NOPE + ROPE_PAD
LAT_PAD = Q_LORA + KV_LORA + ROPE_PAD
HEAD_GROUP = 4
V_ROWS = D_V + 16
HIST = 2 * (POOL_BUF + 1)
VMEM_LIMIT = 56 * 1024 * 1024

W_IN_ALIGN = 64

F32 = jnp.float32
BF16 = jnp.bfloat16


class _Tiles(NamedTuple):
    prep_rows: int = D_IN // 22
    latent_rows: int = 256
    gate_rows: int = 1024
    gate_cols: int = 1024
    pool_rows: int = 512
    attn_block: int = 512
    cache_keys: int = 2048
    cache_sub: int = 512
    out_rows: int = 1024
    out_cols: int = 1024


TILES = _Tiles()
_NT = (((1,), (1,)), ((), ()))


def _params(semantics):
    return pltpu.CompilerParams(dimension_semantics=semantics, vmem_limit_bytes=VMEM_LIMIT)


def _rms(x, g, n=None):
    n = x.shape[-1] if n is None else n
    ms = jnp.sum(x * x, axis=-1, keepdims=True) * (1.0 / n)
    return x * lax.rsqrt(ms + EPS) * g


def _rope(x, cos, sin_hi, sin_lo):
    return x * cos + pltpu.roll(x, D_ROPE // 2, 1) * sin_hi + pltpu.roll(x, ROPE_PAD - D_ROPE // 2, 1) * sin_lo


def _silu(z):
    return z * (1.0 / (1.0 + jnp.exp(-z)))


def _dot_nt(a, b):
    return lax.dot_general(a, b, _NT, preferred_element_type=F32)


def _cast_kernel(w_ref, o_ref):
    o_ref[...] = w_ref[...].astype(BF16)


def _prep_in_call(w_in_t, tr):
    return pl.pallas_call(
        _cast_kernel,
        grid=(D_IN // tr,),
        in_specs=[pl.BlockSpec((tr, D_MODEL), lambda n: (n, 0))],
        out_specs=pl.BlockSpec((tr, D_MODEL), lambda n: (n, 0)),
        out_shape=jax.ShapeDtypeStruct((D_IN, D_MODEL), BF16),
        compiler_params=_params(("arbitrary",)),
        name="prep_w_in",
    )(w_in_t)


def _w_in_window(n_rows, row_of):
    return pl.BlockSpec((pl.Element(n_rows), pl.Element(D_MODEL)),
                        lambda *idx: (pl.multiple_of(row_of(*idx), W_IN_ALIGN), 0))


def _prep_uq_kernel(w_ref, o_ref):
    rope0 = HEAD_GROUP * D_NOPE
    for j in range(HEAD_GROUP):
        o_ref[0, j * D_NOPE:(j + 1) * D_NOPE, :] = w_ref[j, :D_NOPE, :].astype(BF16)
        o_ref[0, rope0 + j * D_ROPE:rope0 + (j + 1) * D_ROPE, :] = w_ref[j, D_NOPE:, :].astype(BF16)


def _prep_uq_call(w_uq_t):
    groups = N_HEADS // HEAD_GROUP
    return pl.pallas_call(
        _prep_uq_kernel,
        grid=(groups,),
        in_specs=[pl.BlockSpec((HEAD_GROUP, D_QK, Q_LORA), lambda g: (g, 0, 0))],
        out_specs=pl.BlockSpec((1, HEAD_GROUP * D_QK, Q_LORA), lambda g: (g, 0, 0)),
        out_shape=jax.ShapeDtypeStruct((groups, HEAD_GROUP * D_QK, Q_LORA), BF16),
        compiler_params=_params(("arbitrary",)),
        name="prep_w_uq",
    )(w_uq_t)


def _latent_kernel(x_ref, gn_ref, wlat_ref, gql_ref, wuq_ref, gqn_ref, gqr_ref, gkv_ref, gkr_ref,
                   cos_ref, shi_ref, slo_ref, h_ref, q_ref, ckv_ref, kr_ref, *, scale):
    h = _rms(x_ref[...], gn_ref[...]).astype(BF16)
    h_ref[...] = h
    c = _dot_nt(h, wlat_ref[...])
    qlat = _rms(c[:, :Q_LORA], gql_ref[...]).astype(BF16)
    ckv_ref[...] = _rms(c[:, Q_LORA:Q_LORA + KV_LORA], gkv_ref[...])
    cos, shi, slo = cos_ref[...], shi_ref[...], slo_ref[...]
    c_kr = c[:, Q_LORA + KV_LORA:]
    c_kr = jnp.where(lax.broadcasted_iota(jnp.int32, c_kr.shape, 1) < D_ROPE, c_kr, 0.0)
    kr = _rope(_rms(c_kr, gkr_ref[...], D_ROPE), cos, shi, slo)
    kr_ref[...] = kr[:, :D_ROPE]
    low = lax.broadcasted_iota(jnp.int32, c_kr.shape, 1) < D_ROPE
    rope0 = HEAD_GROUP * D_NOPE
    for g in range(N_HEADS // HEAD_GROUP):
        q = _dot_nt(qlat, wuq_ref[g])
        for j in range(HEAD_GROUP):
            qn = _rms(q[:, j * D_NOPE:(j + 1) * D_NOPE], gqn_ref[...]) * scale
            q_ref[g * HEAD_GROUP + j, :, :D_NOPE] = qn.astype(BF16)
        for t in range(HEAD_GROUP // 2):
            x = q[:, rope0 + t * ROPE_PAD:rope0 + (t + 1) * ROPE_PAD]
            sq = x * x
            ms_lo = jnp.sum(jnp.where(low, sq, 0.0), axis=-1, keepdims=True) * (1.0 / D_ROPE)
            ms_hi = jnp.sum(jnp.where(low, 0.0, sq), axis=-1, keepdims=True) * (1.0 / D_ROPE)
            r = jnp.where(low, lax.rsqrt(ms_lo + EPS), lax.rsqrt(ms_hi + EPS))
            y = _rope(x * r * gqr_ref[...], cos, shi, slo) * scale
            hd = g * HEAD_GROUP + 2 * t
            q_ref[hd, :, D_NOPE:] = jnp.where(low, y, 0.0).astype(BF16)
            q_ref[hd + 1, :, D_NOPE:] = jnp.where(low, pltpu.roll(y, D_ROPE, 1), 0.0).astype(BF16)


def _latent_call(x2, tabs, w, t_len, tm):
    rows = x2.shape[0]
    if tm > t_len:
        tabs = [jnp.tile(t, (tm // t_len, 1)) for t in tabs]
        t_len = tm
    nt = t_len // tm
    resident = lambda shape: pl.BlockSpec(shape, lambda i: (0,) * len(shape), pipeline_mode=pl.Buffered(1))
    tab_spec = pl.BlockSpec((tm, ROPE_PAD), lambda i: (i % nt, 0))
    return pl.pallas_call(
        functools.partial(_latent_kernel, scale=float(D_QK ** -0.5 * np.log2(np.e))),
        grid=(rows // tm,),
        in_specs=[
            pl.BlockSpec((tm, D_MODEL), lambda i: (i, 0)),
            resident((1, D_MODEL)),
            resident((LAT_PAD, D_MODEL)),
            resident((1, Q_LORA)),
            resident((N_HEADS // HEAD_GROUP, HEAD_GROUP * D_QK, Q_LORA)),
            resident((1, D_NOPE)),
            resident((1, ROPE_PAD)),
            resident((1, KV_LORA)),
            resident((1, ROPE_PAD)),
            tab_spec, tab_spec, tab_spec,
        ],
        out_specs=[
            pl.BlockSpec((tm, D_MODEL), lambda i: (i, 0)),
            pl.BlockSpec((N_HEADS, tm, QK_PAD), lambda i: (0, i, 0)),
            pl.BlockSpec((tm, KV_LORA), lambda i: (i, 0)),
            pl.BlockSpec((tm, D_ROPE), lambda i: (i, 0)),
        ],
        out_shape=[
            jax.ShapeDtypeStruct((rows, D_MODEL), BF16),
            jax.ShapeDtypeStruct((N_HEADS, rows, QK_PAD), BF16),
            jax.ShapeDtypeStruct((rows, KV_LORA), F32),
            jax.ShapeDtypeStruct((rows, D_ROPE), F32),
        ],
        compiler_params=_params(("arbitrary",)),
        name="latent",
    )(x2, w["g_norm"], w["w_in"], w["g_q_lat"], w["w_uq"], w["g_qn"], w["g_qr"], w["g_kv_lat"], w["g_kr"],
      *tabs)


def _gate_kernel(h_ref, w_ref, o_ref):
    o_ref[...] = _silu(_dot_nt(h_ref[...], w_ref[...]))


def _gate_call(h, w_all, tm, tn):
    rows = h.shape[0]
    return pl.pallas_call(
        _gate_kernel,
        grid=(D_ATT // tn, rows // tm),
        in_specs=[pl.BlockSpec((tm, D_MODEL), lambda j, i: (i, 0)),
                  _w_in_window(tn, lambda j, i: O_Z + j * tn)],
        out_specs=pl.BlockSpec((tm, tn), lambda j, i: (i, j)),
        out_shape=jax.ShapeDtypeStruct((rows, D_ATT), F32),
        compiler_params=_params(("arbitrary", "arbitrary")),
        name="gate",
    )(h, w_all)


def _pool_kernel(*refs, nb, tt, pos0, lookahead):
    if lookahead:
        h_ref, hn_ref, wu_ref, wz_ref, past_ref, wp_ref, ps_ref, o_ref, st_ref, ext_ref, b1_ref, b2_ref, p_ref, un_ref = refs
    else:
        h_ref, wu_ref, wz_ref, past_ref, wp_ref, ps_ref, o_ref, st_ref, ext_ref, b1_ref, b2_ref, p_ref = refs
    g = pl.program_id(0)
    i = pl.program_id(2)

    @pl.when(i == 0)
    def _():
        for b in range(nb):
            ext_ref[b, 0:HIST - POOL_BUF, :] = jnp.zeros((HIST - POOL_BUF, D_POOL_GROUP), F32)
            ext_ref[b, HIST - POOL_BUF:HIST, :] = past_ref[b]

    first = jnp.logical_and(pl.program_id(1) == 0, i == 0) if lookahead else i == 0

    @pl.when(first)
    def _():
        u0 = _dot_nt(h_ref[...], wu_ref[...])
        for b in range(nb):
            ext_ref[b, HIST:HIST + tt, :] = u0[b * tt:(b + 1) * tt]

    if lookahead:
        @pl.when(i > 0)
        def _():
            for b in range(nb):
                ext_ref[b, 0:HIST, :] = ext_ref[b, tt:tt + HIST, :]

        @pl.when(jnp.logical_not(first))
        def _():
            for b in range(nb):
                ext_ref[b, HIST:HIST + tt, :] = un_ref[b * tt:(b + 1) * tt, :]

    pos = pos0 + i * tt + lax.broadcasted_iota(jnp.int32, (tt, 1), 0)

    for gi, win in enumerate(POOL_WINDOWS):
        @pl.when(g == gi)
        def _(win=win):
            steps = win.bit_length() - 1
            src = ext_ref
            for k in range(steps):
                dst = (b1_ref, b2_ref)[k % 2]
                lo, sh = SUBLANES * (k + 1), 1 << k
                for b in range(nb):
                    dst[b, lo:HIST + tt, :] = src[b, lo:HIST + tt, :] + src[b, lo - sh:HIST + tt - sh, :]
                src = dst
            cnt = jnp.minimum(pos + 1, win).astype(F32)
            for b in range(nb):
                p_ref[b * tt:(b + 1) * tt, :] = (src[b, HIST:HIST + tt, :] / cnt
                                                 - ext_ref[b, HIST:HIST + tt, :]).astype(BF16)
            gate = _silu(_dot_nt(h_ref[...], wz_ref[...]))
            if lookahead:
                un_ref[...] = _dot_nt(hn_ref[...], wu_ref[...])
            ph = jnp.dot(p_ref[...], wp_ref[0], preferred_element_type=F32)
            o_ref[...] = (ph * ps_ref[...] * gate).astype(BF16)

    for b in range(nb):
        st_ref[b] = ext_ref[b, HIST + tt - POOL_BUF:HIST + tt, :]


def _pool_call(h, pool_past, w, batch, t_len, pos0, tm):
    rows = h.shape[0]
    tt = min(tm, t_len)
    nb = tm // tt
    nt = t_len // tt
    buf = pltpu.VMEM((nb, tt + HIST, D_POOL_GROUP), F32)
    lookahead = nt > 1
    h_specs = [pl.BlockSpec((tm, D_MODEL), lambda g, b, i: (b * nt + i, 0))]
    scratch = [buf, buf, buf, pltpu.VMEM((tm, D_POOL_GROUP), BF16)]
    if lookahead:
        h_specs.append(pl.BlockSpec((tm, D_MODEL), lambda g, b, i: (jnp.minimum(b * nt + i + 1, rows // tm - 1), 0)))
        scratch.append(pltpu.VMEM((tm, D_POOL_GROUP), F32))
    return pl.pallas_call(
        functools.partial(_pool_kernel, nb=nb, tt=tt, pos0=pos0, lookahead=lookahead),
        grid=(N_POOL_GROUPS, batch // nb, nt),
        in_specs=h_specs + [
            _w_in_window(D_POOL_GROUP, lambda g, b, i: O_U + g * D_POOL_GROUP),
            _w_in_window(D_POOL_GROUP, lambda g, b, i: O_ZP + g * D_POOL_GROUP),
            pl.BlockSpec((nb, POOL_BUF, D_POOL_GROUP), lambda g, b, i: (b, 0, g)),
            pl.BlockSpec((1, D_POOL_GROUP, D_POOL_GROUP), lambda g, b, i: (g, 0, 0)),
            pl.BlockSpec((1, D_POOL_GROUP), lambda g, b, i: (0, g)),
        ],
        out_specs=[
            pl.BlockSpec((tm, D_POOL_GROUP), lambda g, b, i: (b * nt + i, g)),
            pl.BlockSpec((nb, POOL_BUF, D_POOL_GROUP), lambda g, b, i: (b, 0, g)),
        ],
        out_shape=[
            jax.ShapeDtypeStruct((rows, D_POOL), BF16),
            jax.ShapeDtypeStruct((batch, POOL_BUF, D_POOL), F32),
        ],
        scratch_shapes=scratch,
        compiler_params=_params(("arbitrary", "arbitrary", "arbitrary")),
        name="pool",
    )(*([h] * len(h_specs)), w["w_in"], w["w_in"], pool_past, w["w_pool"], w["pool_scale"])


def _kv_kernel(ckv_ref, kr_ref, wk_ref, wv_ref, gkn_ref, k_ref, v_ref):
    ckv = ckv_ref[...].astype(BF16)
    kr = kr_ref[...].astype(BF16)
    tm = ckv.shape[0]
    pair = 2 * D_NOPE
    for j in range(N_HEADS // 2):
        kk = jnp.dot(ckv, wk_ref[:, j * pair:(j + 1) * pair], preferred_element_type=F32)
        for t in range(2):
            hd = 2 * j + t
            k_ref[hd, :, :D_NOPE] = _rms(kk[:, t * D_NOPE:(t + 1) * D_NOPE], gkn_ref[...]).astype(BF16)
            k_ref[hd, :, D_NOPE:D_QK] = kr
            k_ref[hd, :, D_QK:] = jnp.zeros((tm, QK_PAD - D_QK), BF16)
    vt = _dot_nt(wv_ref[...], ckv)
    ones_row = (lax.broadcasted_iota(jnp.int32, (V_ROWS - D_V, tm), 0) == 0).astype(BF16)
    for hd in range(N_HEADS):
        v_ref[0, hd, :D_V, :] = vt[hd * D_V:(hd + 1) * D_V].astype(BF16)
        v_ref[0, hd, D_V:, :] = ones_row


def _kv_call(ckv, kr, w, tm):
    rows = ckv.shape[0]
    return pl.pallas_call(
        _kv_kernel,
        grid=(rows // tm,),
        in_specs=[
            pl.BlockSpec((tm, KV_LORA), lambda i: (i, 0)),
            pl.BlockSpec((tm, D_ROPE), lambda i: (i, 0)),
            pl.BlockSpec((KV_LORA, N_HEADS * D_NOPE), lambda i: (0, 0)),
            pl.BlockSpec((N_HEADS * D_V, KV_LORA), lambda i: (0, 0)),
            pl.BlockSpec((1, D_NOPE), lambda i: (0, 0)),
        ],
        out_specs=[
            pl.BlockSpec((N_HEADS, tm, QK_PAD), lambda i: (0, i, 0)),
            pl.BlockSpec((1, N_HEADS, V_ROWS, tm), lambda i: (i, 0, 0, 0)),
        ],
        out_shape=[
            jax.ShapeDtypeStruct((N_HEADS, rows, QK_PAD), BF16),
            jax.ShapeDtypeStruct((rows // tm, N_HEADS, V_ROWS, tm), BF16),
        ],
        compiler_params=_params(("arbitrary",)),
        name="kv",
    )(ckv, kr, w["w_uk"], w["w_uv"], w["g_kn"])


def _attn_kernel(qi_ref, kj_ref, flag_ref, q_ref, k_ref, v_ref, g_ref, o_ref,
                 m_ref, acc_ref, bias_ref, sa_ref, sb_ref, sc_ref, *, tq, tk):
    n = pl.program_id(1)
    qi, kj, flag = qi_ref[n], kj_ref[n], flag_ref[n]
    masked = (flag & 1) == 1
    last = (flag & 2) == 2

    @pl.when(kj == 0)
    def _():
        m_ref[...] = jnp.full(m_ref.shape, -jnp.inf, F32)
        acc_ref[...] = jnp.zeros(acc_ref.shape, F32)

    def run_heads(scores, softmax_pv):
        bufs = (sa_ref, sb_ref, sc_ref)
        scores(0, bufs[0])
        scores(1, bufs[1])
        for hd in range(N_HEADS):
            if hd + 2 < N_HEADS:
                scores(hd + 2, bufs[(hd + 2) % 3])
            softmax_pv(hd, bufs[hd % 3])

    def full_block():
        def scores(hd, s_ref):
            s_ref[...] = _dot_nt(k_ref[hd], q_ref[hd])

        def softmax_pv(hd, s_ref):
            m_prev = m_ref[hd]
            m_new = jnp.maximum(m_prev, jnp.max(s_ref[...], axis=0, keepdims=True))
            p = jnp.exp2(s_ref[...] - m_new)
            acc_ref[hd] = jnp.exp2(m_prev - m_new) * acc_ref[hd] + jnp.dot(
                v_ref[0, hd], p.astype(BF16), preferred_element_type=F32)
            m_ref[hd] = m_new

        run_heads(scores, softmax_pv)

    def diagonal_block():
        half = tk // 2

        def scores(hd, s_ref):
            s_ref[:half, :] = _dot_nt(k_ref[hd, :half, :], q_ref[hd]) + bias_ref[:half, :]
            s_ref[half:, half:] = _dot_nt(k_ref[hd, half:, :], q_ref[hd, half:, :]) + bias_ref[half:, half:]

        def softmax_pv(hd, s_ref):
            lo, hi = slice(0, half), slice(half, tq)
            m_lo = jnp.maximum(m_ref[hd, :, lo], jnp.max(s_ref[:half, lo], axis=0, keepdims=True))
            m_hi = jnp.maximum(m_ref[hd, :, hi], jnp.maximum(jnp.max(s_ref[:half, hi], axis=0, keepdims=True),
                                                            jnp.max(s_ref[half:, hi], axis=0, keepdims=True)))
            p_lo = jnp.exp2(s_ref[:half, lo] - m_lo)
            p_hi = jnp.exp2(s_ref[:, hi] - m_hi)
            acc_ref[hd, :, lo] = jnp.exp2(m_ref[hd, :, lo] - m_lo) * acc_ref[hd, :, lo] + jnp.dot(
                v_ref[0, hd, :, :half], p_lo.astype(BF16), preferred_element_type=F32)
            acc_ref[hd, :, hi] = jnp.exp2(m_ref[hd, :, hi] - m_hi) * acc_ref[hd, :, hi] + jnp.dot(
                v_ref[0, hd], p_hi.astype(BF16), preferred_element_type=F32)
            m_ref[hd, :, lo] = m_lo
            m_ref[hd, :, hi] = m_hi

        run_heads(scores, softmax_pv)

    @pl.when(masked)
    def _():
        shift = CHUNK.bit_length() - 1
        kc = (kj * tk + lax.broadcasted_iota(jnp.int32, (tk, 1), 0)) >> shift
        qc = (qi * tq + lax.broadcasted_iota(jnp.int32, (1, tq), 1)) >> shift
        bias_ref[...] = jnp.where(kc <= qc, 0.0, -jnp.inf).astype(F32)
        diagonal_block()

    @pl.when(jnp.logical_not(masked))
    def _():
        full_block()

    @pl.when(last)
    def _():
        for hd in range(N_HEADS):
            att = (acc_ref[hd, :D_V, :] * (1.0 / acc_ref[hd, D_V:D_V + 1, :])).T
            o_ref[:, hd * D_V:(hd + 1) * D_V] = (att * g_ref[:, hd * D_V:(hd + 1) * D_V]).astype(BF16)


def _attn_schedule(t_len, tq, tk):
    qi, kj, flag = [], [], []
    for i in range(t_len // tq):
        first_chunk = (i * tq) // CHUNK
        last_blk = ((i + 1) * tq - 1) // tk
        for j in range(last_blk + 1):
            needs_mask = ((j + 1) * tk - 1) // CHUNK > first_chunk
            qi.append(i)
            kj.append(j)
            flag.append(int(needs_mask) + 2 * int(j == last_blk))
    return tuple(np.asarray(a, np.int32) for a in (qi, kj, flag))


def _attn_call(q, k, vt, gate, batch, t_len, tq, tk):
    assert tq == tk and tq % (2 * CHUNK) == 0 and tq % (2 * LANES) == 0 and t_len % tq == 0
    rows = batch * t_len
    nq, nk = t_len // tq, t_len // tk
    qi, kj, flag = _attn_schedule(t_len, tq, tk)
    row_map = lambda b, n, qi, kj, fl: (b * nq + qi[n], 0)
    grid_spec = pltpu.PrefetchScalarGridSpec(
        num_scalar_prefetch=3,
        grid=(batch, len(qi)),
        in_specs=[
            pl.BlockSpec((N_HEADS, tq, QK_PAD), lambda b, n, qi, kj, fl: (0, b * nq + qi[n], 0)),
            pl.BlockSpec((N_HEADS, tk, QK_PAD), lambda b, n, qi, kj, fl: (0, b * nk + kj[n], 0)),
            pl.BlockSpec((1, N_HEADS, V_ROWS, tk), lambda b, n, qi, kj, fl: (b * nk + kj[n], 0, 0, 0)),
            pl.BlockSpec((tq, D_ATT), row_map),
        ],
        out_specs=pl.BlockSpec((tq, D_ATT), row_map),
        scratch_shapes=[
            pltpu.VMEM((N_HEADS, 1, tq), F32),
            pltpu.VMEM((N_HEADS, V_ROWS, tq), F32),
            pltpu.VMEM((tk, tq), F32),
            pltpu.VMEM((tk, tq), F32),
            pltpu.VMEM((tk, tq), F32),
            pltpu.VMEM((tk, tq), F32),
        ],
    )
    return pl.pallas_call(
        functools.partial(_attn_kernel, tq=tq, tk=tk),
        grid_spec=grid_spec,
        out_shape=jax.ShapeDtypeStruct((rows, D_ATT), BF16),
        compiler_params=_params(("arbitrary", "arbitrary")),
        name="attn",
    )(jnp.asarray(qi), jnp.asarray(kj), jnp.asarray(flag), q, k, vt, gate)


def _cache_attn_kernel(q_ref, ckv_ref, krt_ref, ckvn_ref, krn_ref, wk_ref, wv_ref, gkn_ref, g_ref, o_ref,
                       lhs_ref, qr_ref, s_ref, s2_ref, sn_ref, m_ref, l_ref, acc_ref, *, t_len, sub):
    kj = pl.program_id(1)
    n_w = N_HEADS * D_NOPE

    @pl.when(kj == 0)
    def _():
        lhs_ref[0:n_w, :] = wk_ref[...]
        for hd in range(N_HEADS):
            qn = (q_ref[hd, :, :D_NOPE].astype(F32) * gkn_ref[...]).astype(BF16)
            lhs_ref[n_w + hd * t_len:n_w + (hd + 1) * t_len, :] = jnp.dot(
                qn, wk_ref[hd * D_NOPE:(hd + 1) * D_NOPE, :], preferred_element_type=F32).astype(BF16)
            qr_ref[hd * t_len:(hd + 1) * t_len, :] = q_ref[hd, :, D_NOPE:D_QK]
        m_ref[...] = jnp.full(m_ref.shape, -jnp.inf, F32)
        l_ref[...] = jnp.zeros(l_ref.shape, F32)
        acc_ref[...] = jnp.zeros(acc_ref.shape, F32)

    def scores(ckv, s_rope, sc_ref):
        both = _dot_nt(lhs_ref[...], ckv)
        for hd in range(N_HEADS):
            kpre = both[hd * D_NOPE:(hd + 1) * D_NOPE]
            r = lax.rsqrt(jnp.sum(kpre * kpre, axis=0, keepdims=True) * (1.0 / D_NOPE) + EPS)
            rows = slice(hd * t_len, (hd + 1) * t_len)
            sc_ref[rows, :] = both[n_w + hd * t_len:n_w + (hd + 1) * t_len] * r + s_rope[rows]

    def softmax_pv(ckv, sc_ref):
        sc = sc_ref[...]
        m_prev = m_ref[...]
        m_new = jnp.maximum(m_prev, jnp.max(sc, axis=1, keepdims=True))
        alpha = jnp.exp2(m_prev - m_new)
        p = jnp.exp2(sc - m_new)
        l_ref[...] = alpha * l_ref[...] + jnp.sum(p, axis=1, keepdims=True)
        acc_ref[...] = alpha * acc_ref[...] + jnp.dot(p.astype(BF16), ckv, preferred_element_type=F32)
        m_ref[...] = m_new

    def sub_ckv(c):
        return ckv_ref[c * sub:(c + 1) * sub, :].astype(BF16)

    def sub_scores(c):
        s_rope = jnp.dot(qr_ref[...], krt_ref[0, :, c * sub:(c + 1) * sub].astype(BF16), preferred_element_type=F32)
        scores(sub_ckv(c), s_rope, (s_ref, s2_ref)[c % 2])

    n_sub = ckv_ref.shape[0] // sub
    sub_scores(0)
    for c in range(n_sub):
        if c + 1 < n_sub:
            sub_scores(c + 1)
        softmax_pv(sub_ckv(c), (s_ref, s2_ref)[c % 2])

    @pl.when(kj == pl.num_programs(1) - 1)
    def _():
        ckv_new = ckvn_ref[...].astype(BF16)
        scores(ckv_new, _dot_nt(qr_ref[...], krn_ref[...].astype(BF16)), sn_ref)
        softmax_pv(ckv_new, sn_ref)
        ctx = (acc_ref[...] * (1.0 / l_ref[...])).astype(BF16)
        for hd in range(N_HEADS):
            att = _dot_nt(ctx[hd * t_len:(hd + 1) * t_len], wv_ref[hd * D_V:(hd + 1) * D_V, :])
            o_ref[:, hd * D_V:(hd + 1) * D_V] = (att * g_ref[:, hd * D_V:(hd + 1) * D_V]).astype(BF16)


def _cache_attn_call(q, ckv_past, krt_past, ckv_new, kr_new, gate, w, batch, t_len, n_past, tk, sub):
    assert n_past % CHUNK == 0 and t_len <= CHUNK and n_past % tk == 0 and tk % sub == 0
    rows = batch * t_len
    nk = n_past // tk
    n_q = N_HEADS * t_len
    const = lambda shape: pl.BlockSpec(shape, lambda b, j: (0,) * len(shape))
    return pl.pallas_call(
        functools.partial(_cache_attn_kernel, t_len=t_len, sub=sub),
        grid=(batch, nk),
        in_specs=[
            pl.BlockSpec((N_HEADS, t_len, QK_PAD), lambda b, j: (0, b, 0)),
            pl.BlockSpec((tk, KV_LORA), lambda b, j: (b * nk + j, 0)),
            pl.BlockSpec((1, D_ROPE, tk), lambda b, j: (b, 0, j)),
            pl.BlockSpec((t_len, KV_LORA), lambda b, j: (b, 0)),
            pl.BlockSpec((t_len, D_ROPE), lambda b, j: (b, 0)),
            const((N_HEADS * D_NOPE, KV_LORA)),
            const((N_HEADS * D_V, KV_LORA)),
            const((1, D_NOPE)),
            pl.BlockSpec((t_len, D_ATT), lambda b, j: (b, 0)),
        ],
        out_specs=pl.BlockSpec((t_len, D_ATT), lambda b, j: (b, 0)),
        out_shape=jax.ShapeDtypeStruct((rows, D_ATT), BF16),
        scratch_shapes=[
            pltpu.VMEM((N_HEADS * D_NOPE + n_q, KV_LORA), BF16),
            pltpu.VMEM((n_q, D_ROPE), BF16),
            pltpu.VMEM((n_q, sub), F32),
            pltpu.VMEM((n_q, sub), F32),
            pltpu.VMEM((n_q, t_len), F32),
            pltpu.VMEM((n_q, 1), F32),
            pltpu.VMEM((n_q, 1), F32),
            pltpu.VMEM((n_q, KV_LORA), F32),
        ],
        compiler_params=_params(("arbitrary", "arbitrary")),
        name="cache_attn",
    )(q, ckv_past, krt_past, ckv_new, kr_new, w["w_uk_t"], w["w_uv"], w["g_kn"], gate)


def _out_kernel(a_ref, p_ref, w_ref, x_ref, y_ref):
    y_ref[...] = x_ref[...] + (jnp.dot(a_ref[...], w_ref[0], preferred_element_type=F32)
                               + jnp.dot(p_ref[...], w_ref[1], preferred_element_type=F32))


def _out_call(att, pool, w_out, x2, tm, tn):
    rows = x2.shape[0]
    return pl.pallas_call(
        _out_kernel,
        grid=(D_MODEL // tn, rows // tm),
        in_specs=[
            pl.BlockSpec((tm, D_ATT), lambda j, i: (i, 0)),
            pl.BlockSpec((tm, D_POOL), lambda j, i: (i, 0)),
            pl.BlockSpec((2, D_ATT, tn), lambda j, i: (0, 0, j)),
            pl.BlockSpec((tm, tn), lambda j, i: (i, j)),
        ],
        out_specs=pl.BlockSpec((tm, tn), lambda j, i: (i, j)),
        out_shape=jax.ShapeDtypeStruct((rows, D_MODEL), F32),
        compiler_params=_params(("arbitrary", "arbitrary")),
        name="out_proj",
    )(att, pool, w_out, x2)


def _rope_tables(pos0, t_len):
    half = D_ROPE // 2
    freqs = ROPE_THETA ** (-jnp.arange(half, dtype=F32) / half)
    ang = (pos0 + jnp.arange(t_len, dtype=jnp.int32)).astype(F32)[:, None] * freqs[None, :]
    cos, sin = jnp.cos(ang), jnp.sin(ang)
    zero = jnp.zeros_like(cos)
    halves = ROPE_PAD // D_ROPE
    return (jnp.concatenate([cos, cos] * halves, axis=1),
            jnp.concatenate([zero, sin] * halves, axis=1),
            jnp.concatenate([-sin, zero] * halves, axis=1))


def _prep_weights(g_norm, w_in, g_q_lat, w_uq, g_qn, g_qr, g_kv_lat, g_kr, w_ukv, g_kn,
                  w_pool, pool_scale, w_out):
    row = lambda v: v.reshape(1, -1).astype(F32)
    pad_row = lambda v: jnp.pad(v, (0, ROPE_PAD - D_ROPE)).reshape(1, -1).astype(F32)
    return {
        "g_norm": row(g_norm), "g_q_lat": row(g_q_lat), "g_qn": row(g_qn), "g_qr": row(jnp.tile(g_qr, ROPE_PAD // D_ROPE)),
        "g_kv_lat": row(g_kv_lat), "g_kr": pad_row(g_kr), "g_kn": row(g_kn), "pool_scale": row(pool_scale),
        "w_in": _prep_in_call(jnp.transpose(w_in), TILES.prep_rows),
        "w_uq": _prep_uq_call(jnp.transpose(w_uq, (1, 2, 0))),
        "w_uk": w_ukv[:, :, :D_NOPE].reshape(KV_LORA, N_HEADS * D_NOPE).astype(BF16),
        "w_uk_t": jnp.transpose(w_ukv[:, :, :D_NOPE], (1, 2, 0)).reshape(N_HEADS * D_NOPE, KV_LORA).astype(BF16),
        "w_uv": jnp.transpose(w_ukv[:, :, D_NOPE:], (1, 2, 0)).reshape(N_HEADS * D_V, KV_LORA).astype(BF16),
        "w_pool": w_pool.astype(BF16),
        "w_out": w_out.reshape(2, D_ATT, D_MODEL).astype(BF16),
    }


def _mixer_layer(x, ckv_past, kr_past, pool_past, w):
    batch, t_len, _ = x.shape
    rows = batch * t_len
    pos0 = 0 if ckv_past is None else ckv_past.shape[1]
    x2 = x.reshape(rows, D_MODEL)

    h, q, ckv, kr = _latent_call(x2, _rope_tables(pos0, t_len), w, t_len, TILES.latent_rows)
    gate = _gate_call(h, w["w_in"], TILES.gate_rows, TILES.gate_cols)
    pool, pool_state = _pool_call(h, pool_past, w, batch, t_len, pos0, TILES.pool_rows)
    if ckv_past is None:
        k, vt = _kv_call(ckv, kr, w, TILES.attn_block)
        att = _attn_call(q, k, vt, gate, batch, t_len, TILES.attn_block, TILES.attn_block)
    else:
        att = _cache_attn_call(q, ckv_past.reshape(batch * pos0, KV_LORA), jnp.swapaxes(kr_past, 1, 2), ckv, kr,
                               gate, w, batch, t_len, pos0, TILES.cache_keys, TILES.cache_sub)
    y = _out_call(att, pool, w["w_out"], x2, TILES.out_rows, TILES.out_cols)
    return (y.reshape(batch, t_len, D_MODEL), ckv.reshape(batch, t_len, KV_LORA),
            kr.reshape(batch, t_len, D_ROPE), pool_state)


def kernel(x_prompt, x_sample, cache_ckv, cache_krope, state_pool, g_norm, w_in, g_q_lat, w_uq, g_qn, g_qr,
           g_kv_lat, g_kr, w_ukv, g_kn, w_pool, pool_scale, w_out):
    depth = w_in.shape[0]
    batch = x_prompt.shape[0]
    yp, ys = x_prompt, x_sample
    outs = [[] for _ in range(6)]
    for l in range(depth):
        w = _prep_weights(g_norm[l], w_in[l], g_q_lat[l], w_uq[l], g_qn[l], g_qr[l], g_kv_lat[l], g_kr[l],
                          w_ukv[l], g_kn[l], w_pool[l], pool_scale[l], w_out[l])
        yp, a, b, c = _mixer_layer(yp, None, None, jnp.zeros((batch, POOL_BUF, D_POOL), F32), w)
        outs[0].append(a); outs[1].append(b); outs[2].append(c)
        ys, a, b, c = _mixer_layer(ys, cache_ckv[l], cache_krope[l], state_pool[l], w)
        outs[3].append(a); outs[4].append(b); outs[5].append(c)
    return (yp, ys) + tuple(jnp.stack(o) for o in outs)
```

```python
import functools
from typing import NamedTuple

import numpy as np
import jax
import jax.numpy as jnp
from jax import lax
from jax.experimental import pallas as pl
from jax.experimental.pallas import tpu as pltpu

D_MODEL = 4096
CHUNK = 64
N_HEADS = 16
D_NOPE = 128
D_ROPE = 64
D_QK = D_NOPE + D_ROPE
D_V = 128
D_ATT = N_HEADS * D_V
Q_LORA = 1024
KV_LORA = 512
D_POOL = 2048
POOL_WINDOWS = (2, 4, 8, 16)
N_POOL_GROUPS = len(POOL_WINDOWS)
D_POOL_GROUP = D_POOL // N_POOL_GROUPS
POOL_BUF = max(POOL_WINDOWS) - 1
ROPE_THETA = 10000.0
EPS = 1e-6

O_Z = Q_LORA + KV_LORA + D_ROPE
O_U = O_Z + D_ATT
O_ZP = O_U + D_POOL
D_IN = O_ZP + D_POOL

LANES = 128
SUBLANES = 8
BF16_SUBLANES = 2 * SUBLANES
ROPE_PAD = LANES
QK_PAD = D_NOPE + ROPE_PAD
LAT_PAD = Q_LORA + KV_LORA + ROPE_PAD
HEAD_GROUP = 4
V_ROWS = D_V + BF16_SUBLANES
HIST = 2 * (POOL_BUF + 1)
VMEM_LIMIT = 56 * 1024 * 1024

W_IN_ALIGN = 64

F32 = jnp.float32
BF16 = jnp.bfloat16


class _Tiles(NamedTuple):
    prep_rows: int = D_IN // 22
    latent_rows: int = 256
    gate_rows: int = 1024
    gate_cols: int = 1024
    pool_rows: int = 512
    attn_block: int = 512
    cache_keys: int = 2048
    cache_sub: int = 512
    out_rows: int = 1024
    out_cols: int = 1024


TILES = _Tiles()
_NT = (((1,), (1,)), ((), ()))


def _params(semantics):
    return pltpu.CompilerParams(dimension_semantics=semantics, vmem_limit_bytes=VMEM_LIMIT)


def _rms(x, g, n=None):
    n = x.shape[-1] if n is None else n
    ms = jnp.sum(x * x, axis=-1, keepdims=True) * (1.0 / n)
    return x * lax.rsqrt(ms + EPS) * g


def _rope(x, cos, sin_hi, sin_lo):
    return x * cos + pltpu.roll(x, D_ROPE // 2, 1) * sin_hi + pltpu.roll(x, ROPE_PAD - D_ROPE // 2, 1) * sin_lo


def _silu(z):
    return z * (1.0 / (1.0 + jnp.exp(-z)))


def _dot_nt(a, b):
    return lax.dot_general(a, b, _NT, preferred_element_type=F32)


def _cast_kernel(w_ref, o_ref):
    o_ref[...] = w_ref[...].astype(BF16)


def _prep_in_call(w_in_t, tr):
    return pl.pallas_call(
        _cast_kernel,
        grid=(D_IN // tr,),
        in_specs=[pl.BlockSpec((tr, D_MODEL), lambda n: (n, 0))],
        out_specs=pl.BlockSpec((tr, D_MODEL), lambda n: (n, 0)),
        out_shape=jax.ShapeDtypeStruct((D_IN, D_MODEL), BF16),
        compiler_params=_params(("arbitrary",)),
        name="prep_w_in",
    )(w_in_t)


def _w_in_window(n_rows, row_of):
    return pl.BlockSpec((pl.Element(n_rows), pl.Element(D_MODEL)),
                        lambda *idx: (pl.multiple_of(row_of(*idx), W_IN_ALIGN), 0))


def _prep_uq_kernel(w_ref, o_ref):
    rope0 = HEAD_GROUP * D_NOPE
    for j in range(HEAD_GROUP):
        o_ref[0, j * D_NOPE:(j + 1) * D_NOPE, :] = w_ref[j, :D_NOPE, :].astype(BF16)
        o_ref[0, rope0 + j * D_ROPE:rope0 + (j + 1) * D_ROPE, :] = w_ref[j, D_NOPE:, :].astype(BF16)


def _prep_uq_call(w_uq_t):
    groups = N_HEADS // HEAD_GROUP
    return pl.pallas_call(
        _prep_uq_kernel,
        grid=(groups,),
        in_specs=[pl.BlockSpec((HEAD_GROUP, D_QK, Q_LORA), lambda g: (g, 0, 0))],
        out_specs=pl.BlockSpec((1, HEAD_GROUP * D_QK, Q_LORA), lambda g: (g, 0, 0)),
        out_shape=jax.ShapeDtypeStruct((groups, HEAD_GROUP * D_QK, Q_LORA), BF16),
        compiler_params=_params(("arbitrary",)),
        name="prep_w_uq",
    )(w_uq_t)


def _latent_kernel(x_ref, gn_ref, wlat_ref, gql_ref, wuq_ref, gqn_ref, gqr_ref, gkv_ref, gkr_ref,
                   cos_ref, shi_ref, slo_ref, h_ref, q_ref, ckv_ref, kr_ref, *, scale):
    h = _rms(x_ref[...], gn_ref[...]).astype(BF16)
    h_ref[...] = h
    c = _dot_nt(h, wlat_ref[...])
    qlat = _rms(c[:, :Q_LORA], gql_ref[...]).astype(BF16)
    ckv_ref[...] = _rms(c[:, Q_LORA:Q_LORA + KV_LORA], gkv_ref[...])
    cos, shi, slo = cos_ref[...], shi_ref[...], slo_ref[...]
    c_kr = c[:, Q_LORA + KV_LORA:]
    c_kr = jnp.where(lax.broadcasted_iota(jnp.int32, c_kr.shape, 1) < D_ROPE, c_kr, 0.0)
    kr = _rope(_rms(c_kr, gkr_ref[...], D_ROPE), cos, shi, slo)
    kr_ref[...] = kr[:, :D_ROPE]
    low = lax.broadcasted_iota(jnp.int32, c_kr.shape, 1) < D_ROPE
    rope0 = HEAD_GROUP * D_NOPE
    for g in range(N_HEADS // HEAD_GROUP):
        q = _dot_nt(qlat, wuq_ref[g])
        for j in range(HEAD_GROUP):
            qn = _rms(q[:, j * D_NOPE:(j + 1) * D_NOPE], gqn_ref[...]) * scale
            q_ref[g * HEAD_GROUP + j, :, :D_NOPE] = qn.astype(BF16)
        for t in range(HEAD_GROUP // 2):
            x = q[:, rope0 + t * ROPE_PAD:rope0 + (t + 1) * ROPE_PAD]
            sq = x * x
            ms_lo = jnp.sum(jnp.where(low, sq, 0.0), axis=-1, keepdims=True) * (1.0 / D_ROPE)
            ms_hi = jnp.sum(jnp.where(low, 0.0, sq), axis=-1, keepdims=True) * (1.0 / D_ROPE)
            r = jnp.where(low, lax.rsqrt(ms_lo + EPS), lax.rsqrt(ms_hi + EPS))
            y = _rope(x * r * gqr_ref[...], cos, shi, slo) * scale
            hd = g * HEAD_GROUP + 2 * t
            q_ref[hd, :, D_NOPE:] = jnp.where(low, y, 0.0).astype(BF16)
            q_ref[hd + 1, :, D_NOPE:] = jnp.where(low, pltpu.roll(y, D_ROPE, 1), 0.0).astype(BF16)


def _latent_call(x2, tabs, w, t_len, tm):
    rows = x2.shape[0]
    if tm > t_len:
        tabs = [jnp.tile(t, (tm // t_len, 1)) for t in tabs]
        t_len = tm
    nt = t_len // tm
    resident = lambda shape: pl.BlockSpec(shape, lambda i: (0,) * len(shape), pipeline_mode=pl.Buffered(1))
    tab_spec = pl.BlockSpec((tm, ROPE_PAD), lambda i: (i % nt, 0))
    return pl.pallas_call(
        functools.partial(_latent_kernel, scale=float(D_QK ** -0.5 * np.log2(np.e))),
        grid=(rows // tm,),
        in_specs=[
            pl.BlockSpec((tm, D_MODEL), lambda i: (i, 0)),
            resident((1, D_MODEL)),
            resident((LAT_PAD, D_MODEL)),
            resident((1, Q_LORA)),
            resident((N_HEADS // HEAD_GROUP, HEAD_GROUP * D_QK, Q_LORA)),
            resident((1, D_NOPE)),
            resident((1, ROPE_PAD)),
            resident((1, KV_LORA)),
            resident((1, ROPE_PAD)),
            tab_spec, tab_spec, tab_spec,
        ],
        out_specs=[
            pl.BlockSpec((tm, D_MODEL), lambda i: (i, 0)),
            pl.BlockSpec((N_HEADS, tm, QK_PAD), lambda i: (0, i, 0)),
            pl.BlockSpec((tm, KV_LORA), lambda i: (i, 0)),
            pl.BlockSpec((tm, D_ROPE), lambda i: (i, 0)),
        ],
        out_shape=[
            jax.ShapeDtypeStruct((rows, D_MODEL), BF16),
            jax.ShapeDtypeStruct((N_HEADS, rows, QK_PAD), BF16),
            jax.ShapeDtypeStruct((rows, KV_LORA), F32),
            jax.ShapeDtypeStruct((rows, D_ROPE), F32),
        ],
        compiler_params=_params(("arbitrary",)),
        name="latent",
    )(x2, w["g_norm"], w["w_in"], w["g_q_lat"], w["w_uq"], w["g_qn"], w["g_qr"], w["g_kv_lat"], w["g_kr"],
      *tabs)


def _gate_kernel(h_ref, w_ref, o_ref):
    o_ref[...] = _silu(_dot_nt(h_ref[...], w_ref[...]))


def _gate_call(h, w_all, tm, tn):
    rows = h.shape[0]
    return pl.pallas_call(
        _gate_kernel,
        grid=(D_ATT // tn, rows // tm),
        in_specs=[pl.BlockSpec((tm, D_MODEL), lambda j, i: (i, 0)),
                  _w_in_window(tn, lambda j, i: O_Z + j * tn)],
        out_specs=pl.BlockSpec((tm, tn), lambda j, i: (i, j)),
        out_shape=jax.ShapeDtypeStruct((rows, D_ATT), F32),
        compiler_params=_params(("arbitrary", "arbitrary")),
        name="gate",
    )(h, w_all)


def _pool_kernel(*refs, nb, tt, pos0, lookahead):
    if lookahead:
        h_ref, hn_ref, wu_ref, wz_ref, past_ref, wp_ref, ps_ref, o_ref, st_ref, ext_ref, b1_ref, b2_ref, p_ref, un_ref = refs
    else:
        h_ref, wu_ref, wz_ref, past_ref, wp_ref, ps_ref, o_ref, st_ref, ext_ref, b1_ref, b2_ref, p_ref = refs
    g = pl.program_id(0)
    i = pl.program_id(2)

    @pl.when(i == 0)
    def _():
        for b in range(nb):
            ext_ref[b, 0:HIST - POOL_BUF, :] = jnp.zeros((HIST - POOL_BUF, D_POOL_GROUP), F32)
            ext_ref[b, HIST - POOL_BUF:HIST, :] = past_ref[b]

    first = jnp.logical_and(pl.program_id(1) == 0, i == 0) if lookahead else i == 0

    @pl.when(first)
    def _():
        u0 = _dot_nt(h_ref[...], wu_ref[...])
        for b in range(nb):
            ext_ref[b, HIST:HIST + tt, :] = u0[b * tt:(b + 1) * tt]

    if lookahead:
        @pl.when(i > 0)
        def _():
            for b in range(nb):
                ext_ref[b, 0:HIST, :] = ext_ref[b, tt:tt + HIST, :]

        @pl.when(jnp.logical_not(first))
        def _():
            for b in range(nb):
                ext_ref[b, HIST:HIST + tt, :] = un_ref[b * tt:(b + 1) * tt, :]

    pos = pos0 + i * tt + lax.broadcasted_iota(jnp.int32, (tt, 1), 0)

    for gi, win in enumerate(POOL_WINDOWS):
        @pl.when(g == gi)
        def _(win=win):
            steps = win.bit_length() - 1
            src = ext_ref
            for k in range(steps):
                dst = (b1_ref, b2_ref)[k % 2]
                lo, sh = SUBLANES * (k + 1), 1 << k
                for b in range(nb):
                    dst[b, lo:HIST + tt, :] = src[b, lo:HIST + tt, :] + src[b, lo - sh:HIST + tt - sh, :]
                src = dst
            cnt = jnp.minimum(pos + 1, win).astype(F32)
            for b in range(nb):
                p_ref[b * tt:(b + 1) * tt, :] = (src[b, HIST:HIST + tt, :] / cnt
                                                 - ext_ref[b, HIST:HIST + tt, :]).astype(BF16)
            gate = _silu(_dot_nt(h_ref[...], wz_ref[...]))
            if lookahead:
                un_ref[...] = _dot_nt(hn_ref[...], wu_ref[...])
            ph = jnp.dot(p_ref[...], wp_ref[0], preferred_element_type=F32)
            o_ref[...] = (ph * ps_ref[...] * gate).astype(BF16)

    for b in range(nb):
        st_ref[b] = ext_ref[b, HIST + tt - POOL_BUF:HIST + tt, :]


def _pool_call(h, pool_past, w, batch, t_len, pos0, tm):
    rows = h.shape[0]
    tt = min(tm, t_len)
    nb = tm // tt
    nt = t_len // tt
    buf = pltpu.VMEM((nb, tt + HIST, D_POOL_GROUP), F32)
    lookahead = nt > 1
    h_specs = [pl.BlockSpec((tm, D_MODEL), lambda g, b, i: (b * nt + i, 0))]
    scratch = [buf, buf, buf, pltpu.VMEM((tm, D_POOL_GROUP), BF16)]
    if lookahead:
        h_specs.append(pl.BlockSpec((tm, D_MODEL), lambda g, b, i: (jnp.minimum(b * nt + i + 1, rows // tm - 1), 0)))
        scratch.append(pltpu.VMEM((tm, D_POOL_GROUP), F32))
    return pl.pallas_call(
        functools.partial(_pool_kernel, nb=nb, tt=tt, pos0=pos0, lookahead=lookahead),
        grid=(N_POOL_GROUPS, batch // nb, nt),
        in_specs=h_specs + [
            _w_in_window(D_POOL_GROUP, lambda g, b, i: O_U + g * D_POOL_GROUP),
            _w_in_window(D_POOL_GROUP, lambda g, b, i: O_ZP + g * D_POOL_GROUP),
            pl.BlockSpec((nb, POOL_BUF, D_POOL_GROUP), lambda g, b, i: (b, 0, g)),
            pl.BlockSpec((1, D_POOL_GROUP, D_POOL_GROUP), lambda g, b, i: (g, 0, 0)),
            pl.BlockSpec((1, D_POOL_GROUP), lambda g, b, i: (0, g)),
        ],
        out_specs=[
            pl.BlockSpec((tm, D_POOL_GROUP), lambda g, b, i: (b * nt + i, g)),
            pl.BlockSpec((nb, POOL_BUF, D_POOL_GROUP), lambda g, b, i: (b, 0, g)),
        ],
        out_shape=[
            jax.ShapeDtypeStruct((rows, D_POOL), BF16),
            jax.ShapeDtypeStruct((batch, POOL_BUF, D_POOL), F32),
        ],
        scratch_shapes=scratch,
        compiler_params=_params(("arbitrary", "arbitrary", "arbitrary")),
        name="pool",
    )(*([h] * len(h_specs)), w["w_in"], w["w_in"], pool_past, w["w_pool"], w["pool_scale"])


def _kv_kernel(ckv_ref, kr_ref, wk_ref, wv_ref, gkn_ref, k_ref, v_ref):
    ckv = ckv_ref[...].astype(BF16)
    kr = kr_ref[...].astype(BF16)
    tm = ckv.shape[0]
    pair = 2 * D_NOPE
    for j in range(N_HEADS // 2):
        kk = jnp.dot(ckv, wk_ref[:, j * pair:(j + 1) * pair], preferred_element_type=F32)
        for t in range(2):
            hd = 2 * j + t
            k_ref[hd, :, :D_NOPE] = _rms(kk[:, t * D_NOPE:(t + 1) * D_NOPE], gkn_ref[...]).astype(BF16)
            k_ref[hd, :, D_NOPE:D_QK] = kr
            k_ref[hd, :, D_QK:] = jnp.zeros((tm, QK_PAD - D_QK), BF16)
    vt = _dot_nt(wv_ref[...], ckv)
    ones_row = (lax.broadcasted_iota(jnp.int32, (V_ROWS - D_V, tm), 0) == 0).astype(BF16)
    for hd in range(N_HEADS):
        v_ref[0, hd, :D_V, :] = vt[hd * D_V:(hd + 1) * D_V].astype(BF16)
        v_ref[0, hd, D_V:, :] = ones_row


def _kv_call(ckv, kr, w, tm):
    rows = ckv.shape[0]
    return pl.pallas_call(
        _kv_kernel,
        grid=(rows // tm,),
        in_specs=[
            pl.BlockSpec((tm, KV_LORA), lambda i: (i, 0)),
            pl.BlockSpec((tm, D_ROPE), lambda i: (i, 0)),
            pl.BlockSpec((KV_LORA, N_HEADS * D_NOPE), lambda i: (0, 0)),
            pl.BlockSpec((N_HEADS * D_V, KV_LORA), lambda i: (0, 0)),
            pl.BlockSpec((1, D_NOPE), lambda i: (0, 0)),
        ],
        out_specs=[
            pl.BlockSpec((N_HEADS, tm, QK_PAD), lambda i: (0, i, 0)),
            pl.BlockSpec((1, N_HEADS, V_ROWS, tm), lambda i: (i, 0, 0, 0)),
        ],
        out_shape=[
            jax.ShapeDtypeStruct((N_HEADS, rows, QK_PAD), BF16),
            jax.ShapeDtypeStruct((rows // tm, N_HEADS, V_ROWS, tm), BF16),
        ],
        compiler_params=_params(("arbitrary",)),
        name="kv",
    )(ckv, kr, w["w_uk"], w["w_uv"], w["g_kn"])


def _attn_kernel(qi_ref, kj_ref, flag_ref, q_ref, k_ref, v_ref, g_ref, o_ref,
                 m_ref, acc_ref, bias_ref, sa_ref, sb_ref, sc_ref, *, tq, tk):
    n = pl.program_id(1)
    qi, kj, flag = qi_ref[n], kj_ref[n], flag_ref[n]
    masked = (flag & 1) == 1
    last = (flag & 2) == 2

    @pl.when(kj == 0)
    def _():
        m_ref[...] = jnp.full(m_ref.shape, -jnp.inf, F32)
        acc_ref[...] = jnp.zeros(acc_ref.shape, F32)

    def run_heads(scores, softmax_pv):
        bufs = (sa_ref, sb_ref, sc_ref)
        scores(0, bufs[0])
        scores(1, bufs[1])
        for hd in range(N_HEADS):
            if hd + 2 < N_HEADS:
                scores(hd + 2, bufs[(hd + 2) % 3])
            softmax_pv(hd, bufs[hd % 3])

    def full_block():
        def scores(hd, s_ref):
            s_ref[...] = _dot_nt(k_ref[hd], q_ref[hd])

        def softmax_pv(hd, s_ref):
            m_prev = m_ref[hd]
            m_new = jnp.maximum(m_prev, jnp.max(s_ref[...], axis=0, keepdims=True))
            p = jnp.exp2(s_ref[...] - m_new)
            acc_ref[hd] = jnp.exp2(m_prev - m_new) * acc_ref[hd] + jnp.dot(
                v_ref[0, hd], p.astype(BF16), preferred_element_type=F32)
            m_ref[hd] = m_new

        run_heads(scores, softmax_pv)

    def diagonal_block():
        half = tk // 2

        def scores(hd, s_ref):
            s_ref[:half, :] = _dot_nt(k_ref[hd, :half, :], q_ref[hd]) + bias_ref[:half, :]
            s_ref[half:, half:] = _dot_nt(k_ref[hd, half:, :], q_ref[hd, half:, :]) + bias_ref[half:, half:]

        def softmax_pv(hd, s_ref):
            lo, hi = slice(0, half), slice(half, tq)
            m_lo = jnp.maximum(m_ref[hd, :, lo], jnp.max(s_ref[:half, lo], axis=0, keepdims=True))
            m_hi = jnp.maximum(m_ref[hd, :, hi], jnp.maximum(jnp.max(s_ref[:half, hi], axis=0, keepdims=True),
                                                            jnp.max(s_ref[half:, hi], axis=0, keepdims=True)))
            p_lo = jnp.exp2(s_ref[:half, lo] - m_lo)
            p_hi = jnp.exp2(s_ref[:, hi] - m_hi)
            acc_ref[hd, :, lo] = jnp.exp2(m_ref[hd, :, lo] - m_lo) * acc_ref[hd, :, lo] + jnp.dot(
                v_ref[0, hd, :, :half], p_lo.astype(BF16), preferred_element_type=F32)
            acc_ref[hd, :, hi] = jnp.exp2(m_ref[hd, :, hi] - m_hi) * acc_ref[hd, :, hi] + jnp.dot(
                v_ref[0, hd], p_hi.astype(BF16), preferred_element_type=F32)
            m_ref[hd, :, lo] = m_lo
            m_ref[hd, :, hi] = m_hi

        run_heads(scores, softmax_pv)

    @pl.when(masked)
    def _():
        shift = CHUNK.bit_length() - 1
        kc = (kj * tk + lax.broadcasted_iota(jnp.int32, (tk, 1), 0)) >> shift
        qc = (qi * tq + lax.broadcasted_iota(jnp.int32, (1, tq), 1)) >> shift
        bias_ref[...] = jnp.where(kc <= qc, 0.0, -jnp.inf).astype(F32)
        diagonal_block()

    @pl.when(jnp.logical_not(masked))
    def _():
        full_block()

    @pl.when(last)
    def _():
        for hd in range(N_HEADS):
            att = (acc_ref[hd, :D_V, :] * (1.0 / acc_ref[hd, D_V:D_V + 1, :])).T
            o_ref[:, hd * D_V:(hd + 1) * D_V] = (att * g_ref[:, hd * D_V:(hd + 1) * D_V]).astype(BF16)


def _attn_schedule(t_len, tq, tk):
    qi, kj, flag = [], [], []
    for i in range(t_len // tq):
        first_chunk = (i * tq) // CHUNK
        last_blk = ((i + 1) * tq - 1) // tk
        for j in range(last_blk + 1):
            needs_mask = ((j + 1) * tk - 1) // CHUNK > first_chunk
            qi.append(i)
            kj.append(j)
            flag.append(int(needs_mask) + 2 * int(j == last_blk))
    return tuple(np.asarray(a, np.int32) for a in (qi, kj, flag))


def _attn_call(q, k, vt, gate, batch, t_len, tq, tk):
    assert tq == tk and tq % (2 * CHUNK) == 0 and tq % (2 * LANES) == 0 and t_len % tq == 0
    rows = batch * t_len
    nq, nk = t_len // tq, t_len // tk
    qi, kj, flag = _attn_schedule(t_len, tq, tk)
    row_map = lambda b, n, qi, kj, fl: (b * nq + qi[n], 0)
    grid_spec = pltpu.PrefetchScalarGridSpec(
        num_scalar_prefetch=3,
        grid=(batch, len(qi)),
        in_specs=[
            pl.BlockSpec((N_HEADS, tq, QK_PAD), lambda b, n, qi, kj, fl: (0, b * nq + qi[n], 0)),
            pl.BlockSpec((N_HEADS, tk, QK_PAD), lambda b, n, qi, kj, fl: (0, b * nk + kj[n], 0)),
            pl.BlockSpec((1, N_HEADS, V_ROWS, tk), lambda b, n, qi, kj, fl: (b * nk + kj[n], 0, 0, 0)),
            pl.BlockSpec((tq, D_ATT), row_map),
        ],
        out_specs=pl.BlockSpec((tq, D_ATT), row_map),
        scratch_shapes=[
            pltpu.VMEM((N_HEADS, 1, tq), F32),
            pltpu.VMEM((N_HEADS, V_ROWS, tq), F32),
            pltpu.VMEM((tk, tq), F32),
            pltpu.VMEM((tk, tq), F32),
            pltpu.VMEM((tk, tq), F32),
            pltpu.VMEM((tk, tq), F32),
        ],
    )
    return pl.pallas_call(
        functools.partial(_attn_kernel, tq=tq, tk=tk),
        grid_spec=grid_spec,
        out_shape=jax.ShapeDtypeStruct((rows, D_ATT), BF16),
        compiler_params=_params(("arbitrary", "arbitrary")),
        name="attn",
    )(jnp.asarray(qi), jnp.asarray(kj), jnp.asarray(flag), q, k, vt, gate)


def _cache_attn_kernel(q_ref, ckv_ref, krt_ref, ckvn_ref, krn_ref, wk_ref, wv_ref, gkn_ref, g_ref, o_ref,
                       lhs_ref, qr_ref, s_ref, s2_ref, sn_ref, m_ref, l_ref, acc_ref, *, t_len, sub):
    kj = pl.program_id(1)
    n_w = N_HEADS * D_NOPE

    @pl.when(kj == 0)
    def _():
        lhs_ref[0:n_w, :] = wk_ref[...]
        for hd in range(N_HEADS):
            qn = (q_ref[hd, :, :D_NOPE].astype(F32) * gkn_ref[...]).astype(BF16)
            lhs_ref[n_w + hd * t_len:n_w + (hd + 1) * t_len, :] = jnp.dot(
                qn, wk_ref[hd * D_NOPE:(hd + 1) * D_NOPE, :], preferred_element_type=F32).astype(BF16)
            qr_ref[hd * t_len:(hd + 1) * t_len, :] = q_ref[hd, :, D_NOPE:D_QK]
        m_ref[...] = jnp.full(m_ref.shape, -jnp.inf, F32)
        l_ref[...] = jnp.zeros(l_ref.shape, F32)
        acc_ref[...] = jnp.zeros(acc_ref.shape, F32)

    def scores(ckv, s_rope, sc_ref):
        both = _dot_nt(lhs_ref[...], ckv)
        for hd in range(N_HEADS):
            kpre = both[hd * D_NOPE:(hd + 1) * D_NOPE]
            r = lax.rsqrt(jnp.sum(kpre * kpre, axis=0, keepdims=True) * (1.0 / D_NOPE) + EPS)
            rows = slice(hd * t_len, (hd + 1) * t_len)
            sc_ref[rows, :] = both[n_w + hd * t_len:n_w + (hd + 1) * t_len] * r + s_rope[rows]

    def softmax_pv(ckv, sc_ref):
        sc = sc_ref[...]
        m_prev = m_ref[...]
        m_new = jnp.maximum(m_prev, jnp.max(sc, axis=1, keepdims=True))
        alpha = jnp.exp2(m_prev - m_new)
        p = jnp.exp2(sc - m_new)
        l_ref[...] = alpha * l_ref[...] + jnp.sum(p, axis=1, keepdims=True)
        acc_ref[...] = alpha * acc_ref[...] + jnp.dot(p.astype(BF16), ckv, preferred_element_type=F32)
        m_ref[...] = m_new

    def sub_ckv(c):
        return ckv_ref[c * sub:(c + 1) * sub, :].astype(BF16)

    def sub_scores(c):
        s_rope = jnp.dot(qr_ref[...], krt_ref[0, :, c * sub:(c + 1) * sub].astype(BF16), preferred_element_type=F32)
        scores(sub_ckv(c), s_rope, (s_ref, s2_ref)[c % 2])

    n_sub = ckv_ref.shape[0] // sub
    sub_scores(0)
    for c in range(n_sub):
        if c + 1 < n_sub:
            sub_scores(c + 1)
        softmax_pv(sub_ckv(c), (s_ref, s2_ref)[c % 2])

    @pl.when(kj == pl.num_programs(1) - 1)
    def _():
        ckv_new = ckvn_ref[...].astype(BF16)
        scores(ckv_new, _dot_nt(qr_ref[...], krn_ref[...].astype(BF16)), sn_ref)
        softmax_pv(ckv_new, sn_ref)
        ctx = (acc_ref[...] * (1.0 / l_ref[...])).astype(BF16)
        for hd in range(N_HEADS):
            att = _dot_nt(ctx[hd * t_len:(hd + 1) * t_len], wv_ref[hd * D_V:(hd + 1) * D_V, :])
            o_ref[:, hd * D_V:(hd + 1) * D_V] = (att * g_ref[:, hd * D_V:(hd + 1) * D_V]).astype(BF16)


def _cache_attn_call(q, ckv_past, krt_past, ckv_new, kr_new, gate, w, batch, t_len, n_past, tk, sub):
    assert n_past % CHUNK == 0 and t_len <= CHUNK and n_past % tk == 0 and tk % sub == 0
    rows = batch * t_len
    nk = n_past // tk
    n_q = N_HEADS * t_len
    const = lambda shape: pl.BlockSpec(shape, lambda b, j: (0,) * len(shape))
    return pl.pallas_call(
        functools.partial(_cache_attn_kernel, t_len=t_len, sub=sub),
        grid=(batch, nk),
        in_specs=[
            pl.BlockSpec((N_HEADS, t_len, QK_PAD), lambda b, j: (0, b, 0)),
            pl.BlockSpec((tk, KV_LORA), lambda b, j: (b * nk + j, 0)),
            pl.BlockSpec((1, D_ROPE, tk), lambda b, j: (b, 0, j)),
            pl.BlockSpec((t_len, KV_LORA), lambda b, j: (b, 0)),
            pl.BlockSpec((t_len, D_ROPE), lambda b, j: (b, 0)),
            const((N_HEADS * D_NOPE, KV_LORA)),
            const((N_HEADS * D_V, KV_LORA)),
            const((1, D_NOPE)),
            pl.BlockSpec((t_len, D_ATT), lambda b, j: (b, 0)),
        ],
        out_specs=pl.BlockSpec((t_len, D_ATT), lambda b, j: (b, 0)),
        out_shape=jax.ShapeDtypeStruct((rows, D_ATT), BF16),
        scratch_shapes=[
            pltpu.VMEM((N_HEADS * D_NOPE + n_q, KV_LORA), BF16),
            pltpu.VMEM((n_q, D_ROPE), BF16),
            pltpu.VMEM((n_q, sub), F32),
            pltpu.VMEM((n_q, sub), F32),
            pltpu.VMEM((n_q, t_len), F32),
            pltpu.VMEM((n_q, 1), F32),
            pltpu.VMEM((n_q, 1), F32),
            pltpu.VMEM((n_q, KV_LORA), F32),
        ],
        compiler_params=_params(("arbitrary", "arbitrary")),
        name="cache_attn",
    )(q, ckv_past, krt_past, ckv_new, kr_new, w["w_uk_t"], w["w_uv"], w["g_kn"], gate)


def _out_kernel(a_ref, p_ref, w_ref, x_ref, y_ref):
    y_ref[...] = x_ref[...] + (jnp.dot(a_ref[...], w_ref[0], preferred_element_type=F32)
                               + jnp.dot(p_ref[...], w_ref[1], preferred_element_type=F32))


def _out_call(att, pool, w_out, x2, tm, tn):
    rows = x2.shape[0]
    return pl.pallas_call(
        _out_kernel,
        grid=(D_MODEL // tn, rows // tm),
        in_specs=[
            pl.BlockSpec((tm, D_ATT), lambda j, i: (i, 0)),
            pl.BlockSpec((tm, D_POOL), lambda j, i: (i, 0)),
            pl.BlockSpec((2, D_ATT, tn), lambda j, i: (0, 0, j)),
            pl.BlockSpec((tm, tn), lambda j, i: (i, j)),
        ],
        out_specs=pl.BlockSpec((tm, tn), lambda j, i: (i, j)),
        out_shape=jax.ShapeDtypeStruct((rows, D_MODEL), F32),
        compiler_params=_params(("arbitrary", "arbitrary")),
        name="out_proj",
    )(att, pool, w_out, x2)


def _rope_tables(pos0, t_len):
    half = D_ROPE // 2
    freqs = ROPE_THETA ** (-jnp.arange(half, dtype=F32) / half)
    ang = (pos0 + jnp.arange(t_len, dtype=jnp.int32)).astype(F32)[:, None] * freqs[None, :]
    cos, sin = jnp.cos(ang), jnp.sin(ang)
    reps = ROPE_PAD // half
    cos, sin = jnp.tile(cos, (1, reps)), jnp.tile(sin, (1, reps))
    is_x1 = (jnp.arange(ROPE_PAD) // half) % 2 == 0
    return cos, jnp.where(is_x1, 0.0, sin), jnp.where(is_x1, -sin, 0.0)


def _prep_weights(g_norm, w_in, g_q_lat, w_uq, g_qn, g_qr, g_kv_lat, g_kr, w_ukv, g_kn,
                  w_pool, pool_scale, w_out):
    row = lambda v: v.reshape(1, -1).astype(F32)
    pad_row = lambda v: jnp.pad(v, (0, ROPE_PAD - D_ROPE)).reshape(1, -1).astype(F32)
    return {
        "g_norm": row(g_norm), "g_q_lat": row(g_q_lat), "g_qn": row(g_qn), "g_qr": row(jnp.tile(g_qr, ROPE_PAD // D_ROPE)),
        "g_kv_lat": row(g_kv_lat), "g_kr": pad_row(g_kr), "g_kn": row(g_kn), "pool_scale": row(pool_scale),
        "w_in": _prep_in_call(jnp.transpose(w_in), TILES.prep_rows),
        "w_uq": _prep_uq_call(jnp.transpose(w_uq, (1, 2, 0))),
        "w_uk": w_ukv[:, :, :D_NOPE].reshape(KV_LORA, N_HEADS * D_NOPE).astype(BF16),
        "w_uk_t": jnp.transpose(w_ukv[:, :, :D_NOPE], (1, 2, 0)).reshape(N_HEADS * D_NOPE, KV_LORA).astype(BF16),
        "w_uv": jnp.transpose(w_ukv[:, :, D_NOPE:], (1, 2, 0)).reshape(N_HEADS * D_V, KV_LORA).astype(BF16),
        "w_pool": w_pool.astype(BF16),
        "w_out": w_out.reshape(2, D_ATT, D_MODEL).astype(BF16),
    }


def _mixer_layer(x, ckv_past, kr_past, pool_past, w):
    batch, t_len, _ = x.shape
    rows = batch * t_len
    pos0 = 0 if ckv_past is None else ckv_past.shape[1]
    x2 = x.reshape(rows, D_MODEL)

    h, q, ckv, kr = _latent_call(x2, _rope_tables(pos0, t_len), w, t_len, TILES.latent_rows)
    gate = _gate_call(h, w["w_in"], TILES.gate_rows, TILES.gate_cols)
    pool, pool_state = _pool_call(h, pool_past, w, batch, t_len, pos0, TILES.pool_rows)
    if ckv_past is None:
        k, vt = _kv_call(ckv, kr, w, TILES.attn_block)
        att = _attn_call(q, k, vt, gate, batch, t_len, TILES.attn_block, TILES.attn_block)
    else:
        att = _cache_attn_call(q, ckv_past.reshape(batch * pos0, KV_LORA), jnp.swapaxes(kr_past, 1, 2), ckv, kr,
                               gate, w, batch, t_len, pos0, TILES.cache_keys, TILES.cache_sub)
    y = _out_call(att, pool, w["w_out"], x2, TILES.out_rows, TILES.out_cols)
    return (y.reshape(batch, t_len, D_MODEL), ckv.reshape(batch, t_len, KV_LORA),
            kr.reshape(batch, t_len, D_ROPE), pool_state)


def kernel(x_prompt, x_sample, cache_ckv, cache_krope, state_pool, g_norm, w_in, g_q_lat, w_uq, g_qn, g_qr,
           g_kv_lat, g_kr, w_ukv, g_kn, w_pool, pool_scale, w_out):
    depth = w_in.shape[0]
    batch = x_prompt.shape[0]
    yp, ys = x_prompt, x_sample
    outs = [[] for _ in range(6)]
    for l in range(depth):
        w = _prep_weights(g_norm[l], w_in[l], g_q_lat[l], w_uq[l], g_qn[l], g_qr[l], g_kv_lat[l], g_kr[l],
                          w_ukv[l], g_kn[l], w_pool[l], pool_scale[l], w_out[l])
        yp, a, b, c = _mixer_layer(yp, None, None, jnp.zeros((batch, POOL_BUF, D_POOL), F32), w)
        outs[0].append(a); outs[1].append(b); outs[2].append(c)
        ys, a, b, c = _mixer_layer(ys, cache_ckv[l], cache_krope[l], state_pool[l], w)
        outs[3].append(a); outs[4].append(b); outs[5].append(c)
    return (yp, ys) + tuple(jnp.stack(o) for o in outs)
```

```python
import functools
from typing import NamedTuple

import numpy as np
import jax
import jax.numpy as jnp
from jax import lax
from jax.experimental import pallas as pl
from jax.experimental.pallas import tpu as pltpu

D_MODEL = 4096
CHUNK = 64
N_HEADS = 16
D_NOPE = 128
D_ROPE = 64
D_QK = D_NOPE + D_ROPE
D_V = 128
D_ATT = N_HEADS * D_V
Q_LORA = 1024
KV_LORA = 512
D_POOL = 2048
POOL_WINDOWS = (2, 4, 8, 16)
N_POOL_GROUPS = len(POOL_WINDOWS)
D_POOL_GROUP = D_POOL // N_POOL_GROUPS
POOL_BUF = max(POOL_WINDOWS) - 1
ROPE_THETA = 10000.0
EPS = 1e-6

O_Z = Q_LORA + KV_LORA + D_ROPE
O_U = O_Z + D_ATT
O_ZP = O_U + D_POOL
D_IN = O_ZP + D_POOL

LANES = 128
SUBLANES = 8
BF16_SUBLANES = 2 * SUBLANES
ROPE_PAD = LANES
QK_PAD = D_NOPE + ROPE_PAD
LAT_PAD = Q_LORA + KV_LORA + ROPE_PAD
HEAD_GROUP = 4
V_ROWS = D_V + BF16_SUBLANES
HIST = 2 * (POOL_BUF + 1)
VMEM_LIMIT = 56 * 1024 * 1024

W_IN_ALIGN = 64

F32 = jnp.float32
BF16 = jnp.bfloat16


class _Tiles(NamedTuple):
    prep_rows: int = D_IN // 22
    latent_rows: int = 256
    gate_rows: int = 1024
    gate_cols: int = 1024
    pool_rows: int = 512
    attn_block: int = 512
    cache_keys: int = 2048
    cache_sub: int = 1024
    out_rows: int = 1024
    out_cols: int = 1024


TILES = _Tiles()
_NT = (((1,), (1,)), ((), ()))


def _params(semantics):
    return pltpu.CompilerParams(dimension_semantics=semantics, vmem_limit_bytes=VMEM_LIMIT)


def _rms(x, g, n=None):
    n = x.shape[-1] if n is None else n
    ms = jnp.sum(x * x, axis=-1, keepdims=True) * (1.0 / n)
    return x * lax.rsqrt(ms + EPS) * g


def _rope(x, cos, sin_hi, sin_lo):
    return x * cos + pltpu.roll(x, D_ROPE // 2, 1) * sin_hi + pltpu.roll(x, ROPE_PAD - D_ROPE // 2, 1) * sin_lo


def _silu(z):
    return z * (1.0 / (1.0 + jnp.exp(-z)))


def _dot_nt(a, b):
    return lax.dot_general(a, b, _NT, preferred_element_type=F32)


def _cast_kernel(w_ref, o_ref):
    o_ref[...] = w_ref[...].astype(BF16)


def _prep_in_call(w_in_t, tr):
    return pl.pallas_call(
        _cast_kernel,
        grid=(D_IN // tr,),
        in_specs=[pl.BlockSpec((tr, D_MODEL), lambda n: (n, 0))],
        out_specs=pl.BlockSpec((tr, D_MODEL), lambda n: (n, 0)),
        out_shape=jax.ShapeDtypeStruct((D_IN, D_MODEL), BF16),
        compiler_params=_params(("arbitrary",)),
        name="prep_w_in",
    )(w_in_t)


def _w_in_window(n_rows, row_of):
    return pl.BlockSpec((pl.Element(n_rows), pl.Element(D_MODEL)),
                        lambda *idx: (pl.multiple_of(row_of(*idx), W_IN_ALIGN), 0))


def _prep_uq_kernel(w_ref, o_ref):
    rope0 = HEAD_GROUP * D_NOPE
    for j in range(HEAD_GROUP):
        o_ref[0, j * D_NOPE:(j + 1) * D_NOPE, :] = w_ref[j, :D_NOPE, :].astype(BF16)
        o_ref[0, rope0 + j * D_ROPE:rope0 + (j + 1) * D_ROPE, :] = w_ref[j, D_NOPE:, :].astype(BF16)


def _prep_uq_call(w_uq_t):
    groups = N_HEADS // HEAD_GROUP
    return pl.pallas_call(
        _prep_uq_kernel,
        grid=(groups,),
        in_specs=[pl.BlockSpec((HEAD_GROUP, D_QK, Q_LORA), lambda g: (g, 0, 0))],
        out_specs=pl.BlockSpec((1, HEAD_GROUP * D_QK, Q_LORA), lambda g: (g, 0, 0)),
        out_shape=jax.ShapeDtypeStruct((groups, HEAD_GROUP * D_QK, Q_LORA), BF16),
        compiler_params=_params(("arbitrary",)),
        name="prep_w_uq",
    )(w_uq_t)


def _latent_kernel(x_ref, gn_ref, wlat_ref, gql_ref, wuq_ref, gqn_ref, gqr_ref, gkv_ref, gkr_ref,
                   cos_ref, shi_ref, slo_ref, h_ref, q_ref, ckv_ref, kr_ref, *, scale):
    h = _rms(x_ref[...], gn_ref[...]).astype(BF16)
    h_ref[...] = h
    c = _dot_nt(h, wlat_ref[...])
    qlat = _rms(c[:, :Q_LORA], gql_ref[...]).astype(BF16)
    ckv_ref[...] = _rms(c[:, Q_LORA:Q_LORA + KV_LORA], gkv_ref[...])
    cos, shi, slo = cos_ref[...], shi_ref[...], slo_ref[...]
    c_kr = c[:, Q_LORA + KV_LORA:]
    c_kr = jnp.where(lax.broadcasted_iota(jnp.int32, c_kr.shape, 1) < D_ROPE, c_kr, 0.0)
    kr = _rope(_rms(c_kr, gkr_ref[...], D_ROPE), cos, shi, slo)
    kr_ref[...] = kr[:, :D_ROPE]
    low = lax.broadcasted_iota(jnp.int32, c_kr.shape, 1) < D_ROPE
    rope0 = HEAD_GROUP * D_NOPE
    for g in range(N_HEADS // HEAD_GROUP):
        q = _dot_nt(qlat, wuq_ref[g])
        for j in range(HEAD_GROUP):
            qn = _rms(q[:, j * D_NOPE:(j + 1) * D_NOPE], gqn_ref[...]) * scale
            q_ref[g * HEAD_GROUP + j, :, :D_NOPE] = qn.astype(BF16)
        for t in range(HEAD_GROUP // 2):
            x = q[:, rope0 + t * ROPE_PAD:rope0 + (t + 1) * ROPE_PAD]
            sq = x * x
            ms_lo = jnp.sum(jnp.where(low, sq, 0.0), axis=-1, keepdims=True) * (1.0 / D_ROPE)
            ms_hi = jnp.sum(jnp.where(low, 0.0, sq), axis=-1, keepdims=True) * (1.0 / D_ROPE)
            r = jnp.where(low, lax.rsqrt(ms_lo + EPS), lax.rsqrt(ms_hi + EPS))
            y = _rope(x * r * gqr_ref[...], cos, shi, slo) * scale
            hd = g * HEAD_GROUP + 2 * t
            q_ref[hd, :, D_NOPE:] = jnp.where(low, y, 0.0).astype(BF16)
            q_ref[hd + 1, :, D_NOPE:] = jnp.where(low, pltpu.roll(y, D_ROPE, 1), 0.0).astype(BF16)


def _latent_call(x2, tabs, w, t_len, tm):
    rows = x2.shape[0]
    if tm > t_len:
        tabs = [jnp.tile(t, (tm // t_len, 1)) for t in tabs]
        t_len = tm
    nt = t_len // tm
    resident = lambda shape: pl.BlockSpec(shape, lambda i: (0,) * len(shape), pipeline_mode=pl.Buffered(1))
    tab_spec = pl.BlockSpec((tm, ROPE_PAD), lambda i: (i % nt, 0))
    return pl.pallas_call(
        functools.partial(_latent_kernel, scale=float(D_QK ** -0.5 * np.log2(np.e))),
        grid=(rows // tm,),
        in_specs=[
            pl.BlockSpec((tm, D_MODEL), lambda i: (i, 0)),
            resident((1, D_MODEL)),
            resident((LAT_PAD, D_MODEL)),
            resident((1, Q_LORA)),
            resident((N_HEADS // HEAD_GROUP, HEAD_GROUP * D_QK, Q_LORA)),
            resident((1, D_NOPE)),
            resident((1, ROPE_PAD)),
            resident((1, KV_LORA)),
            resident((1, ROPE_PAD)),
            tab_spec, tab_spec, tab_spec,
        ],
        out_specs=[
            pl.BlockSpec((tm, D_MODEL), lambda i: (i, 0)),
            pl.BlockSpec((N_HEADS, tm, QK_PAD), lambda i: (0, i, 0)),
            pl.BlockSpec((tm, KV_LORA), lambda i: (i, 0)),
            pl.BlockSpec((tm, D_ROPE), lambda i: (i, 0)),
        ],
        out_shape=[
            jax.ShapeDtypeStruct((rows, D_MODEL), BF16),
            jax.ShapeDtypeStruct((N_HEADS, rows, QK_PAD), BF16),
            jax.ShapeDtypeStruct((rows, KV_LORA), F32),
            jax.ShapeDtypeStruct((rows, D_ROPE), F32),
        ],
        compiler_params=_params(("arbitrary",)),
        name="latent",
    )(x2, w["g_norm"], w["w_in"], w["g_q_lat"], w["w_uq"], w["g_qn"], w["g_qr"], w["g_kv_lat"], w["g_kr"],
      *tabs)


def _gate_kernel(h_ref, w_ref, o_ref):
    o_ref[...] = _silu(_dot_nt(h_ref[...], w_ref[...]))


def _gate_call(h, w_all, tm, tn):
    rows = h.shape[0]
    return pl.pallas_call(
        _gate_kernel,
        grid=(D_ATT // tn, rows // tm),
        in_specs=[pl.BlockSpec((tm, D_MODEL), lambda j, i: (i, 0)),
                  _w_in_window(tn, lambda j, i: O_Z + j * tn)],
        out_specs=pl.BlockSpec((tm, tn), lambda j, i: (i, j)),
        out_shape=jax.ShapeDtypeStruct((rows, D_ATT), F32),
        compiler_params=_params(("arbitrary", "arbitrary")),
        name="gate",
    )(h, w_all)


def _pool_kernel(*refs, nb, tt, pos0, lookahead):
    if lookahead:
        h_ref, hn_ref, wu_ref, wz_ref, past_ref, wp_ref, ps_ref, o_ref, st_ref, ext_ref, b1_ref, b2_ref, p_ref, un_ref = refs
    else:
        h_ref, wu_ref, wz_ref, past_ref, wp_ref, ps_ref, o_ref, st_ref, ext_ref, b1_ref, b2_ref, p_ref = refs
    g = pl.program_id(0)
    i = pl.program_id(2)

    @pl.when(i == 0)
    def _():
        for b in range(nb):
            ext_ref[b, 0:HIST - POOL_BUF, :] = jnp.zeros((HIST - POOL_BUF, D_POOL_GROUP), F32)
            ext_ref[b, HIST - POOL_BUF:HIST, :] = past_ref[b]

    first = jnp.logical_and(pl.program_id(1) == 0, i == 0) if lookahead else i == 0

    @pl.when(first)
    def _():
        u0 = _dot_nt(h_ref[...], wu_ref[...])
        for b in range(nb):
            ext_ref[b, HIST:HIST + tt, :] = u0[b * tt:(b + 1) * tt]

    if lookahead:
        @pl.when(i > 0)
        def _():
            for b in range(nb):
                ext_ref[b, 0:HIST, :] = ext_ref[b, tt:tt + HIST, :]

        @pl.when(jnp.logical_not(first))
        def _():
            for b in range(nb):
                ext_ref[b, HIST:HIST + tt, :] = un_ref[b * tt:(b + 1) * tt, :]

    pos = pos0 + i * tt + lax.broadcasted_iota(jnp.int32, (tt, 1), 0)

    for gi, win in enumerate(POOL_WINDOWS):
        @pl.when(g == gi)
        def _(win=win):
            steps = win.bit_length() - 1
            src = ext_ref
            for k in range(steps):
                dst = (b1_ref, b2_ref)[k % 2]
                lo, sh = SUBLANES * (k + 1), 1 << k
                for b in range(nb):
                    dst[b, lo:HIST + tt, :] = src[b, lo:HIST + tt, :] + src[b, lo - sh:HIST + tt - sh, :]
                src = dst
            cnt = jnp.minimum(pos + 1, win).astype(F32)
            for b in range(nb):
                p_ref[b * tt:(b + 1) * tt, :] = (src[b, HIST:HIST + tt, :] / cnt
                                                 - ext_ref[b, HIST:HIST + tt, :]).astype(BF16)
            gate = _silu(_dot_nt(h_ref[...], wz_ref[...]))
            if lookahead:
                un_ref[...] = _dot_nt(hn_ref[...], wu_ref[...])
            ph = jnp.dot(p_ref[...], wp_ref[0], preferred_element_type=F32)
            o_ref[...] = (ph * ps_ref[...] * gate).astype(BF16)

    for b in range(nb):
        st_ref[b] = ext_ref[b, HIST + tt - POOL_BUF:HIST + tt, :]


def _pool_call(h, pool_past, w, batch, t_len, pos0, tm):
    rows = h.shape[0]
    tt = min(tm, t_len)
    nb = tm // tt
    nt = t_len // tt
    buf = pltpu.VMEM((nb, tt + HIST, D_POOL_GROUP), F32)
    lookahead = nt > 1
    h_specs = [pl.BlockSpec((tm, D_MODEL), lambda g, b, i: (b * nt + i, 0))]
    scratch = [buf, buf, buf, pltpu.VMEM((tm, D_POOL_GROUP), BF16)]
    if lookahead:
        h_specs.append(pl.BlockSpec((tm, D_MODEL), lambda g, b, i: (jnp.minimum(b * nt + i + 1, rows // tm - 1), 0)))
        scratch.append(pltpu.VMEM((tm, D_POOL_GROUP), F32))
    return pl.pallas_call(
        functools.partial(_pool_kernel, nb=nb, tt=tt, pos0=pos0, lookahead=lookahead),
        grid=(N_POOL_GROUPS, batch // nb, nt),
        in_specs=h_specs + [
            _w_in_window(D_POOL_GROUP, lambda g, b, i: O_U + g * D_POOL_GROUP),
            _w_in_window(D_POOL_GROUP, lambda g, b, i: O_ZP + g * D_POOL_GROUP),
            pl.BlockSpec((nb, POOL_BUF, D_POOL_GROUP), lambda g, b, i: (b, 0, g)),
            pl.BlockSpec((1, D_POOL_GROUP, D_POOL_GROUP), lambda g, b, i: (g, 0, 0)),
            pl.BlockSpec((1, D_POOL_GROUP), lambda g, b, i: (0, g)),
        ],
        out_specs=[
            pl.BlockSpec((tm, D_POOL_GROUP), lambda g, b, i: (b * nt + i, g)),
            pl.BlockSpec((nb, POOL_BUF, D_POOL_GROUP), lambda g, b, i: (b, 0, g)),
        ],
        out_shape=[
            jax.ShapeDtypeStruct((rows, D_POOL), BF16),
            jax.ShapeDtypeStruct((batch, POOL_BUF, D_POOL), F32),
        ],
        scratch_shapes=scratch,
        compiler_params=_params(("arbitrary", "arbitrary", "arbitrary")),
        name="pool",
    )(*([h] * len(h_specs)), w["w_in"], w["w_in"], pool_past, w["w_pool"], w["pool_scale"])


def _kv_kernel(ckv_ref, kr_ref, wk_ref, wv_ref, gkn_ref, k_ref, v_ref):
    ckv = ckv_ref[...].astype(BF16)
    kr = kr_ref[...].astype(BF16)
    tm = ckv.shape[0]
    pair = 2 * D_NOPE
    for j in range(N_HEADS // 2):
        kk = jnp.dot(ckv, wk_ref[:, j * pair:(j + 1) * pair], preferred_element_type=F32)
        for t in range(2):
            hd = 2 * j + t
            k_ref[hd, :, :D_NOPE] = _rms(kk[:, t * D_NOPE:(t + 1) * D_NOPE], gkn_ref[...]).astype(BF16)
            k_ref[hd, :, D_NOPE:D_QK] = kr
            k_ref[hd, :, D_QK:] = jnp.zeros((tm, QK_PAD - D_QK), BF16)
    vt = _dot_nt(wv_ref[...], ckv)
    ones_row = (lax.broadcasted_iota(jnp.int32, (V_ROWS - D_V, tm), 0) == 0).astype(BF16)
    for hd in range(N_HEADS):
        v_ref[0, hd, :D_V, :] = vt[hd * D_V:(hd + 1) * D_V].astype(BF16)
        v_ref[0, hd, D_V:, :] = ones_row


def _kv_call(ckv, kr, w, tm):
    rows = ckv.shape[0]
    return pl.pallas_call(
        _kv_kernel,
        grid=(rows // tm,),
        in_specs=[
            pl.BlockSpec((tm, KV_LORA), lambda i: (i, 0)),
            pl.BlockSpec((tm, D_ROPE), lambda i: (i, 0)),
            pl.BlockSpec((KV_LORA, N_HEADS * D_NOPE), lambda i: (0, 0)),
            pl.BlockSpec((N_HEADS * D_V, KV_LORA), lambda i: (0, 0)),
            pl.BlockSpec((1, D_NOPE), lambda i: (0, 0)),
        ],
        out_specs=[
            pl.BlockSpec((N_HEADS, tm, QK_PAD), lambda i: (0, i, 0)),
            pl.BlockSpec((1, N_HEADS, V_ROWS, tm), lambda i: (i, 0, 0, 0)),
        ],
        out_shape=[
            jax.ShapeDtypeStruct((N_HEADS, rows, QK_PAD), BF16),
            jax.ShapeDtypeStruct((rows // tm, N_HEADS, V_ROWS, tm), BF16),
        ],
        compiler_params=_params(("arbitrary",)),
        name="kv",
    )(ckv, kr, w["w_uk"], w["w_uv"], w["g_kn"])


def _attn_kernel(qi_ref, kj_ref, flag_ref, q_ref, k_ref, v_ref, g_ref, o_ref,
                 m_ref, acc_ref, bias_ref, sa_ref, sb_ref, sc_ref, *, tq, tk):
    n = pl.program_id(1)
    qi, kj, flag = qi_ref[n], kj_ref[n], flag_ref[n]
    masked = (flag & 1) == 1
    last = (flag & 2) == 2

    @pl.when(kj == 0)
    def _():
        m_ref[...] = jnp.full(m_ref.shape, -jnp.inf, F32)
        acc_ref[...] = jnp.zeros(acc_ref.shape, F32)

    def run_heads(scores, softmax_pv):
        bufs = (sa_ref, sb_ref, sc_ref)
        scores(0, bufs[0])
        scores(1, bufs[1])
        for hd in range(N_HEADS):
            if hd + 2 < N_HEADS:
                scores(hd + 2, bufs[(hd + 2) % 3])
            softmax_pv(hd, bufs[hd % 3])

    def full_block():
        def scores(hd, s_ref):
            s_ref[...] = _dot_nt(k_ref[hd], q_ref[hd])

        def softmax_pv(hd, s_ref):
            m_prev = m_ref[hd]
            m_new = jnp.maximum(m_prev, jnp.max(s_ref[...], axis=0, keepdims=True))
            p = jnp.exp2(s_ref[...] - m_new)
            acc_ref[hd] = jnp.exp2(m_prev - m_new) * acc_ref[hd] + jnp.dot(
                v_ref[0, hd], p.astype(BF16), preferred_element_type=F32)
            m_ref[hd] = m_new

        run_heads(scores, softmax_pv)

    def diagonal_block():
        half = tk // 2

        def scores(hd, s_ref):
            s_ref[:half, :] = _dot_nt(k_ref[hd, :half, :], q_ref[hd]) + bias_ref[:half, :]
            s_ref[half:, half:] = _dot_nt(k_ref[hd, half:, :], q_ref[hd, half:, :]) + bias_ref[half:, half:]

        def softmax_pv(hd, s_ref):
            lo, hi = slice(0, half), slice(half, tq)
            m_lo = jnp.maximum(m_ref[hd, :, lo], jnp.max(s_ref[:half, lo], axis=0, keepdims=True))
            m_hi = jnp.maximum(m_ref[hd, :, hi], jnp.maximum(jnp.max(s_ref[:half, hi], axis=0, keepdims=True),
                                                            jnp.max(s_ref[half:, hi], axis=0, keepdims=True)))
            p_lo = jnp.exp2(s_ref[:half, lo] - m_lo)
            p_hi = jnp.exp2(s_ref[:, hi] - m_hi)
            acc_ref[hd, :, lo] = jnp.exp2(m_ref[hd, :, lo] - m_lo) * acc_ref[hd, :, lo] + jnp.dot(
                v_ref[0, hd, :, :half], p_lo.astype(BF16), preferred_element_type=F32)
            acc_ref[hd, :, hi] = jnp.exp2(m_ref[hd, :, hi] - m_hi) * acc_ref[hd, :, hi] + jnp.dot(
                v_ref[0, hd], p_hi.astype(BF16), preferred_element_type=F32)
            m_ref[hd, :, lo] = m_lo
            m_ref[hd, :, hi] = m_hi

        run_heads(scores, softmax_pv)

    @pl.when(masked)
    def _():
        shift = CHUNK.bit_length() - 1
        kc = (kj * tk + lax.broadcasted_iota(jnp.int32, (tk, 1), 0)) >> shift
        qc = (qi * tq + lax.broadcasted_iota(jnp.int32, (1, tq), 1)) >> shift
        bias_ref[...] = jnp.where(kc <= qc, 0.0, -jnp.inf).astype(F32)
        diagonal_block()

    @pl.when(jnp.logical_not(masked))
    def _():
        full_block()

    @pl.when(last)
    def _():
        for hd in range(N_HEADS):
            att = (acc_ref[hd, :D_V, :] * (1.0 / acc_ref[hd, D_V:D_V + 1, :])).T
            o_ref[:, hd * D_V:(hd + 1) * D_V] = (att * g_ref[:, hd * D_V:(hd + 1) * D_V]).astype(BF16)


def _attn_schedule(t_len, tq, tk):
    qi, kj, flag = [], [], []
    for i in range(t_len // tq):
        first_chunk = (i * tq) // CHUNK
        last_blk = ((i + 1) * tq - 1) // tk
        for j in range(last_blk + 1):
            needs_mask = ((j + 1) * tk - 1) // CHUNK > first_chunk
            qi.append(i)
            kj.append(j)
            flag.append(int(needs_mask) + 2 * int(j == last_blk))
    return tuple(np.asarray(a, np.int32) for a in (qi, kj, flag))


def _attn_call(q, k, vt, gate, batch, t_len, tq, tk):
    assert tq == tk and tq % (2 * CHUNK) == 0 and tq % (2 * LANES) == 0 and t_len % tq == 0
    rows = batch * t_len
    nq, nk = t_len // tq, t_len // tk
    qi, kj, flag = _attn_schedule(t_len, tq, tk)
    row_map = lambda b, n, qi, kj, fl: (b * nq + qi[n], 0)
    grid_spec = pltpu.PrefetchScalarGridSpec(
        num_scalar_prefetch=3,
        grid=(batch, len(qi)),
        in_specs=[
            pl.BlockSpec((N_HEADS, tq, QK_PAD), lambda b, n, qi, kj, fl: (0, b * nq + qi[n], 0)),
            pl.BlockSpec((N_HEADS, tk, QK_PAD), lambda b, n, qi, kj, fl: (0, b * nk + kj[n], 0)),
            pl.BlockSpec((1, N_HEADS, V_ROWS, tk), lambda b, n, qi, kj, fl: (b * nk + kj[n], 0, 0, 0)),
            pl.BlockSpec((tq, D_ATT), row_map),
        ],
        out_specs=pl.BlockSpec((tq, D_ATT), row_map),
        scratch_shapes=[
            pltpu.VMEM((N_HEADS, 1, tq), F32),
            pltpu.VMEM((N_HEADS, V_ROWS, tq), F32),
            pltpu.VMEM((tk, tq), F32),
            pltpu.VMEM((tk, tq), F32),
            pltpu.VMEM((tk, tq), F32),
            pltpu.VMEM((tk, tq), F32),
        ],
    )
    return pl.pallas_call(
        functools.partial(_attn_kernel, tq=tq, tk=tk),
        grid_spec=grid_spec,
        out_shape=jax.ShapeDtypeStruct((rows, D_ATT), BF16),
        compiler_params=_params(("arbitrary", "arbitrary")),
        name="attn",
    )(jnp.asarray(qi), jnp.asarray(kj), jnp.asarray(flag), q, k, vt, gate)


def _cache_attn_kernel(q_ref, ckv_ref, krt_ref, ckvn_ref, krn_ref, wk_ref, wv_ref, gkn_ref, g_ref, o_ref,
                       lhs_ref, qr_ref, s_ref, s2_ref, sn_ref, m_ref, l_ref, acc_ref, *, t_len, sub):
    kj = pl.program_id(1)
    n_w = N_HEADS * D_NOPE

    @pl.when(kj == 0)
    def _():
        lhs_ref[0:n_w, :] = wk_ref[...]
        for hd in range(N_HEADS):
            qn = (q_ref[hd, :, :D_NOPE].astype(F32) * gkn_ref[...]).astype(BF16)
            lhs_ref[n_w + hd * t_len:n_w + (hd + 1) * t_len, :] = jnp.dot(
                qn, wk_ref[hd * D_NOPE:(hd + 1) * D_NOPE, :], preferred_element_type=F32).astype(BF16)
            qr_ref[hd * t_len:(hd + 1) * t_len, :] = q_ref[hd, :, D_NOPE:D_QK]
        m_ref[...] = jnp.full(m_ref.shape, -jnp.inf, F32)
        l_ref[...] = jnp.zeros(l_ref.shape, F32)
        acc_ref[...] = jnp.zeros(acc_ref.shape, F32)

    def scores(ckv, s_rope, sc_ref):
        both = _dot_nt(lhs_ref[...], ckv)
        for hd in range(N_HEADS):
            kpre = both[hd * D_NOPE:(hd + 1) * D_NOPE]
            r = lax.rsqrt(jnp.sum(kpre * kpre, axis=0, keepdims=True) * (1.0 / D_NOPE) + EPS)
            rows = slice(hd * t_len, (hd + 1) * t_len)
            sc_ref[rows, :] = both[n_w + hd * t_len:n_w + (hd + 1) * t_len] * r + s_rope[rows]

    def softmax_pv(ckv, sc_ref):
        sc = sc_ref[...]
        m_prev = m_ref[...]
        m_new = jnp.maximum(m_prev, jnp.max(sc, axis=1, keepdims=True))
        alpha = jnp.exp2(m_prev - m_new)
        p = jnp.exp2(sc - m_new)
        l_ref[...] = alpha * l_ref[...] + jnp.sum(p, axis=1, keepdims=True)
        acc_ref[...] = alpha * acc_ref[...] + jnp.dot(p.astype(BF16), ckv, preferred_element_type=F32)
        m_ref[...] = m_new

    def sub_ckv(c):
        return ckv_ref[c * sub:(c + 1) * sub, :].astype(BF16)

    def sub_scores(c):
        s_rope = jnp.dot(qr_ref[...], krt_ref[0, :, c * sub:(c + 1) * sub].astype(BF16), preferred_element_type=F32)
        scores(sub_ckv(c), s_rope, (s_ref, s2_ref)[c % 2])

    n_sub = ckv_ref.shape[0] // sub
    sub_scores(0)
    for c in range(n_sub):
        if c + 1 < n_sub:
            sub_scores(c + 1)
        softmax_pv(sub_ckv(c), (s_ref, s2_ref)[c % 2])

    @pl.when(kj == pl.num_programs(1) - 1)
    def _():
        ckv_new = ckvn_ref[...].astype(BF16)
        scores(ckv_new, _dot_nt(qr_ref[...], krn_ref[...].astype(BF16)), sn_ref)
        softmax_pv(ckv_new, sn_ref)
        ctx = (acc_ref[...] * (1.0 / l_ref[...])).astype(BF16)
        for hd in range(N_HEADS):
            att = _dot_nt(ctx[hd * t_len:(hd + 1) * t_len], wv_ref[hd * D_V:(hd + 1) * D_V, :])
            o_ref[:, hd * D_V:(hd + 1) * D_V] = (att * g_ref[:, hd * D_V:(hd + 1) * D_V]).astype(BF16)


def _cache_attn_call(q, ckv_past, krt_past, ckv_new, kr_new, gate, w, batch, t_len, n_past, tk, sub):
    assert n_past % CHUNK == 0 and t_len <= CHUNK and n_past % tk == 0 and tk % sub == 0
    rows = batch * t_len
    nk = n_past // tk
    n_q = N_HEADS * t_len
    const = lambda shape: pl.BlockSpec(shape, lambda b, j: (0,) * len(shape))
    return pl.pallas_call(
        functools.partial(_cache_attn_kernel, t_len=t_len, sub=sub),
        grid=(batch, nk),
        in_specs=[
            pl.BlockSpec((N_HEADS, t_len, QK_PAD), lambda b, j: (0, b, 0)),
            pl.BlockSpec((tk, KV_LORA), lambda b, j: (b * nk + j, 0)),
            pl.BlockSpec((1, D_ROPE, tk), lambda b, j: (b, 0, j)),
            pl.BlockSpec((t_len, KV_LORA), lambda b, j: (b, 0)),
            pl.BlockSpec((t_len, D_ROPE), lambda b, j: (b, 0)),
            const((N_HEADS * D_NOPE, KV_LORA)),
            const((N_HEADS * D_V, KV_LORA)),
            const((1, D_NOPE)),
            pl.BlockSpec((t_len, D_ATT), lambda b, j: (b, 0)),
        ],
        out_specs=pl.BlockSpec((t_len, D_ATT), lambda b, j: (b, 0)),
        out_shape=jax.ShapeDtypeStruct((rows, D_ATT), BF16),
        scratch_shapes=[
            pltpu.VMEM((N_HEADS * D_NOPE + n_q, KV_LORA), BF16),
            pltpu.VMEM((n_q, D_ROPE), BF16),
            pltpu.VMEM((n_q, sub), F32),
            pltpu.VMEM((n_q, sub), F32),
            pltpu.VMEM((n_q, t_len), F32),
            pltpu.VMEM((n_q, 1), F32),
            pltpu.VMEM((n_q, 1), F32),
            pltpu.VMEM((n_q, KV_LORA), F32),
        ],
        compiler_params=_params(("arbitrary", "arbitrary")),
        name="cache_attn",
    )(q, ckv_past, krt_past, ckv_new, kr_new, w["w_uk_t"], w["w_uv"], w["g_kn"], gate)


def _out_kernel(a_ref, p_ref, w_ref, x_ref, y_ref):
    y_ref[...] = x_ref[...] + (jnp.dot(a_ref[...], w_ref[0], preferred_element_type=F32)
                               + jnp.dot(p_ref[...], w_ref[1], preferred_element_type=F32))


def _out_call(att, pool, w_out, x2, tm, tn):
    rows = x2.shape[0]
    return pl.pallas_call(
        _out_kernel,
        grid=(D_MODEL // tn, rows // tm),
        in_specs=[
            pl.BlockSpec((tm, D_ATT), lambda j, i: (i, 0)),
            pl.BlockSpec((tm, D_POOL), lambda j, i: (i, 0)),
            pl.BlockSpec((2, D_ATT, tn), lambda j, i: (0, 0, j)),
            pl.BlockSpec((tm, tn), lambda j, i: (i, j)),
        ],
        out_specs=pl.BlockSpec((tm, tn), lambda j, i: (i, j)),
        out_shape=jax.ShapeDtypeStruct((rows, D_MODEL), F32),
        compiler_params=_params(("arbitrary", "arbitrary")),
        name="out_proj",
    )(att, pool, w_out, x2)


def _rope_tables(pos0, t_len):
    half = D_ROPE // 2
    freqs = ROPE_THETA ** (-jnp.arange(half, dtype=F32) / half)
    ang = (pos0 + jnp.arange(t_len, dtype=jnp.int32)).astype(F32)[:, None] * freqs[None, :]
    cos, sin = jnp.cos(ang), jnp.sin(ang)
    reps = ROPE_PAD // half
    cos, sin = jnp.tile(cos, (1, reps)), jnp.tile(sin, (1, reps))
    is_x1 = (jnp.arange(ROPE_PAD) // half) % 2 == 0
    return cos, jnp.where(is_x1, 0.0, sin), jnp.where(is_x1, -sin, 0.0)


def _prep_weights(g_norm, w_in, g_q_lat, w_uq, g_qn, g_qr, g_kv_lat, g_kr, w_ukv, g_kn,
                  w_pool, pool_scale, w_out):
    row = lambda v: v.reshape(1, -1).astype(F32)
    pad_row = lambda v: jnp.pad(v, (0, ROPE_PAD - D_ROPE)).reshape(1, -1).astype(F32)
    return {
        "g_norm": row(g_norm), "g_q_lat": row(g_q_lat), "g_qn": row(g_qn), "g_qr": row(jnp.tile(g_qr, ROPE_PAD // D_ROPE)),
        "g_kv_lat": row(g_kv_lat), "g_kr": pad_row(g_kr), "g_kn": row(g_kn), "pool_scale": row(pool_scale),
        "w_in": _prep_in_call(jnp.transpose(w_in), TILES.prep_rows),
        "w_uq": _prep_uq_call(jnp.transpose(w_uq, (1, 2, 0))),
        "w_uk": w_ukv[:, :, :D_NOPE].reshape(KV_LORA, N_HEADS * D_NOPE).astype(BF16),
        "w_uk_t": jnp.transpose(w_ukv[:, :, :D_NOPE], (1, 2, 0)).reshape(N_HEADS * D_NOPE, KV_LORA).astype(BF16),
        "w_uv": jnp.transpose(w_ukv[:, :, D_NOPE:], (1, 2, 0)).reshape(N_HEADS * D_V, KV_LORA).astype(BF16),
        "w_pool": w_pool.astype(BF16),
        "w_out": w_out.reshape(2, D_ATT, D_MODEL).astype(BF16),
    }


def _mixer_layer(x, ckv_past, kr_past, pool_past, w):
    batch, t_len, _ = x.shape
    rows = batch * t_len
    pos0 = 0 if ckv_past is None else ckv_past.shape[1]
    x2 = x.reshape(rows, D_MODEL)

    h, q, ckv, kr = _latent_call(x2, _rope_tables(pos0, t_len), w, t_len, TILES.latent_rows)
    gate = _gate_call(h, w["w_in"], TILES.gate_rows, TILES.gate_cols)
    pool, pool_state = _pool_call(h, pool_past, w, batch, t_len, pos0, TILES.pool_rows)
    if ckv_past is None:
        k, vt = _kv_call(ckv, kr, w, TILES.attn_block)
        att = _attn_call(q, k, vt, gate, batch, t_len, TILES.attn_block, TILES.attn_block)
    else:
        att = _cache_attn_call(q, ckv_past.reshape(batch * pos0, KV_LORA), jnp.swapaxes(kr_past, 1, 2), ckv, kr,
                               gate, w, batch, t_len, pos0, TILES.cache_keys, TILES.cache_sub)
    y = _out_call(att, pool, w["w_out"], x2, TILES.out_rows, TILES.out_cols)
    return (y.reshape(batch, t_len, D_MODEL), ckv.reshape(batch, t_len, KV_LORA),
            kr.reshape(batch, t_len, D_ROPE), pool_state)


def kernel(x_prompt, x_sample, cache_ckv, cache_krope, state_pool, g_norm, w_in, g_q_lat, w_uq, g_qn, g_qr,
           g_kv_lat, g_kr, w_ukv, g_kn, w_pool, pool_scale, w_out):
    depth = w_in.shape[0]
    batch = x_prompt.shape[0]
    yp, ys = x_prompt, x_sample
    outs = [[] for _ in range(6)]
    for l in range(depth):
        w = _prep_weights(g_norm[l], w_in[l], g_q_lat[l], w_uq[l], g_qn[l], g_qr[l], g_kv_lat[l], g_kr[l],
                          w_ukv[l], g_kn[l], w_pool[l], pool_scale[l], w_out[l])
        yp, a, b, c = _mixer_layer(yp, None, None, jnp.zeros((batch, POOL_BUF, D_POOL), F32), w)
        outs[0].append(a); outs[1].append(b); outs[2].append(c)
        ys, a, b, c = _mixer_layer(ys, cache_ckv[l], cache_krope[l], state_pool[l], w)
        outs[3].append(a); outs[4].append(b); outs[5].append(c)
    return (yp, ys) + tuple(jnp.stack(o) for o in outs)
```

```python
import functools
from typing import NamedTuple

import numpy as np
import jax
import jax.numpy as jnp
from jax import lax
from jax.experimental import pallas as pl
from jax.experimental.pallas import tpu as pltpu

D_MODEL = 4096
CHUNK = 64
N_HEADS = 16
D_NOPE = 128
D_ROPE = 64
D_QK = D_NOPE + D_ROPE
D_V = 128
D_ATT = N_HEADS * D_V
Q_LORA = 1024
KV_LORA = 512
D_POOL = 2048
POOL_WINDOWS = (2, 4, 8, 16)
N_POOL_GROUPS = len(POOL_WINDOWS)
D_POOL_GROUP = D_POOL // N_POOL_GROUPS
POOL_BUF = max(POOL_WINDOWS) - 1
ROPE_THETA = 10000.0
EPS = 1e-6

O_Z = Q_LORA + KV_LORA + D_ROPE
O_U = O_Z + D_ATT
O_ZP = O_U + D_POOL
D_IN = O_ZP + D_POOL

LANES = 128
SUBLANES = 8
BF16_SUBLANES = 2 * SUBLANES
ROPE_PAD = LANES
QK_PAD = D_NOPE + ROPE_PAD
LAT_PAD = Q_LORA + KV_LORA + ROPE_PAD
HEAD_GROUP = 4
V_ROWS = D_V + BF16_SUBLANES
HIST = 2 * (POOL_BUF + 1)
VMEM_LIMIT = 56 * 1024 * 1024

W_IN_ALIGN = 64

F32 = jnp.float32
BF16 = jnp.bfloat16


class _Tiles(NamedTuple):
    prep_rows: int = D_IN // 22
    latent_rows: int = 256
    gate_rows: int = 1024
    gate_cols: int = 1024
    pool_rows: int = 512
    attn_block: int = 512
    cache_keys: int = 2048
    cache_sub: int = 1024
    out_rows: int = 1024
    out_cols: int = 1024


TILES = _Tiles()
_NT = (((1,), (1,)), ((), ()))


def _params(semantics):
    return pltpu.CompilerParams(dimension_semantics=semantics, vmem_limit_bytes=VMEM_LIMIT)


def _rms(x, g, n=None):
    n = x.shape[-1] if n is None else n
    ms = jnp.sum(x * x, axis=-1, keepdims=True) * (1.0 / n)
    return x * lax.rsqrt(ms + EPS) * g


def _rope(x, cos, sin_hi, sin_lo):
    return x * cos + pltpu.roll(x, D_ROPE // 2, 1) * sin_hi + pltpu.roll(x, ROPE_PAD - D_ROPE // 2, 1) * sin_lo


def _silu(z):
    return z * (1.0 / (1.0 + jnp.exp(-z)))


def _dot_nt(a, b):
    return lax.dot_general(a, b, _NT, preferred_element_type=F32)


def _cast_kernel(w_ref, o_ref):
    o_ref[...] = w_ref[...].astype(BF16)


def _prep_in_call(w_in_t, tr):
    return pl.pallas_call(
        _cast_kernel,
        grid=(D_IN // tr,),
        in_specs=[pl.BlockSpec((tr, D_MODEL), lambda n: (n, 0))],
        out_specs=pl.BlockSpec((tr, D_MODEL), lambda n: (n, 0)),
        out_shape=jax.ShapeDtypeStruct((D_IN, D_MODEL), BF16),
        compiler_params=_params(("arbitrary",)),
        name="prep_w_in",
    )(w_in_t)


def _w_in_window(n_rows, row_of):
    return pl.BlockSpec((pl.Element(n_rows), pl.Element(D_MODEL)),
                        lambda *idx: (pl.multiple_of(row_of(*idx), W_IN_ALIGN), 0))


def _prep_uq_kernel(w_ref, o_ref):
    rope0 = HEAD_GROUP * D_NOPE
    for j in range(HEAD_GROUP):
        o_ref[0, j * D_NOPE:(j + 1) * D_NOPE, :] = w_ref[j, :D_NOPE, :].astype(BF16)
        o_ref[0, rope0 + j * D_ROPE:rope0 + (j + 1) * D_ROPE, :] = w_ref[j, D_NOPE:, :].astype(BF16)


def _prep_uq_call(w_uq_t):
    groups = N_HEADS // HEAD_GROUP
    return pl.pallas_call(
        _prep_uq_kernel,
        grid=(groups,),
        in_specs=[pl.BlockSpec((HEAD_GROUP, D_QK, Q_LORA), lambda g: (g, 0, 0))],
        out_specs=pl.BlockSpec((1, HEAD_GROUP * D_QK, Q_LORA), lambda g: (g, 0, 0)),
        out_shape=jax.ShapeDtypeStruct((groups, HEAD_GROUP * D_QK, Q_LORA), BF16),
        compiler_params=_params(("arbitrary",)),
        name="prep_w_uq",
    )(w_uq_t)


def _latent_kernel(x_ref, gn_ref, wlat_ref, gql_ref, wuq_ref, gqn_ref, gqr_ref, gkv_ref, gkr_ref,
                   cos_ref, shi_ref, slo_ref, h_ref, q_ref, ckv_ref, kr_ref, *, scale):
    h = _rms(x_ref[...], gn_ref[...]).astype(BF16)
    h_ref[...] = h
    c = _dot_nt(h, wlat_ref[...])
    qlat = _rms(c[:, :Q_LORA], gql_ref[...]).astype(BF16)
    ckv_ref[...] = _rms(c[:, Q_LORA:Q_LORA + KV_LORA], gkv_ref[...])
    cos, shi, slo = cos_ref[...], shi_ref[...], slo_ref[...]
    c_kr = c[:, Q_LORA + KV_LORA:]
    c_kr = jnp.where(lax.broadcasted_iota(jnp.int32, c_kr.shape, 1) < D_ROPE, c_kr, 0.0)
    kr = _rope(_rms(c_kr, gkr_ref[...], D_ROPE), cos, shi, slo)
    kr_ref[...] = kr[:, :D_ROPE]
    low = lax.broadcasted_iota(jnp.int32, c_kr.shape, 1) < D_ROPE
    rope0 = HEAD_GROUP * D_NOPE
    for g in range(N_HEADS // HEAD_GROUP):
        q = _dot_nt(qlat, wuq_ref[g])
        for j in range(HEAD_GROUP):
            qn = _rms(q[:, j * D_NOPE:(j + 1) * D_NOPE], gqn_ref[...]) * scale
            q_ref[g * HEAD_GROUP + j, :, :D_NOPE] = qn.astype(BF16)
        for t in range(HEAD_GROUP // 2):
            x = q[:, rope0 + t * ROPE_PAD:rope0 + (t + 1) * ROPE_PAD]
            sq = x * x
            ms_lo = jnp.sum(jnp.where(low, sq, 0.0), axis=-1, keepdims=True) * (1.0 / D_ROPE)
            ms_hi = jnp.sum(jnp.where(low, 0.0, sq), axis=-1, keepdims=True) * (1.0 / D_ROPE)
            r = jnp.where(low, lax.rsqrt(ms_lo + EPS), lax.rsqrt(ms_hi + EPS))
            y = _rope(x * r * gqr_ref[...], cos, shi, slo) * scale
            hd = g * HEAD_GROUP + 2 * t
            q_ref[hd, :, D_NOPE:] = jnp.where(low, y, 0.0).astype(BF16)
            q_ref[hd + 1, :, D_NOPE:] = jnp.where(low, pltpu.roll(y, D_ROPE, 1), 0.0).astype(BF16)


def _latent_call(x2, tabs, w, t_len, tm):
    rows = x2.shape[0]
    if tm > t_len:
        tabs = [jnp.tile(t, (tm // t_len, 1)) for t in tabs]
        t_len = tm
    nt = t_len // tm
    resident = lambda shape: pl.BlockSpec(shape, lambda i: (0,) * len(shape), pipeline_mode=pl.Buffered(1))
    tab_spec = pl.BlockSpec((tm, ROPE_PAD), lambda i: (i % nt, 0))
    return pl.pallas_call(
        functools.partial(_latent_kernel, scale=float(D_QK ** -0.5 * np.log2(np.e))),
        grid=(rows // tm,),
        in_specs=[
            pl.BlockSpec((tm, D_MODEL), lambda i: (i, 0)),
            resident((1, D_MODEL)),
            resident((LAT_PAD, D_MODEL)),
            resident((1, Q_LORA)),
            resident((N_HEADS // HEAD_GROUP, HEAD_GROUP * D_QK, Q_LORA)),
            resident((1, D_NOPE)),
            resident((1, ROPE_PAD)),
            resident((1, KV_LORA)),
            resident((1, ROPE_PAD)),
            tab_spec, tab_spec, tab_spec,
        ],
        out_specs=[
            pl.BlockSpec((tm, D_MODEL), lambda i: (i, 0)),
            pl.BlockSpec((N_HEADS, tm, QK_PAD), lambda i: (0, i, 0)),
            pl.BlockSpec((tm, KV_LORA), lambda i: (i, 0)),
            pl.BlockSpec((tm, D_ROPE), lambda i: (i, 0)),
        ],
        out_shape=[
            jax.ShapeDtypeStruct((rows, D_MODEL), BF16),
            jax.ShapeDtypeStruct((N_HEADS, rows, QK_PAD), BF16),
            jax.ShapeDtypeStruct((rows, KV_LORA), F32),
            jax.ShapeDtypeStruct((rows, D_ROPE), F32),
        ],
        compiler_params=_params(("arbitrary",)),
        name="latent",
    )(x2, w["g_norm"], w["w_in"], w["g_q_lat"], w["w_uq"], w["g_qn"], w["g_qr"], w["g_kv_lat"], w["g_kr"],
      *tabs)


def _gate_kernel(h_ref, w_ref, o_ref):
    o_ref[...] = _silu(_dot_nt(h_ref[...], w_ref[...]))


def _gate_call(h, w_all, tm, tn):
    rows = h.shape[0]
    return pl.pallas_call(
        _gate_kernel,
        grid=(D_ATT // tn, rows // tm),
        in_specs=[pl.BlockSpec((tm, D_MODEL), lambda j, i: (i, 0)),
                  _w_in_window(tn, lambda j, i: O_Z + j * tn)],
        out_specs=pl.BlockSpec((tm, tn), lambda j, i: (i, j)),
        out_shape=jax.ShapeDtypeStruct((rows, D_ATT), F32),
        compiler_params=_params(("arbitrary", "arbitrary")),
        name="gate",
    )(h, w_all)


def _pool_kernel(*refs, nb, tt, pos0, lookahead):
    if lookahead:
        h_ref, hn_ref, wu_ref, wz_ref, past_ref, wp_ref, ps_ref, o_ref, st_ref, ext_ref, b1_ref, b2_ref, p_ref, un_ref = refs
    else:
        h_ref, wu_ref, wz_ref, past_ref, wp_ref, ps_ref, o_ref, st_ref, ext_ref, b1_ref, b2_ref, p_ref = refs
    g = pl.program_id(0)
    i = pl.program_id(2)

    @pl.when(i == 0)
    def _():
        for b in range(nb):
            ext_ref[b, 0:HIST - POOL_BUF, :] = jnp.zeros((HIST - POOL_BUF, D_POOL_GROUP), F32)
            ext_ref[b, HIST - POOL_BUF:HIST, :] = past_ref[b]

    first = jnp.logical_and(pl.program_id(1) == 0, i == 0) if lookahead else i == 0

    @pl.when(first)
    def _():
        u0 = _dot_nt(h_ref[...], wu_ref[...])
        for b in range(nb):
            ext_ref[b, HIST:HIST + tt, :] = u0[b * tt:(b + 1) * tt]

    if lookahead:
        @pl.when(i > 0)
        def _():
            for b in range(nb):
                ext_ref[b, 0:HIST, :] = ext_ref[b, tt:tt + HIST, :]

        @pl.when(jnp.logical_not(first))
        def _():
            for b in range(nb):
                ext_ref[b, HIST:HIST + tt, :] = un_ref[b * tt:(b + 1) * tt, :]

    pos = pos0 + i * tt + lax.broadcasted_iota(jnp.int32, (tt, 1), 0)

    for gi, win in enumerate(POOL_WINDOWS):
        @pl.when(g == gi)
        def _(win=win):
            steps = win.bit_length() - 1
            src = ext_ref
            for k in range(steps):
                dst = (b1_ref, b2_ref)[k % 2]
                lo, sh = SUBLANES * (k + 1), 1 << k
                for b in range(nb):
                    dst[b, lo:HIST + tt, :] = src[b, lo:HIST + tt, :] + src[b, lo - sh:HIST + tt - sh, :]
                src = dst
            cnt = jnp.minimum(pos + 1, win).astype(F32)
            for b in range(nb):
                p_ref[b * tt:(b + 1) * tt, :] = (src[b, HIST:HIST + tt, :] / cnt
                                                 - ext_ref[b, HIST:HIST + tt, :]).astype(BF16)
            gate = _silu(_dot_nt(h_ref[...], wz_ref[...]))
            if lookahead:
                un_ref[...] = _dot_nt(hn_ref[...], wu_ref[...])
            ph = jnp.dot(p_ref[...], wp_ref[0], preferred_element_type=F32)
            o_ref[...] = (ph * ps_ref[...] * gate).astype(BF16)

    for b in range(nb):
        st_ref[b] = ext_ref[b, HIST + tt - POOL_BUF:HIST + tt, :]


def _pool_call(h, pool_past, w, batch, t_len, pos0, tm):
    rows = h.shape[0]
    tt = min(tm, t_len)
    nb = tm // tt
    nt = t_len // tt
    buf = pltpu.VMEM((nb, tt + HIST, D_POOL_GROUP), F32)
    lookahead = nt > 1
    h_specs = [pl.BlockSpec((tm, D_MODEL), lambda g, b, i: (b * nt + i, 0))]
    scratch = [buf, buf, buf, pltpu.VMEM((tm, D_POOL_GROUP), BF16)]
    if lookahead:
        h_specs.append(pl.BlockSpec((tm, D_MODEL), lambda g, b, i: (jnp.minimum(b * nt + i + 1, rows // tm - 1), 0)))
        scratch.append(pltpu.VMEM((tm, D_POOL_GROUP), F32))
    return pl.pallas_call(
        functools.partial(_pool_kernel, nb=nb, tt=tt, pos0=pos0, lookahead=lookahead),
        grid=(N_POOL_GROUPS, batch // nb, nt),
        in_specs=h_specs + [
            _w_in_window(D_POOL_GROUP, lambda g, b, i: O_U + g * D_POOL_GROUP),
            _w_in_window(D_POOL_GROUP, lambda g, b, i: O_ZP + g * D_POOL_GROUP),
            pl.BlockSpec((nb, POOL_BUF, D_POOL_GROUP), lambda g, b, i: (b, 0, g)),
            pl.BlockSpec((1, D_POOL_GROUP, D_POOL_GROUP), lambda g, b, i: (g, 0, 0)),
            pl.BlockSpec((1, D_POOL_GROUP), lambda g, b, i: (0, g)),
        ],
        out_specs=[
            pl.BlockSpec((tm, D_POOL_GROUP), lambda g, b, i: (b * nt + i, g)),
            pl.BlockSpec((nb, POOL_BUF, D_POOL_GROUP), lambda g, b, i: (b, 0, g)),
        ],
        out_shape=[
            jax.ShapeDtypeStruct((rows, D_POOL), BF16),
            jax.ShapeDtypeStruct((batch, POOL_BUF, D_POOL), F32),
        ],
        scratch_shapes=scratch,
        compiler_params=_params(("arbitrary", "arbitrary", "arbitrary")),
        name="pool",
    )(*([h] * len(h_specs)), w["w_in"], w["w_in"], pool_past, w["w_pool"], w["pool_scale"])


def _kv_kernel(ckv_ref, kr_ref, wk_ref, wv_ref, gkn_ref, k_ref, v_ref):
    ckv = ckv_ref[...].astype(BF16)
    kr = kr_ref[...].astype(BF16)
    tm = ckv.shape[0]
    pair = 2 * D_NOPE
    for j in range(N_HEADS // 2):
        kk = jnp.dot(ckv, wk_ref[:, j * pair:(j + 1) * pair], preferred_element_type=F32)
        for t in range(2):
            hd = 2 * j + t
            k_ref[hd, :, :D_NOPE] = _rms(kk[:, t * D_NOPE:(t + 1) * D_NOPE], gkn_ref[...]).astype(BF16)
            k_ref[hd, :, D_NOPE:D_QK] = kr
            k_ref[hd, :, D_QK:] = jnp.zeros((tm, QK_PAD - D_QK), BF16)
    vt = _dot_nt(wv_ref[...], ckv)
    ones_row = (lax.broadcasted_iota(jnp.int32, (V_ROWS - D_V, tm), 0) == 0).astype(BF16)
    for hd in range(N_HEADS):
        v_ref[0, hd, :D_V, :] = vt[hd * D_V:(hd + 1) * D_V].astype(BF16)
        v_ref[0, hd, D_V:, :] = ones_row


def _kv_call(ckv, kr, w, tm):
    rows = ckv.shape[0]
    return pl.pallas_call(
        _kv_kernel,
        grid=(rows // tm,),
        in_specs=[
            pl.BlockSpec((tm, KV_LORA), lambda i: (i, 0)),
            pl.BlockSpec((tm, D_ROPE), lambda i: (i, 0)),
            pl.BlockSpec((KV_LORA, N_HEADS * D_NOPE), lambda i: (0, 0)),
            pl.BlockSpec((N_HEADS * D_V, KV_LORA), lambda i: (0, 0)),
            pl.BlockSpec((1, D_NOPE), lambda i: (0, 0)),
        ],
        out_specs=[
            pl.BlockSpec((N_HEADS, tm, QK_PAD), lambda i: (0, i, 0)),
            pl.BlockSpec((1, N_HEADS, V_ROWS, tm), lambda i: (i, 0, 0, 0)),
        ],
        out_shape=[
            jax.ShapeDtypeStruct((N_HEADS, rows, QK_PAD), BF16),
            jax.ShapeDtypeStruct((rows // tm, N_HEADS, V_ROWS, tm), BF16),
        ],
        compiler_params=_params(("arbitrary",)),
        name="kv",
    )(ckv, kr, w["w_uk"], w["w_uv"], w["g_kn"])


def _attn_kernel(qi_ref, kj_ref, flag_ref, q_ref, k_ref, v_ref, g_ref, o_ref,
                 m_ref, acc_ref, bias_ref, sa_ref, sb_ref, sc_ref, *, tq, tk):
    n = pl.program_id(1)
    qi, kj, flag = qi_ref[n], kj_ref[n], flag_ref[n]
    masked = (flag & 1) == 1
    last = (flag & 2) == 2

    @pl.when(kj == 0)
    def _():
        m_ref[...] = jnp.full(m_ref.shape, -jnp.inf, F32)
        acc_ref[...] = jnp.zeros(acc_ref.shape, F32)

    def run_heads(scores, softmax_pv):
        bufs = (sa_ref, sb_ref, sc_ref)
        scores(0, bufs[0])
        scores(1, bufs[1])
        for hd in range(N_HEADS):
            if hd + 2 < N_HEADS:
                scores(hd + 2, bufs[(hd + 2) % 3])
            softmax_pv(hd, bufs[hd % 3])

    def full_block():
        halves = (slice(0, tq // 2), slice(tq // 2, tq))

        def scores(hd, s_ref):
            for h in halves:
                s_ref[:, h] = _dot_nt(k_ref[hd], q_ref[hd, h, :])

        def softmax_pv(hd, s_ref):
            for h in halves:
                m_prev = m_ref[hd, :, h]
                m_new = jnp.maximum(m_prev, jnp.max(s_ref[:, h], axis=0, keepdims=True))
                p = jnp.exp2(s_ref[:, h] - m_new)
                acc_ref[hd, :, h] = jnp.exp2(m_prev - m_new) * acc_ref[hd, :, h] + jnp.dot(
                    v_ref[0, hd], p.astype(BF16), preferred_element_type=F32)
                m_ref[hd, :, h] = m_new

        run_heads(scores, softmax_pv)

    def diagonal_block():
        half = tk // 2

        def scores(hd, s_ref):
            s_ref[:half, :] = _dot_nt(k_ref[hd, :half, :], q_ref[hd]) + bias_ref[:half, :]
            s_ref[half:, half:] = _dot_nt(k_ref[hd, half:, :], q_ref[hd, half:, :]) + bias_ref[half:, half:]

        def softmax_pv(hd, s_ref):
            lo, hi = slice(0, half), slice(half, tq)
            m_lo = jnp.maximum(m_ref[hd, :, lo], jnp.max(s_ref[:half, lo], axis=0, keepdims=True))
            m_hi = jnp.maximum(m_ref[hd, :, hi], jnp.maximum(jnp.max(s_ref[:half, hi], axis=0, keepdims=True),
                                                            jnp.max(s_ref[half:, hi], axis=0, keepdims=True)))
            p_lo = jnp.exp2(s_ref[:half, lo] - m_lo)
            p_hi = jnp.exp2(s_ref[:, hi] - m_hi)
            acc_ref[hd, :, lo] = jnp.exp2(m_ref[hd, :, lo] - m_lo) * acc_ref[hd, :, lo] + jnp.dot(
                v_ref[0, hd, :, :half], p_lo.astype(BF16), preferred_element_type=F32)
            acc_ref[hd, :, hi] = jnp.exp2(m_ref[hd, :, hi] - m_hi) * acc_ref[hd, :, hi] + jnp.dot(
                v_ref[0, hd], p_hi.astype(BF16), preferred_element_type=F32)
            m_ref[hd, :, lo] = m_lo
            m_ref[hd, :, hi] = m_hi

        run_heads(scores, softmax_pv)

    @pl.when(masked)
    def _():
        shift = CHUNK.bit_length() - 1
        kc = (kj * tk + lax.broadcasted_iota(jnp.int32, (tk, 1), 0)) >> shift
        qc = (qi * tq + lax.broadcasted_iota(jnp.int32, (1, tq), 1)) >> shift
        bias_ref[...] = jnp.where(kc <= qc, 0.0, -jnp.inf).astype(F32)
        diagonal_block()

    @pl.when(jnp.logical_not(masked))
    def _():
        full_block()

    @pl.when(last)
    def _():
        for hd in range(N_HEADS):
            att = (acc_ref[hd, :D_V, :] * (1.0 / acc_ref[hd, D_V:D_V + 1, :])).T
            o_ref[:, hd * D_V:(hd + 1) * D_V] = (att * g_ref[:, hd * D_V:(hd + 1) * D_V]).astype(BF16)


def _attn_schedule(t_len, tq, tk):
    qi, kj, flag = [], [], []
    for i in range(t_len // tq):
        first_chunk = (i * tq) // CHUNK
        last_blk = ((i + 1) * tq - 1) // tk
        for j in range(last_blk + 1):
            needs_mask = ((j + 1) * tk - 1) // CHUNK > first_chunk
            qi.append(i)
            kj.append(j)
            flag.append(int(needs_mask) + 2 * int(j == last_blk))
    return tuple(np.asarray(a, np.int32) for a in (qi, kj, flag))


def _attn_call(q, k, vt, gate, batch, t_len, tq, tk):
    assert tq == tk and tq % (2 * CHUNK) == 0 and tq % (2 * LANES) == 0 and t_len % tq == 0
    rows = batch * t_len
    nq, nk = t_len // tq, t_len // tk
    qi, kj, flag = _attn_schedule(t_len, tq, tk)
    row_map = lambda b, n, qi, kj, fl: (b * nq + qi[n], 0)
    grid_spec = pltpu.PrefetchScalarGridSpec(
        num_scalar_prefetch=3,
        grid=(batch, len(qi)),
        in_specs=[
            pl.BlockSpec((N_HEADS, tq, QK_PAD), lambda b, n, qi, kj, fl: (0, b * nq + qi[n], 0)),
            pl.BlockSpec((N_HEADS, tk, QK_PAD), lambda b, n, qi, kj, fl: (0, b * nk + kj[n], 0)),
            pl.BlockSpec((1, N_HEADS, V_ROWS, tk), lambda b, n, qi, kj, fl: (b * nk + kj[n], 0, 0, 0)),
            pl.BlockSpec((tq, D_ATT), row_map),
        ],
        out_specs=pl.BlockSpec((tq, D_ATT), row_map),
        scratch_shapes=[
            pltpu.VMEM((N_HEADS, 1, tq), F32),
            pltpu.VMEM((N_HEADS, V_ROWS, tq), F32),
            pltpu.VMEM((tk, tq), F32),
            pltpu.VMEM((tk, tq), F32),
            pltpu.VMEM((tk, tq), F32),
            pltpu.VMEM((tk, tq), F32),
        ],
    )
    return pl.pallas_call(
        functools.partial(_attn_kernel, tq=tq, tk=tk),
        grid_spec=grid_spec,
        out_shape=jax.ShapeDtypeStruct((rows, D_ATT), BF16),
        compiler_params=_params(("arbitrary", "arbitrary")),
        name="attn",
    )(jnp.asarray(qi), jnp.asarray(kj), jnp.asarray(flag), q, k, vt, gate)


def _cache_attn_kernel(q_ref, ckv_ref, krt_ref, ckvn_ref, krn_ref, wk_ref, wv_ref, gkn_ref, g_ref, o_ref,
                       lhs_ref, qr_ref, s_ref, s2_ref, sn_ref, m_ref, l_ref, acc_ref, *, t_len, sub):
    kj = pl.program_id(1)
    n_w = N_HEADS * D_NOPE

    @pl.when(kj == 0)
    def _():
        lhs_ref[0:n_w, :] = wk_ref[...]
        for hd in range(N_HEADS):
            qn = (q_ref[hd, :, :D_NOPE].astype(F32) * gkn_ref[...]).astype(BF16)
            lhs_ref[n_w + hd * t_len:n_w + (hd + 1) * t_len, :] = jnp.dot(
                qn, wk_ref[hd * D_NOPE:(hd + 1) * D_NOPE, :], preferred_element_type=F32).astype(BF16)
            qr_ref[hd * t_len:(hd + 1) * t_len, :] = q_ref[hd, :, D_NOPE:D_QK]
        m_ref[...] = jnp.full(m_ref.shape, -jnp.inf, F32)
        l_ref[...] = jnp.zeros(l_ref.shape, F32)
        acc_ref[...] = jnp.zeros(acc_ref.shape, F32)

    def scores(ckv, s_rope, sc_ref):
        both = _dot_nt(lhs_ref[...], ckv)
        for hd in range(N_HEADS):
            kpre = both[hd * D_NOPE:(hd + 1) * D_NOPE]
            r = lax.rsqrt(jnp.sum(kpre * kpre, axis=0, keepdims=True) * (1.0 / D_NOPE) + EPS)
            rows = slice(hd * t_len, (hd + 1) * t_len)
            sc_ref[rows, :] = both[n_w + hd * t_len:n_w + (hd + 1) * t_len] * r + s_rope[rows]

    def softmax_pv(ckv, sc_ref):
        sc = sc_ref[...]
        m_prev = m_ref[...]
        m_new = jnp.maximum(m_prev, jnp.max(sc, axis=1, keepdims=True))
        alpha = jnp.exp2(m_prev - m_new)
        p = jnp.exp2(sc - m_new)
        l_ref[...] = alpha * l_ref[...] + jnp.sum(p, axis=1, keepdims=True)
        acc_ref[...] = alpha * acc_ref[...] + jnp.dot(p.astype(BF16), ckv, preferred_element_type=F32)
        m_ref[...] = m_new

    def sub_ckv(c):
        return ckv_ref[c * sub:(c + 1) * sub, :].astype(BF16)

    def sub_scores(c):
        s_rope = jnp.dot(qr_ref[...], krt_ref[0, :, c * sub:(c + 1) * sub].astype(BF16), preferred_element_type=F32)
        scores(sub_ckv(c), s_rope, (s_ref, s2_ref)[c % 2])

    n_sub = ckv_ref.shape[0] // sub
    sub_scores(0)
    for c in range(n_sub):
        if c + 1 < n_sub:
            sub_scores(c + 1)
        softmax_pv(sub_ckv(c), (s_ref, s2_ref)[c % 2])

    @pl.when(kj == pl.num_programs(1) - 1)
    def _():
        ckv_new = ckvn_ref[...].astype(BF16)
        scores(ckv_new, _dot_nt(qr_ref[...], krn_ref[...].astype(BF16)), sn_ref)
        softmax_pv(ckv_new, sn_ref)
        ctx = (acc_ref[...] * (1.0 / l_ref[...])).astype(BF16)
        for hd in range(N_HEADS):
            att = _dot_nt(ctx[hd * t_len:(hd + 1) * t_len], wv_ref[hd * D_V:(hd + 1) * D_V, :])
            o_ref[:, hd * D_V:(hd + 1) * D_V] = (att * g_ref[:, hd * D_V:(hd + 1) * D_V]).astype(BF16)


def _cache_attn_call(q, ckv_past, krt_past, ckv_new, kr_new, gate, w, batch, t_len, n_past, tk, sub):
    assert n_past % CHUNK == 0 and t_len <= CHUNK and n_past % tk == 0 and tk % sub == 0
    rows = batch * t_len
    nk = n_past // tk
    n_q = N_HEADS * t_len
    const = lambda shape: pl.BlockSpec(shape, lambda b, j: (0,) * len(shape))
    return pl.pallas_call(
        functools.partial(_cache_attn_kernel, t_len=t_len, sub=sub),
        grid=(batch, nk),
        in_specs=[
            pl.BlockSpec((N_HEADS, t_len, QK_PAD), lambda b, j: (0, b, 0)),
            pl.BlockSpec((tk, KV_LORA), lambda b, j: (b * nk + j, 0)),
            pl.BlockSpec((1, D_ROPE, tk), lambda b, j: (b, 0, j)),
            pl.BlockSpec((t_len, KV_LORA), lambda b, j: (b, 0)),
            pl.BlockSpec((t_len, D_ROPE), lambda b, j: (b, 0)),
            const((N_HEADS * D_NOPE, KV_LORA)),
            const((N_HEADS * D_V, KV_LORA)),
            const((1, D_NOPE)),
            pl.BlockSpec((t_len, D_ATT), lambda b, j: (b, 0)),
        ],
        out_specs=pl.BlockSpec((t_len, D_ATT), lambda b, j: (b, 0)),
        out_shape=jax.ShapeDtypeStruct((rows, D_ATT), BF16),
        scratch_shapes=[
            pltpu.VMEM((N_HEADS * D_NOPE + n_q, KV_LORA), BF16),
            pltpu.VMEM((n_q, D_ROPE), BF16),
            pltpu.VMEM((n_q, sub), F32),
            pltpu.VMEM((n_q, sub), F32),
            pltpu.VMEM((n_q, t_len), F32),
            pltpu.VMEM((n_q, 1), F32),
            pltpu.VMEM((n_q, 1), F32),
            pltpu.VMEM((n_q, KV_LORA), F32),
        ],
        compiler_params=_params(("arbitrary", "arbitrary")),
        name="cache_attn",
    )(q, ckv_past, krt_past, ckv_new, kr_new, w["w_uk_t"], w["w_uv"], w["g_kn"], gate)


def _out_kernel(a_ref, p_ref, w_ref, x_ref, y_ref):
    y_ref[...] = x_ref[...] + (jnp.dot(a_ref[...], w_ref[0], preferred_element_type=F32)
                               + jnp.dot(p_ref[...], w_ref[1], preferred_element_type=F32))


def _out_call(att, pool, w_out, x2, tm, tn):
    rows = x2.shape[0]
    return pl.pallas_call(
        _out_kernel,
        grid=(D_MODEL // tn, rows // tm),
        in_specs=[
            pl.BlockSpec((tm, D_ATT), lambda j, i: (i, 0)),
            pl.BlockSpec((tm, D_POOL), lambda j, i: (i, 0)),
            pl.BlockSpec((2, D_ATT, tn), lambda j, i: (0, 0, j)),
            pl.BlockSpec((tm, tn), lambda j, i: (i, j)),
        ],
        out_specs=pl.BlockSpec((tm, tn), lambda j, i: (i, j)),
        out_shape=jax.ShapeDtypeStruct((rows, D_MODEL), F32),
        compiler_params=_params(("arbitrary", "arbitrary")),
        name="out_proj",
    )(att, pool, w_out, x2)


def _rope_tables(pos0, t_len):
    half = D_ROPE // 2
    freqs = ROPE_THETA ** (-jnp.arange(half, dtype=F32) / half)
    ang = (pos0 + jnp.arange(t_len, dtype=jnp.int32)).astype(F32)[:, None] * freqs[None, :]
    cos, sin = jnp.cos(ang), jnp.sin(ang)
    reps = ROPE_PAD // half
    cos, sin = jnp.tile(cos, (1, reps)), jnp.tile(sin, (1, reps))
    is_x1 = (jnp.arange(ROPE_PAD) // half) % 2 == 0
    return cos, jnp.where(is_x1, 0.0, sin), jnp.where(is_x1, -sin, 0.0)


def _prep_weights(g_norm, w_in, g_q_lat, w_uq, g_qn, g_qr, g_kv_lat, g_kr, w_ukv, g_kn,
                  w_pool, pool_scale, w_out):
    row = lambda v: v.reshape(1, -1).astype(F32)
    pad_row = lambda v: jnp.pad(v, (0, ROPE_PAD - D_ROPE)).reshape(1, -1).astype(F32)
    return {
        "g_norm": row(g_norm), "g_q_lat": row(g_q_lat), "g_qn": row(g_qn), "g_qr": row(jnp.tile(g_qr, ROPE_PAD // D_ROPE)),
        "g_kv_lat": row(g_kv_lat), "g_kr": pad_row(g_kr), "g_kn": row(g_kn), "pool_scale": row(pool_scale),
        "w_in": _prep_in_call(jnp.transpose(w_in), TILES.prep_rows),
        "w_uq": _prep_uq_call(jnp.transpose(w_uq, (1, 2, 0))),
        "w_uk": w_ukv[:, :, :D_NOPE].reshape(KV_LORA, N_HEADS * D_NOPE).astype(BF16),
        "w_uk_t": jnp.transpose(w_ukv[:, :, :D_NOPE], (1, 2, 0)).reshape(N_HEADS * D_NOPE, KV_LORA).astype(BF16),
        "w_uv": jnp.transpose(w_ukv[:, :, D_NOPE:], (1, 2, 0)).reshape(N_HEADS * D_V, KV_LORA).astype(BF16),
        "w_pool": w_pool.astype(BF16),
        "w_out": w_out.reshape(2, D_ATT, D_MODEL).astype(BF16),
    }


def _mixer_layer(x, ckv_past, kr_past, pool_past, w):
    batch, t_len, _ = x.shape
    rows = batch * t_len
    pos0 = 0 if ckv_past is None else ckv_past.shape[1]
    x2 = x.reshape(rows, D_MODEL)

    h, q, ckv, kr = _latent_call(x2, _rope_tables(pos0, t_len), w, t_len, TILES.latent_rows)
    gate = _gate_call(h, w["w_in"], TILES.gate_rows, TILES.gate_cols)
    pool, pool_state = _pool_call(h, pool_past, w, batch, t_len, pos0, TILES.pool_rows)
    if ckv_past is None:
        k, vt = _kv_call(ckv, kr, w, TILES.attn_block)
        att = _attn_call(q, k, vt, gate, batch, t_len, TILES.attn_block, TILES.attn_block)
    else:
        att = _cache_attn_call(q, ckv_past.reshape(batch * pos0, KV_LORA), jnp.swapaxes(kr_past, 1, 2), ckv, kr,
                               gate, w, batch, t_len, pos0, TILES.cache_keys, TILES.cache_sub)
    y = _out_call(att, pool, w["w_out"], x2, TILES.out_rows, TILES.out_cols)
    return (y.reshape(batch, t_len, D_MODEL), ckv.reshape(batch, t_len, KV_LORA),
            kr.reshape(batch, t_len, D_ROPE), pool_state)


def kernel(x_prompt, x_sample, cache_ckv, cache_krope, state_pool, g_norm, w_in, g_q_lat, w_uq, g_qn, g_qr,
           g_kv_lat, g_kr, w_ukv, g_kn, w_pool, pool_scale, w_out):
    depth = w_in.shape[0]
    batch = x_prompt.shape[0]
    yp, ys = x_prompt, x_sample
    outs = [[] for _ in range(6)]
    for l in range(depth):
        w = _prep_weights(g_norm[l], w_in[l], g_q_lat[l], w_uq[l], g_qn[l], g_qr[l], g_kv_lat[l], g_kr[l],
                          w_ukv[l], g_kn[l], w_pool[l], pool_scale[l], w_out[l])
        yp, a, b, c = _mixer_layer(yp, None, None, jnp.zeros((batch, POOL_BUF, D_POOL), F32), w)
        outs[0].append(a); outs[1].append(b); outs[2].append(c)
        ys, a, b, c = _mixer_layer(ys, cache_ckv[l], cache_krope[l], state_pool[l], w)
        outs[3].append(a); outs[4].append(b); outs[5].append(c)
    return (yp, ys) + tuple(jnp.stack(o) for o in outs)
```

```python
import functools
from typing import NamedTuple

import numpy as np
import jax
import jax.numpy as jnp
from jax import lax
from jax.experimental import pallas as pl
from jax.experimental.pallas import tpu as pltpu

D_MODEL = 4096
CHUNK = 64
N_HEADS = 16
D_NOPE = 128
D_ROPE = 64
D_QK = D_NOPE + D_ROPE
D_V = 128
D_ATT = N_HEADS * D_V
Q_LORA = 1024
KV_LORA = 512
D_POOL = 2048
POOL_WINDOWS = (2, 4, 8, 16)
N_POOL_GROUPS = len(POOL_WINDOWS)
D_POOL_GROUP = D_POOL // N_POOL_GROUPS
POOL_BUF = max(POOL_WINDOWS) - 1
ROPE_THETA = 10000.0
EPS = 1e-6

O_Z = Q_LORA + KV_LORA + D_ROPE
O_U = O_Z + D_ATT
O_ZP = O_U + D_POOL
D_IN = O_ZP + D_POOL

LANES = 128
SUBLANES = 8
BF16_SUBLANES = 2 * SUBLANES
ROPE_PAD = LANES
QK_PAD = D_NOPE + ROPE_PAD
LAT_PAD = Q_LORA + KV_LORA + ROPE_PAD
HEAD_GROUP = 4
V_ROWS = D_V + BF16_SUBLANES
HIST = 2 * (POOL_BUF + 1)
VMEM_LIMIT = 56 * 1024 * 1024

W_IN_ALIGN = 64

F32 = jnp.float32
BF16 = jnp.bfloat16


class _Tiles(NamedTuple):
    prep_rows: int = D_IN // 22
    latent_rows: int = 256
    gate_rows: int = 1024
    gate_cols: int = 1024
    pool_rows: int = 512
    attn_block: int = 512
    cache_keys: int = 2048
    cache_sub: int = 1024
    out_rows: int = 1024
    out_cols: int = 1024


TILES = _Tiles()
_NT = (((1,), (1,)), ((), ()))


def _params(semantics):
    return pltpu.CompilerParams(dimension_semantics=semantics, vmem_limit_bytes=VMEM_LIMIT)


def _rms(x, g, n=None):
    n = x.shape[-1] if n is None else n
    ms = jnp.sum(x * x, axis=-1, keepdims=True) * (1.0 / n)
    return x * lax.rsqrt(ms + EPS) * g


def _rope(x, cos, sin_hi, sin_lo):
    return x * cos + pltpu.roll(x, D_ROPE // 2, 1) * sin_hi + pltpu.roll(x, ROPE_PAD - D_ROPE // 2, 1) * sin_lo


def _silu(z):
    return z * (1.0 / (1.0 + jnp.exp(-z)))


def _dot_nt(a, b):
    return lax.dot_general(a, b, _NT, preferred_element_type=F32)


def _cast_kernel(w_ref, o_ref):
    o_ref[...] = w_ref[...].astype(BF16)


def _prep_in_call(w_in_t, tr):
    return pl.pallas_call(
        _cast_kernel,
        grid=(D_IN // tr,),
        in_specs=[pl.BlockSpec((tr, D_MODEL), lambda n: (n, 0))],
        out_specs=pl.BlockSpec((tr, D_MODEL), lambda n: (n, 0)),
        out_shape=jax.ShapeDtypeStruct((D_IN, D_MODEL), BF16),
        compiler_params=_params(("arbitrary",)),
        name="prep_w_in",
    )(w_in_t)


def _w_in_window(n_rows, row_of):
    return pl.BlockSpec((pl.Element(n_rows), pl.Element(D_MODEL)),
                        lambda *idx: (pl.multiple_of(row_of(*idx), W_IN_ALIGN), 0))


def _prep_uq_kernel(w_ref, o_ref):
    rope0 = HEAD_GROUP * D_NOPE
    for j in range(HEAD_GROUP):
        o_ref[0, j * D_NOPE:(j + 1) * D_NOPE, :] = w_ref[j, :D_NOPE, :].astype(BF16)
        o_ref[0, rope0 + j * D_ROPE:rope0 + (j + 1) * D_ROPE, :] = w_ref[j, D_NOPE:, :].astype(BF16)


def _prep_uq_call(w_uq_t):
    groups = N_HEADS // HEAD_GROUP
    return pl.pallas_call(
        _prep_uq_kernel,
        grid=(groups,),
        in_specs=[pl.BlockSpec((HEAD_GROUP, D_QK, Q_LORA), lambda g: (g, 0, 0))],
        out_specs=pl.BlockSpec((1, HEAD_GROUP * D_QK, Q_LORA), lambda g: (g, 0, 0)),
        out_shape=jax.ShapeDtypeStruct((groups, HEAD_GROUP * D_QK, Q_LORA), BF16),
        compiler_params=_params(("arbitrary",)),
        name="prep_w_uq",
    )(w_uq_t)


def _latent_kernel(x_ref, gn_ref, wlat_ref, gql_ref, wuq_ref, gqn_ref, gqr_ref, gkv_ref, gkr_ref,
                   cos_ref, shi_ref, slo_ref, h_ref, q_ref, ckv_ref, kr_ref, *, scale):
    h = _rms(x_ref[...], gn_ref[...]).astype(BF16)
    h_ref[...] = h
    c = _dot_nt(h, wlat_ref[...])
    qlat = _rms(c[:, :Q_LORA], gql_ref[...]).astype(BF16)
    ckv_ref[...] = _rms(c[:, Q_LORA:Q_LORA + KV_LORA], gkv_ref[...])
    cos, shi, slo = cos_ref[...], shi_ref[...], slo_ref[...]
    c_kr = c[:, Q_LORA + KV_LORA:]
    c_kr = jnp.where(lax.broadcasted_iota(jnp.int32, c_kr.shape, 1) < D_ROPE, c_kr, 0.0)
    kr = _rope(_rms(c_kr, gkr_ref[...], D_ROPE), cos, shi, slo)
    kr_ref[...] = kr[:, :D_ROPE]
    low = lax.broadcasted_iota(jnp.int32, c_kr.shape, 1) < D_ROPE
    rope0 = HEAD_GROUP * D_NOPE
    for g in range(N_HEADS // HEAD_GROUP):
        q = _dot_nt(qlat, wuq_ref[g])
        for j in range(HEAD_GROUP):
            qn = _rms(q[:, j * D_NOPE:(j + 1) * D_NOPE], gqn_ref[...]) * scale
            q_ref[g * HEAD_GROUP + j, :, :D_NOPE] = qn.astype(BF16)
        for t in range(HEAD_GROUP // 2):
            x = q[:, rope0 + t * ROPE_PAD:rope0 + (t + 1) * ROPE_PAD]
            sq = x * x
            ms_lo = jnp.sum(jnp.where(low, sq, 0.0), axis=-1, keepdims=True) * (1.0 / D_ROPE)
            ms_hi = jnp.sum(jnp.where(low, 0.0, sq), axis=-1, keepdims=True) * (1.0 / D_ROPE)
            r = jnp.where(low, lax.rsqrt(ms_lo + EPS), lax.rsqrt(ms_hi + EPS))
            y = _rope(x * r * gqr_ref[...], cos, shi, slo) * scale
            hd = g * HEAD_GROUP + 2 * t
            q_ref[hd, :, D_NOPE:] = jnp.where(low, y, 0.0).astype(BF16)
            q_ref[hd + 1, :, D_NOPE:] = jnp.where(low, pltpu.roll(y, D_ROPE, 1), 0.0).astype(BF16)


def _latent_call(x2, tabs, w, t_len, tm):
    rows = x2.shape[0]
    if tm > t_len:
        tabs = [jnp.tile(t, (tm // t_len, 1)) for t in tabs]
        t_len = tm
    nt = t_len // tm
    resident = lambda shape: pl.BlockSpec(shape, lambda i: (0,) * len(shape), pipeline_mode=pl.Buffered(1))
    tab_spec = pl.BlockSpec((tm, ROPE_PAD), lambda i: (i % nt, 0))
    return pl.pallas_call(
        functools.partial(_latent_kernel, scale=float(D_QK ** -0.5 * np.log2(np.e))),
        grid=(rows // tm,),
        in_specs=[
            pl.BlockSpec((tm, D_MODEL), lambda i: (i, 0)),
            resident((1, D_MODEL)),
            resident((LAT_PAD, D_MODEL)),
            resident((1, Q_LORA)),
            resident((N_HEADS // HEAD_GROUP, HEAD_GROUP * D_QK, Q_LORA)),
            resident((1, D_NOPE)),
            resident((1, ROPE_PAD)),
            resident((1, KV_LORA)),
            resident((1, ROPE_PAD)),
            tab_spec, tab_spec, tab_spec,
        ],
        out_specs=[
            pl.BlockSpec((tm, D_MODEL), lambda i: (i, 0)),
            pl.BlockSpec((N_HEADS, tm, QK_PAD), lambda i: (0, i, 0)),
            pl.BlockSpec((tm, KV_LORA), lambda i: (i, 0)),
            pl.BlockSpec((tm, D_ROPE), lambda i: (i, 0)),
        ],
        out_shape=[
            jax.ShapeDtypeStruct((rows, D_MODEL), BF16),
            jax.ShapeDtypeStruct((N_HEADS, rows, QK_PAD), BF16),
            jax.ShapeDtypeStruct((rows, KV_LORA), F32),
            jax.ShapeDtypeStruct((rows, D_ROPE), F32),
        ],
        compiler_params=_params(("arbitrary",)),
        name="latent",
    )(x2, w["g_norm"], w["w_in"], w["g_q_lat"], w["w_uq"], w["g_qn"], w["g_qr"], w["g_kv_lat"], w["g_kr"],
      *tabs)


def _gate_kernel(h_ref, w_ref, o_ref):
    o_ref[...] = _silu(_dot_nt(h_ref[...], w_ref[...]))


def _gate_call(h, w_all, tm, tn):
    rows = h.shape[0]
    return pl.pallas_call(
        _gate_kernel,
        grid=(D_ATT // tn, rows // tm),
        in_specs=[pl.BlockSpec((tm, D_MODEL), lambda j, i: (i, 0)),
                  _w_in_window(tn, lambda j, i: O_Z + j * tn)],
        out_specs=pl.BlockSpec((tm, tn), lambda j, i: (i, j)),
        out_shape=jax.ShapeDtypeStruct((rows, D_ATT), F32),
        compiler_params=_params(("arbitrary", "arbitrary")),
        name="gate",
    )(h, w_all)


def _pool_kernel(*refs, nb, tt, pos0, lookahead):
    if lookahead:
        h_ref, hn_ref, wu_ref, wz_ref, past_ref, wp_ref, ps_ref, o_ref, st_ref, ext_ref, b1_ref, b2_ref, p_ref, un_ref = refs
    else:
        h_ref, wu_ref, wz_ref, past_ref, wp_ref, ps_ref, o_ref, st_ref, ext_ref, b1_ref, b2_ref, p_ref = refs
    g = pl.program_id(0)
    i = pl.program_id(2)

    @pl.when(i == 0)
    def _():
        for b in range(nb):
            ext_ref[b, 0:HIST - POOL_BUF, :] = jnp.zeros((HIST - POOL_BUF, D_POOL_GROUP), F32)
            ext_ref[b, HIST - POOL_BUF:HIST, :] = past_ref[b]

    first = jnp.logical_and(pl.program_id(1) == 0, i == 0) if lookahead else i == 0

    @pl.when(first)
    def _():
        u0 = _dot_nt(h_ref[...], wu_ref[...])
        for b in range(nb):
            ext_ref[b, HIST:HIST + tt, :] = u0[b * tt:(b + 1) * tt]

    if lookahead:
        @pl.when(i > 0)
        def _():
            for b in range(nb):
                ext_ref[b, 0:HIST, :] = ext_ref[b, tt:tt + HIST, :]

        @pl.when(jnp.logical_not(first))
        def _():
            for b in range(nb):
                ext_ref[b, HIST:HIST + tt, :] = un_ref[b * tt:(b + 1) * tt, :]

    pos = pos0 + i * tt + lax.broadcasted_iota(jnp.int32, (tt, 1), 0)

    for gi, win in enumerate(POOL_WINDOWS):
        @pl.when(g == gi)
        def _(win=win):
            steps = win.bit_length() - 1
            src = ext_ref
            for k in range(steps):
                dst = (b1_ref, b2_ref)[k % 2]
                lo, sh = SUBLANES * (k + 1), 1 << k
                for b in range(nb):
                    dst[b, lo:HIST + tt, :] = src[b, lo:HIST + tt, :] + src[b, lo - sh:HIST + tt - sh, :]
                src = dst
            cnt = jnp.minimum(pos + 1, win).astype(F32)
            for b in range(nb):
                p_ref[b * tt:(b + 1) * tt, :] = (src[b, HIST:HIST + tt, :] / cnt
                                                 - ext_ref[b, HIST:HIST + tt, :]).astype(BF16)
            gate = _silu(_dot_nt(h_ref[...], wz_ref[...]))
            if lookahead:
                un_ref[...] = _dot_nt(hn_ref[...], wu_ref[...])
            ph = jnp.dot(p_ref[...], wp_ref[0], preferred_element_type=F32)
            o_ref[...] = (ph * ps_ref[...] * gate).astype(BF16)

    for b in range(nb):
        st_ref[b] = ext_ref[b, HIST + tt - POOL_BUF:HIST + tt, :]


def _pool_call(h, pool_past, w, batch, t_len, pos0, tm):
    rows = h.shape[0]
    tt = min(tm, t_len)
    nb = tm // tt
    nt = t_len // tt
    buf = pltpu.VMEM((nb, tt + HIST, D_POOL_GROUP), F32)
    lookahead = nt > 1
    h_specs = [pl.BlockSpec((tm, D_MODEL), lambda g, b, i: (b * nt + i, 0))]
    scratch = [buf, buf, buf, pltpu.VMEM((tm, D_POOL_GROUP), BF16)]
    if lookahead:
        h_specs.append(pl.BlockSpec((tm, D_MODEL), lambda g, b, i: (jnp.minimum(b * nt + i + 1, rows // tm - 1), 0)))
        scratch.append(pltpu.VMEM((tm, D_POOL_GROUP), F32))
    return pl.pallas_call(
        functools.partial(_pool_kernel, nb=nb, tt=tt, pos0=pos0, lookahead=lookahead),
        grid=(N_POOL_GROUPS, batch // nb, nt),
        in_specs=h_specs + [
            _w_in_window(D_POOL_GROUP, lambda g, b, i: O_U + g * D_POOL_GROUP),
            _w_in_window(D_POOL_GROUP, lambda g, b, i: O_ZP + g * D_POOL_GROUP),
            pl.BlockSpec((nb, POOL_BUF, D_POOL_GROUP), lambda g, b, i: (b, 0, g)),
            pl.BlockSpec((1, D_POOL_GROUP, D_POOL_GROUP), lambda g, b, i: (g, 0, 0)),
            pl.BlockSpec((1, D_POOL_GROUP), lambda g, b, i: (0, g)),
        ],
        out_specs=[
            pl.BlockSpec((tm, D_POOL_GROUP), lambda g, b, i: (b * nt + i, g)),
            pl.BlockSpec((nb, POOL_BUF, D_POOL_GROUP), lambda g, b, i: (b, 0, g)),
        ],
        out_shape=[
            jax.ShapeDtypeStruct((rows, D_POOL), BF16),
            jax.ShapeDtypeStruct((batch, POOL_BUF, D_POOL), F32),
        ],
        scratch_shapes=scratch,
        compiler_params=_params(("arbitrary", "arbitrary", "arbitrary")),
        name="pool",
    )(*([h] * len(h_specs)), w["w_in"], w["w_in"], pool_past, w["w_pool"], w["pool_scale"])


def _kv_kernel(ckv_ref, kr_ref, wk_ref, wv_ref, gkn_ref, k_ref, v_ref):
    ckv = ckv_ref[...].astype(BF16)
    kr = kr_ref[...].astype(BF16)
    tm = ckv.shape[0]
    pair = 2 * D_NOPE
    for j in range(N_HEADS // 2):
        kk = jnp.dot(ckv, wk_ref[:, j * pair:(j + 1) * pair], preferred_element_type=F32)
        for t in range(2):
            hd = 2 * j + t
            k_ref[hd, :, :D_NOPE] = _rms(kk[:, t * D_NOPE:(t + 1) * D_NOPE], gkn_ref[...]).astype(BF16)
            k_ref[hd, :, D_NOPE:D_QK] = kr
            k_ref[hd, :, D_QK:] = jnp.zeros((tm, QK_PAD - D_QK), BF16)
    vt = _dot_nt(wv_ref[...], ckv)
    ones_row = (lax.broadcasted_iota(jnp.int32, (V_ROWS - D_V, tm), 0) == 0).astype(BF16)
    for hd in range(N_HEADS):
        v_ref[0, hd, :D_V, :] = vt[hd * D_V:(hd + 1) * D_V].astype(BF16)
        v_ref[0, hd, D_V:, :] = ones_row


def _kv_call(ckv, kr, w, tm):
    rows = ckv.shape[0]
    return pl.pallas_call(
        _kv_kernel,
        grid=(rows // tm,),
        in_specs=[
            pl.BlockSpec((tm, KV_LORA), lambda i: (i, 0)),
            pl.BlockSpec((tm, D_ROPE), lambda i: (i, 0)),
            pl.BlockSpec((KV_LORA, N_HEADS * D_NOPE), lambda i: (0, 0)),
            pl.BlockSpec((N_HEADS * D_V, KV_LORA), lambda i: (0, 0)),
            pl.BlockSpec((1, D_NOPE), lambda i: (0, 0)),
        ],
        out_specs=[
            pl.BlockSpec((N_HEADS, tm, QK_PAD), lambda i: (0, i, 0)),
            pl.BlockSpec((1, N_HEADS, V_ROWS, tm), lambda i: (i, 0, 0, 0)),
        ],
        out_shape=[
            jax.ShapeDtypeStruct((N_HEADS, rows, QK_PAD), BF16),
            jax.ShapeDtypeStruct((rows // tm, N_HEADS, V_ROWS, tm), BF16),
        ],
        compiler_params=_params(("arbitrary",)),
        name="kv",
    )(ckv, kr, w["w_uk"], w["w_uv"], w["g_kn"])


def _attn_kernel(qi_ref, kj_ref, flag_ref, q_ref, k_ref, v_ref, g_ref, o_ref,
                 m_ref, acc_ref, bias_ref, sa_ref, sb_ref, sc_ref, *, tq, tk):
    n = pl.program_id(1)
    qi, kj, flag = qi_ref[n], kj_ref[n], flag_ref[n]
    masked = (flag & 1) == 1
    last = (flag & 2) == 2

    @pl.when(kj == 0)
    def _():
        m_ref[...] = jnp.full(m_ref.shape, -jnp.inf, F32)
        acc_ref[...] = jnp.zeros(acc_ref.shape, F32)

    def run_heads(scores, softmax_pv):
        bufs = (sa_ref, sb_ref, sc_ref)
        scores(0, bufs[0])
        scores(1, bufs[1])
        for hd in range(N_HEADS):
            if hd + 2 < N_HEADS:
                scores(hd + 2, bufs[(hd + 2) % 3])
            softmax_pv(hd, bufs[hd % 3])

    def full_block():
        def scores(hd, s_ref):
            s_ref[...] = _dot_nt(k_ref[hd], q_ref[hd])

        def softmax_pv(hd, s_ref):
            m_prev = m_ref[hd]
            m_new = jnp.maximum(m_prev, jnp.max(s_ref[...], axis=0, keepdims=True))
            p = jnp.exp2(s_ref[...] - m_new)
            acc_ref[hd] = jnp.exp2(m_prev - m_new) * acc_ref[hd] + jnp.dot(
                v_ref[0, hd], p.astype(BF16), preferred_element_type=F32)
            m_ref[hd] = m_new

        run_heads(scores, softmax_pv)

    def diagonal_block():
        half = tk // 2

        def scores(hd, s_ref):
            s_ref[:half, :] = _dot_nt(k_ref[hd, :half, :], q_ref[hd]) + bias_ref[:half, :]
            s_ref[half:, half:] = _dot_nt(k_ref[hd, half:, :], q_ref[hd, half:, :]) + bias_ref[half:, half:]

        def softmax_pv(hd, s_ref):
            lo, hi = slice(0, half), slice(half, tq)
            m_lo = jnp.maximum(m_ref[hd, :, lo], jnp.max(s_ref[:half, lo], axis=0, keepdims=True))
            m_hi = jnp.maximum(m_ref[hd, :, hi], jnp.maximum(jnp.max(s_ref[:half, hi], axis=0, keepdims=True),
                                                            jnp.max(s_ref[half:, hi], axis=0, keepdims=True)))
            p_lo = jnp.exp2(s_ref[:half, lo] - m_lo)
            p_hi = jnp.exp2(s_ref[:, hi] - m_hi)
            acc_ref[hd, :, lo] = jnp.exp2(m_ref[hd, :, lo] - m_lo) * acc_ref[hd, :, lo] + jnp.dot(
                v_ref[0, hd, :, :half], p_lo.astype(BF16), preferred_element_type=F32)
            acc_ref[hd, :, hi] = jnp.exp2(m_ref[hd, :, hi] - m_hi) * acc_ref[hd, :, hi] + jnp.dot(
                v_ref[0, hd], p_hi.astype(BF16), preferred_element_type=F32)
            m_ref[hd, :, lo] = m_lo
            m_ref[hd, :, hi] = m_hi

        run_heads(scores, softmax_pv)

    @pl.when(n == 0)
    def _():
        shift = CHUNK.bit_length() - 1
        kc = lax.broadcasted_iota(jnp.int32, (tk, 1), 0) >> shift
        qc = lax.broadcasted_iota(jnp.int32, (1, tq), 1) >> shift
        bias_ref[...] = jnp.where(kc <= qc, 0.0, -jnp.inf).astype(F32)

    @pl.when(masked)
    def _():
        diagonal_block()

    @pl.when(jnp.logical_not(masked))
    def _():
        full_block()

    @pl.when(last)
    def _():
        for hd in range(N_HEADS):
            att = (acc_ref[hd, :D_V, :] * (1.0 / acc_ref[hd, D_V:D_V + 1, :])).T
            o_ref[:, hd * D_V:(hd + 1) * D_V] = (att * g_ref[:, hd * D_V:(hd + 1) * D_V]).astype(BF16)


def _attn_schedule(t_len, tq, tk):
    qi, kj, flag = [], [], []
    for i in range(t_len // tq):
        first_chunk = (i * tq) // CHUNK
        last_blk = ((i + 1) * tq - 1) // tk
        for j in range(last_blk + 1):
            needs_mask = ((j + 1) * tk - 1) // CHUNK > first_chunk
            qi.append(i)
            kj.append(j)
            flag.append(int(needs_mask) + 2 * int(j == last_blk))
    return tuple(np.asarray(a, np.int32) for a in (qi, kj, flag))


def _attn_call(q, k, vt, gate, batch, t_len, tq, tk):
    assert tq == tk and tq % (2 * CHUNK) == 0 and tq % (2 * LANES) == 0 and t_len % tq == 0
    rows = batch * t_len
    nq, nk = t_len // tq, t_len // tk
    qi, kj, flag = _attn_schedule(t_len, tq, tk)
    row_map = lambda b, n, qi, kj, fl: (b * nq + qi[n], 0)
    grid_spec = pltpu.PrefetchScalarGridSpec(
        num_scalar_prefetch=3,
        grid=(batch, len(qi)),
        in_specs=[
            pl.BlockSpec((N_HEADS, tq, QK_PAD), lambda b, n, qi, kj, fl: (0, b * nq + qi[n], 0)),
            pl.BlockSpec((N_HEADS, tk, QK_PAD), lambda b, n, qi, kj, fl: (0, b * nk + kj[n], 0)),
            pl.BlockSpec((1, N_HEADS, V_ROWS, tk), lambda b, n, qi, kj, fl: (b * nk + kj[n], 0, 0, 0)),
            pl.BlockSpec((tq, D_ATT), row_map),
        ],
        out_specs=pl.BlockSpec((tq, D_ATT), row_map),
        scratch_shapes=[
            pltpu.VMEM((N_HEADS, 1, tq), F32),
            pltpu.VMEM((N_HEADS, V_ROWS, tq), F32),
            pltpu.VMEM((tk, tq), F32),
            pltpu.VMEM((tk, tq), F32),
            pltpu.VMEM((tk, tq), F32),
            pltpu.VMEM((tk, tq), F32),
        ],
    )
    return pl.pallas_call(
        functools.partial(_attn_kernel, tq=tq, tk=tk),
        grid_spec=grid_spec,
        out_shape=jax.ShapeDtypeStruct((rows, D_ATT), BF16),
        compiler_params=_params(("arbitrary", "arbitrary")),
        name="attn",
    )(jnp.asarray(qi), jnp.asarray(kj), jnp.asarray(flag), q, k, vt, gate)


def _cache_attn_kernel(q_ref, ckv_ref, krt_ref, ckvn_ref, krn_ref, wk_ref, wv_ref, gkn_ref, g_ref, o_ref,
                       lhs_ref, qr_ref, s_ref, s2_ref, sn_ref, m_ref, l_ref, acc_ref, *, t_len, sub):
    kj = pl.program_id(1)
    n_w = N_HEADS * D_NOPE

    @pl.when(kj == 0)
    def _():
        lhs_ref[0:n_w, :] = wk_ref[...]
        for hd in range(N_HEADS):
            qn = (q_ref[hd, :, :D_NOPE].astype(F32) * gkn_ref[...]).astype(BF16)
            lhs_ref[n_w + hd * t_len:n_w + (hd + 1) * t_len, :] = jnp.dot(
                qn, wk_ref[hd * D_NOPE:(hd + 1) * D_NOPE, :], preferred_element_type=F32).astype(BF16)
            qr_ref[hd * t_len:(hd + 1) * t_len, :] = q_ref[hd, :, D_NOPE:D_QK]
        m_ref[...] = jnp.full(m_ref.shape, -jnp.inf, F32)
        l_ref[...] = jnp.zeros(l_ref.shape, F32)
        acc_ref[...] = jnp.zeros(acc_ref.shape, F32)

    def scores(ckv, s_rope, sc_ref):
        both = _dot_nt(lhs_ref[...], ckv)
        for hd in range(N_HEADS):
            kpre = both[hd * D_NOPE:(hd + 1) * D_NOPE]
            r = lax.rsqrt(jnp.sum(kpre * kpre, axis=0, keepdims=True) * (1.0 / D_NOPE) + EPS)
            rows = slice(hd * t_len, (hd + 1) * t_len)
            sc_ref[rows, :] = both[n_w + hd * t_len:n_w + (hd + 1) * t_len] * r + s_rope[rows]

    def softmax_pv(ckv, sc_ref):
        sc = sc_ref[...]
        m_prev = m_ref[...]
        m_new = jnp.maximum(m_prev, jnp.max(sc, axis=1, keepdims=True))
        alpha = jnp.exp2(m_prev - m_new)
        p = jnp.exp2(sc - m_new)
        l_ref[...] = alpha * l_ref[...] + jnp.sum(p, axis=1, keepdims=True)
        acc_ref[...] = alpha * acc_ref[...] + jnp.dot(p.astype(BF16), ckv, preferred_element_type=F32)
        m_ref[...] = m_new

    def sub_ckv(c):
        return ckv_ref[c * sub:(c + 1) * sub, :].astype(BF16)

    def sub_scores(c):
        s_rope = jnp.dot(qr_ref[...], krt_ref[0, :, c * sub:(c + 1) * sub].astype(BF16), preferred_element_type=F32)
        scores(sub_ckv(c), s_rope, (s_ref, s2_ref)[c % 2])

    n_sub = ckv_ref.shape[0] // sub
    sub_scores(0)
    for c in range(n_sub):
        if c + 1 < n_sub:
            sub_scores(c + 1)
        softmax_pv(sub_ckv(c), (s_ref, s2_ref)[c % 2])

    @pl.when(kj == pl.num_programs(1) - 1)
    def _():
        ckv_new = ckvn_ref[...].astype(BF16)
        scores(ckv_new, _dot_nt(qr_ref[...], krn_ref[...].astype(BF16)), sn_ref)
        softmax_pv(ckv_new, sn_ref)
        ctx = (acc_ref[...] * (1.0 / l_ref[...])).astype(BF16)
        for hd in range(N_HEADS):
            att = _dot_nt(ctx[hd * t_len:(hd + 1) * t_len], wv_ref[hd * D_V:(hd + 1) * D_V, :])
            o_ref[:, hd * D_V:(hd + 1) * D_V] = (att * g_ref[:, hd * D_V:(hd + 1) * D_V]).astype(BF16)


def _cache_attn_call(q, ckv_past, krt_past, ckv_new, kr_new, gate, w, batch, t_len, n_past, tk, sub):
    assert n_past % CHUNK == 0 and t_len <= CHUNK and n_past % tk == 0 and tk % sub == 0
    rows = batch * t_len
    nk = n_past // tk
    n_q = N_HEADS * t_len
    const = lambda shape: pl.BlockSpec(shape, lambda b, j: (0,) * len(shape))
    return pl.pallas_call(
        functools.partial(_cache_attn_kernel, t_len=t_len, sub=sub),
        grid=(batch, nk),
        in_specs=[
            pl.BlockSpec((N_HEADS, t_len, QK_PAD), lambda b, j: (0, b, 0)),
            pl.BlockSpec((tk, KV_LORA), lambda b, j: (b * nk + j, 0)),
            pl.BlockSpec((1, D_ROPE, tk), lambda b, j: (b, 0, j)),
            pl.BlockSpec((t_len, KV_LORA), lambda b, j: (b, 0)),
            pl.BlockSpec((t_len, D_ROPE), lambda b, j: (b, 0)),
            const((N_HEADS * D_NOPE, KV_LORA)),
            const((N_HEADS * D_V, KV_LORA)),
            const((1, D_NOPE)),
            pl.BlockSpec((t_len, D_ATT), lambda b, j: (b, 0)),
        ],
        out_specs=pl.BlockSpec((t_len, D_ATT), lambda b, j: (b, 0)),
        out_shape=jax.ShapeDtypeStruct((rows, D_ATT), BF16),
        scratch_shapes=[
            pltpu.VMEM((N_HEADS * D_NOPE + n_q, KV_LORA), BF16),
            pltpu.VMEM((n_q, D_ROPE), BF16),
            pltpu.VMEM((n_q, sub), F32),
            pltpu.VMEM((n_q, sub), F32),
            pltpu.VMEM((n_q, t_len), F32),
            pltpu.VMEM((n_q, 1), F32),
            pltpu.VMEM((n_q, 1), F32),
            pltpu.VMEM((n_q, KV_LORA), F32),
        ],
        compiler_params=_params(("arbitrary", "arbitrary")),
        name="cache_attn",
    )(q, ckv_past, krt_past, ckv_new, kr_new, w["w_uk_t"], w["w_uv"], w["g_kn"], gate)


def _out_kernel(a_ref, p_ref, w_ref, x_ref, y_ref):
    y_ref[...] = x_ref[...] + (jnp.dot(a_ref[...], w_ref[0], preferred_element_type=F32)
                               + jnp.dot(p_ref[...], w_ref[1], preferred_element_type=F32))


def _out_call(att, pool, w_out, x2, tm, tn):
    rows = x2.shape[0]
    return pl.pallas_call(
        _out_kernel,
        grid=(D_MODEL // tn, rows // tm),
        in_specs=[
            pl.BlockSpec((tm, D_ATT), lambda j, i: (i, 0)),
            pl.BlockSpec((tm, D_POOL), lambda j, i: (i, 0)),
            pl.BlockSpec((2, D_ATT, tn), lambda j, i: (0, 0, j)),
            pl.BlockSpec((tm, tn), lambda j, i: (i, j)),
        ],
        out_specs=pl.BlockSpec((tm, tn), lambda j, i: (i, j)),
        out_shape=jax.ShapeDtypeStruct((rows, D_MODEL), F32),
        compiler_params=_params(("arbitrary", "arbitrary")),
        name="out_proj",
    )(att, pool, w_out, x2)


def _rope_tables(pos0, t_len):
    half = D_ROPE // 2
    freqs = ROPE_THETA ** (-jnp.arange(half, dtype=F32) / half)
    ang = (pos0 + jnp.arange(t_len, dtype=jnp.int32)).astype(F32)[:, None] * freqs[None, :]
    cos, sin = jnp.cos(ang), jnp.sin(ang)
    reps = ROPE_PAD // half
    cos, sin = jnp.tile(cos, (1, reps)), jnp.tile(sin, (1, reps))
    is_x1 = (jnp.arange(ROPE_PAD) // half) % 2 == 0
    return cos, jnp.where(is_x1, 0.0, sin), jnp.where(is_x1, -sin, 0.0)


def _prep_weights(g_norm, w_in, g_q_lat, w_uq, g_qn, g_qr, g_kv_lat, g_kr, w_ukv, g_kn,
                  w_pool, pool_scale, w_out):
    row = lambda v: v.reshape(1, -1).astype(F32)
    pad_row = lambda v: jnp.pad(v, (0, ROPE_PAD - D_ROPE)).reshape(1, -1).astype(F32)
    return {
        "g_norm": row(g_norm), "g_q_lat": row(g_q_lat), "g_qn": row(g_qn), "g_qr": row(jnp.tile(g_qr, ROPE_PAD // D_ROPE)),
        "g_kv_lat": row(g_kv_lat), "g_kr": pad_row(g_kr), "g_kn": row(g_kn), "pool_scale": row(pool_scale),
        "w_in": _prep_in_call(jnp.transpose(w_in), TILES.prep_rows),
        "w_uq": _prep_uq_call(jnp.transpose(w_uq, (1, 2, 0))),
        "w_uk": w_ukv[:, :, :D_NOPE].reshape(KV_LORA, N_HEADS * D_NOPE).astype(BF16),
        "w_uk_t": jnp.transpose(w_ukv[:, :, :D_NOPE], (1, 2, 0)).reshape(N_HEADS * D_NOPE, KV_LORA).astype(BF16),
        "w_uv": jnp.transpose(w_ukv[:, :, D_NOPE:], (1, 2, 0)).reshape(N_HEADS * D_V, KV_LORA).astype(BF16),
        "w_pool": w_pool.astype(BF16),
        "w_out": w_out.reshape(2, D_ATT, D_MODEL).astype(BF16),
    }


def _mixer_layer(x, ckv_past, kr_past, pool_past, w):
    batch, t_len, _ = x.shape
    rows = batch * t_len
    pos0 = 0 if ckv_past is None else ckv_past.shape[1]
    x2 = x.reshape(rows, D_MODEL)

    h, q, ckv, kr = _latent_call(x2, _rope_tables(pos0, t_len), w, t_len, TILES.latent_rows)
    gate = _gate_call(h, w["w_in"], TILES.gate_rows, TILES.gate_cols)
    pool, pool_state = _pool_call(h, pool_past, w, batch, t_len, pos0, TILES.pool_rows)
    if ckv_past is None:
        k, vt = _kv_call(ckv, kr, w, TILES.attn_block)
        att = _attn_call(q, k, vt, gate, batch, t_len, TILES.attn_block, TILES.attn_block)
    else:
        att = _cache_attn_call(q, ckv_past.reshape(batch * pos0, KV_LORA), jnp.swapaxes(kr_past, 1, 2), ckv, kr,
                               gate, w, batch, t_len, pos0, TILES.cache_keys, TILES.cache_sub)
    y = _out_call(att, pool, w["w_out"], x2, TILES.out_rows, TILES.out_cols)
    return (y.reshape(batch, t_len, D_MODEL), ckv.reshape(batch, t_len, KV_LORA),
            kr.reshape(batch, t_len, D_ROPE), pool_state)


def kernel(x_prompt, x_sample, cache_ckv, cache_krope, state_pool, g_norm, w_in, g_q_lat, w_uq, g_qn, g_qr,
           g_kv_lat, g_kr, w_ukv, g_kn, w_pool, pool_scale, w_out):
    depth = w_in.shape[0]
    batch = x_prompt.shape[0]
    yp, ys = x_prompt, x_sample
    outs = [[] for _ in range(6)]
    for l in range(depth):
        w = _prep_weights(g_norm[l], w_in[l], g_q_lat[l], w_uq[l], g_qn[l], g_qr[l], g_kv_lat[l], g_kr[l],
                          w_ukv[l], g_kn[l], w_pool[l], pool_scale[l], w_out[l])
        yp, a, b, c = _mixer_layer(yp, None, None, jnp.zeros((batch, POOL_BUF, D_POOL), F32), w)
        outs[0].append(a); outs[1].append(b); outs[2].append(c)
        ys, a, b, c = _mixer_layer(ys, cache_ckv[l], cache_krope[l], state_pool[l], w)
        outs[3].append(a); outs[4].append(b); outs[5].append(c)
    return (yp, ys) + tuple(jnp.stack(o) for o in outs)
```
